```python
import math
import jax
import jax.numpy as jnp
from jax import lax
import numpy as np

D_MODEL = 1024
BATCH = 1
SEQ = 16384
DEPTH = 2
DEC_BATCH = 32
DEC_SEQ = 4
PAST_LEN = 16384
PAGE_SIZE = 128

N_A_LAYERS = DEPTH // 2
N_B_LAYERS = DEPTH - N_A_LAYERS
NH_A = 4
DK_A = D_MODEL // NH_A
DV_A = D_MODEL // NH_A
CHUNK_A = 64
F_BIAS_LO = 3.0
F_BIAS_HI = 6.0
NH_B = 16
HD_B = 64
NKV_B = 4
REP_B = NH_B // NKV_B
BLOCK_B = 64
TOPK_B = 16
WINDOW_B = 512
CMP_HIDDEN = 256
Q_BLOCK = 128
NUM_BUCKETS = 32
REL_MAX_DIST = 2048
D_FF = 4 * D_MODEL
EPS = 1e-6
NEG_BIG = -1e30

kernel_name = 'yoco_mlstm_nsa_decoder_step'


def rmsnorm(x, gain):
    x32 = x.astype(jnp.float32)
    y = x32 * lax.rsqrt(jnp.mean(x32 * x32, axis=-1, keepdims=True) + EPS)
    return (y * gain.astype(jnp.float32)).astype(x.dtype)


def modulate(x, gain, shift, scale):
    return rmsnorm(x, gain) * (1 + scale[:, None, :]) + shift[:, None, :]


def rel_bucket(dist):
    dist = jnp.maximum(dist, 0)
    exact = NUM_BUCKETS // 2
    d = jnp.maximum(dist, 1).astype(jnp.float32)
    big = exact + (jnp.log(d / exact) / math.log(REL_MAX_DIST / exact) * (NUM_BUCKETS - exact)).astype(jnp.int32)
    big = jnp.minimum(big, NUM_BUCKETS - 1)
    return jnp.where(dist < exact, dist, big)


def mlstm_scan(q, k, v, log_i, log_f, C0, n0, m0):
    B, H, T, _ = q.shape
    L = math.gcd(T, CHUNK_A)
    nc = T // L

    def split(a):
        return jnp.moveaxis(a.reshape(B, H, nc, L, *a.shape[3:]), 2, 0)

    causal = jnp.tril(jnp.ones((L, L), dtype=bool))

    def step(carry, xs):
        C, n, m = carry
        qc, kc, vc, li, lf = xs
        b = jnp.cumsum(lf, axis=-1)
        D = jnp.where(causal, b[..., :, None] - b[..., None, :] + li[..., None, :], -jnp.inf)
        m_inter = b + m[..., None]
        m_t = jnp.maximum(m_inter, jnp.max(D, axis=-1))
        S = jnp.einsum('bhtd,bhsd->bhts', qc, kc) * jnp.exp(D - m_t[..., None])
        a_inter = jnp.exp(m_inter - m_t)
        num = jnp.einsum('bhts,bhsv->bhtv', S, vc) + a_inter[..., None] * jnp.einsum('bhtd,bhdv->bhtv', qc, C)
        den = jnp.sum(S, axis=-1) + a_inter * jnp.einsum('bhtd,bhd->bht', qc, n)
        h = num / jnp.maximum(jnp.abs(den), jnp.exp(-m_t))[..., None]
        bL = b[..., -1]
        g = bL[..., None] - b + li
        m_new = jnp.maximum(bL + m, jnp.max(g, axis=-1))
        w = jnp.exp(g - m_new[..., None])
        decay = jnp.exp(bL + m - m_new)
        C_new = decay[..., None, None] * C + jnp.einsum('bhs,bhsd,bhsv->bhdv', w, kc, vc)
        n_new = decay[..., None] * n + jnp.einsum('bhs,bhsd->bhd', w, kc)
        return (C_new, n_new, m_new), h

    (C, n, m), hs = lax.scan(step, (C0, n0, m0), (split(q), split(k), split(v), split(log_i), split(log_f)))
    h = jnp.moveaxis(hs, 0, 2).reshape(B, H, T, -1)
    return h, C, n, m


def mlstm_mixer(xn, w_in, b_if, head_gain, w_out, init):
    B, T, _ = xn.shape
    inner = NH_A * DK_A
    proj = xn @ w_in

    def heads(a, d):
        return jnp.transpose(a.reshape(B, T, NH_A, d).astype(jnp.float32), (0, 2, 1, 3))

    q = heads(proj[..., :inner], DK_A) * (DK_A ** -0.5)
    k = heads(proj[..., inner:2 * inner], DK_A)
    v = heads(proj[..., 2 * inner:3 * inner], DV_A)
    o = proj[..., 3 * inner:4 * inner]
    g = proj[..., 4 * inner:].astype(jnp.float32) + b_if.astype(jnp.float32)
    log_i = jnp.transpose(g[..., :NH_A], (0, 2, 1))
    log_f = jnp.transpose(jax.nn.log_sigmoid(g[..., NH_A:]), (0, 2, 1))
    C0, n0, m0 = (s.astype(jnp.float32) for s in init)
    h, C, n, m = mlstm_scan(q, k, v, log_i, log_f, C0, n0, m0)
    h = rmsnorm(jnp.transpose(h, (0, 2, 1, 3)), head_gain).astype(xn.dtype)
    y = (h * jax.nn.sigmoid(o).reshape(B, T, NH_A, DV_A)).reshape(B, T, inner)
    return y @ w_out, (C, n, m)


def shared_kv(hn, w_kv, k_norm):
    B, T, _ = hn.shape
    kv = (hn @ w_kv).reshape(B, T, 3, NKV_B, 2, HD_B)

    def knorm(rows, gain):
        return jnp.stack([rmsnorm(rows[..., 0, :], gain), rows[..., 1, :]], axis=-2)

    return kv[:, :, 0], knorm(kv[:, :, 1], k_norm[1]), knorm(kv[:, :, 2], k_norm[2])


def compress(rows, w_cmp1, w_cmp2, cmp_pos, k_gain):
    B, T = rows.shape[:2]
    N = T // BLOCK_B
    blk = rows[:, :N * BLOCK_B].reshape(B, N, BLOCK_B, NKV_B, 2, HD_B)
    pre = jnp.einsum('bnlgcd,cldh->bngch', blk, w_cmp1) + jnp.einsum('lcd,cldh->ch', cmp_pos, w_cmp1)
    out = jnp.einsum('bngch,chd->bngcd', jax.nn.silu(pre), w_cmp2)
    return rmsnorm(out[..., 0, :], k_gain), out[..., 1, :]


def nsa_queries(xn, w_q, b_gate, q_gain):
    B, T, _ = xn.shape
    width = NH_B * HD_B
    proj = xn @ w_q
    qs = proj[..., :3 * width].reshape(B, T, 3, NKV_B, REP_B, HD_B)
    qs = rmsnorm(qs, q_gain[:, None, None, :]) * (HD_B ** -0.5)
    gates = jax.nn.sigmoid(proj[..., 3 * width:].astype(jnp.float32) + b_gate.astype(jnp.float32))
    return qs, gates.reshape(B, T, 3, NKV_B, REP_B)


def nsa_attend(qs, gates, qpos, kc, vc, cend, fetch_sel, kw, vw, kwpos, rel_table):
    B, Q = qs.shape[:2]
    N = kc.shape[1]
    qc, qsl, qw = qs[:, :, 0], qs[:, :, 1], qs[:, :, 2]
    table_g = rel_table.reshape(NUM_BUCKETS, NKV_B, REP_B)
    dist_c = qpos[:, None] - cend[None, :]
    mask_c = (dist_c >= 0)[:, None, None, :]
    logit = jnp.einsum('bqgrd,bngd->bqgrn', qc, kc, preferred_element_type=jnp.float32)
    logit = logit + jnp.transpose(table_g[rel_bucket(dist_c)], (0, 2, 3, 1)).astype(jnp.float32)
    p_c = jax.nn.softmax(jnp.where(mask_c, logit, NEG_BIG), axis=-1) * mask_c
    o_c = jnp.einsum('bqgrn,bngd->bqgrd', p_c.astype(vc.dtype), vc)
    cur = qpos // BLOCK_B
    cand = jnp.arange(N, dtype=jnp.int32)[None, :] < cur[:, None]
    score = jnp.where(cand[:, None, :], jnp.sum(p_c, axis=3), -jnp.inf)
    n_top = TOPK_B - 1
    if N < n_top:
        score = jnp.pad(score, ((0, 0), (0, 0), (0, 0), (0, n_top - N)), constant_values=-jnp.inf)
    _, idx = lax.top_k(score, n_top)
    cur_b = jnp.broadcast_to(cur[None, :, None, None], (B, Q, NKV_B, 1))
    blk = jnp.concatenate([idx, cur_b], axis=-1)
    bvalid = jnp.concatenate([idx < cur[None, :, None, None], jnp.ones_like(cur_b, dtype=bool)], axis=-1)
    kpos = (blk[..., None] * BLOCK_B + jnp.arange(BLOCK_B, dtype=jnp.int32)).reshape(B, Q, NKV_B, TOPK_B * BLOCK_B)
    kmask = jnp.repeat(bvalid, BLOCK_B, axis=-1) & (kpos <= qpos[None, :, None, None])
    ks, vs = fetch_sel(kpos)
    logit = jnp.einsum('bqgrd,bqgkd->bqgrk', qsl, ks, preferred_element_type=jnp.float32)
    bias_s = table_g[rel_bucket(qpos[None, :, None, None] - kpos), jnp.arange(NKV_B)[None, None, :, None]]
    logit = logit + jnp.moveaxis(bias_s, -1, 3).astype(jnp.float32)
    p_s = jax.nn.softmax(jnp.where(kmask[:, :, :, None, :], logit, NEG_BIG), axis=-1)
    o_s = jnp.einsum('bqgrk,bqgkd->bqgrd', p_s.astype(vs.dtype), vs)
    dist_w = qpos[:, None] - kwpos[None, :]
    mask_w = ((dist_w >= 0) & (dist_w <= WINDOW_B) & (kwpos >= 0)[None, :])[:, None, None, :]
    logit = jnp.einsum('bqgrd,bkgd->bqgrk', qw, kw, preferred_element_type=jnp.float32)
    logit = logit + jnp.transpose(table_g[rel_bucket(dist_w)], (0, 2, 3, 1)).astype(jnp.float32)
    p_w = jax.nn.softmax(jnp.where(mask_w, logit, NEG_BIG), axis=-1)
    o_w = jnp.einsum('bqgrk,bkgd->bqgrd', p_w.astype(vw.dtype), vw)
    g = gates.astype(o_c.dtype)[..., None]
    out = g[:, :, 0] * o_c + g[:, :, 1] * o_s + g[:, :, 2] * o_w
    return out.reshape(B, Q, NH_B * HD_B)


def nsa_prompt(qs, gates, kc, vc, cend, sel_rows, win_rows, rel_table):
    B, T = qs.shape[:2]
    bidx = jnp.arange(B)[:, None, None, None]
    gidx = jnp.arange(NKV_B)[None, None, :, None]

    def fetch(kpos):
        rows = sel_rows[bidx, jnp.minimum(kpos, T - 1), gidx]
        return rows[..., 0, :], rows[..., 1, :]

    win_pad = jnp.pad(win_rows, ((0, 0), (WINDOW_B, 0), (0, 0), (0, 0), (0, 0)))

    def one_block(start):
        q_blk = lax.dynamic_slice_in_dim(qs, start, Q_BLOCK, axis=1)
        g_blk = lax.dynamic_slice_in_dim(gates, start, Q_BLOCK, axis=1)
        qpos = start + jnp.arange(Q_BLOCK, dtype=jnp.int32)
        w_blk = lax.dynamic_slice_in_dim(win_pad, start, WINDOW_B + Q_BLOCK, axis=1)
        kwpos = start - WINDOW_B + jnp.arange(WINDOW_B + Q_BLOCK, dtype=jnp.int32)
        return nsa_attend(q_blk, g_blk, qpos, kc, vc, cend, fetch, w_blk[..., 0, :], w_blk[..., 1, :], kwpos, rel_table)

    outs = lax.map(one_block, jnp.arange(T // Q_BLOCK, dtype=jnp.int32) * Q_BLOCK)
    return jnp.moveaxis(outs, 0, 1).reshape(B, T, -1)


def trunk(x, c, a_init, make_ctx, p):
    silu_c = jax.nn.silu(c)
    a_states = []
    attend = None
    kv_rows = None
    for l in range(DEPTH):
        ada = silu_c @ p['w_ada'][l] + p['b_ada'][l]
        sh1, sc1, g1, sh2, sc2, g2 = jnp.split(ada, 6, axis=-1)
        xn = modulate(x, p['norm_mix'][l], sh1, sc1)
        if l < N_A_LAYERS:
            y, st = mlstm_mixer(xn, p['w_in_a'][l], p['b_if_a'][l], p['head_norm_a'][l], p['w_out_a'][l], a_init[l])
            a_states.append(st)
        else:
            j = l - N_A_LAYERS
            qs, gates = nsa_queries(xn, p['w_q_b'][j], p['b_gate_b'][j], p['q_norm_b'][j])
            y = attend(qs, gates) @ p['w_o_b'][j]
        x = x + g1[:, None, :] * y
        xn = modulate(x, p['norm_ffn'][l], sh2, sc2)
        x = x + g2[:, None, :] * (jnp.square(jax.nn.relu(xn @ p['w_ff1'][l])) @ p['w_ff2'][l])
        if l == N_A_LAYERS - 1:
            kv_ada = silu_c @ p['w_ada_kv'] + p['b_ada_kv']
            sh, sc = jnp.split(kv_ada, 2, axis=-1)
            kv_rows = shared_kv(modulate(x, p['norm_kv'], sh, sc), p['w_kv'], p['k_norm'])
            attend = make_ctx(*kv_rows)
    return x, a_states, kv_rows


def setup_inputs(seed: int = 0) -> dict:
    key = jax.random.key(seed)
    keys = iter(jax.random.split(key, 64))
    f32 = jnp.float32

    def nrm(shape, scale=1.0):
        return jax.random.normal(next(keys), shape, f32) * scale

    inner = NH_A * DK_A
    n_pages = PAST_LEN // PAGE_SIZE
    n_used = DEC_BATCH * n_pages
    n_pool = n_used + (n_used + 3) // 4
    wbuf = min(WINDOW_B, PAST_LEN)
    page_table = jax.random.permutation(next(keys), n_pool)[:n_used].reshape(DEC_BATCH, n_pages).astype(jnp.int32)
    f_bias = jnp.linspace(F_BIAS_LO, F_BIAS_HI, NH_A, dtype=f32)
    b_if_a = jnp.concatenate([nrm((N_A_LAYERS, NH_A), 0.1), f_bias[None, :] + nrm((N_A_LAYERS, NH_A), 0.1)], axis=-1)
    return {
        'x_prompt': nrm((BATCH, SEQ, D_MODEL)),
        'x_sample': nrm((DEC_BATCH, DEC_SEQ, D_MODEL)),
        'cache_cmp': nrm((n_pool, PAGE_SIZE, NKV_B, 2, HD_B)),
        'cache_sel': nrm((n_pool, PAGE_SIZE, NKV_B, 2, HD_B)),
        'cache_win': nrm((DEC_BATCH, wbuf, NKV_B, 2, HD_B)),
        'state_C': nrm((N_A_LAYERS, DEC_BATCH, NH_A, DK_A, DV_A)),
        'state_n': nrm((N_A_LAYERS, DEC_BATCH, NH_A, DK_A)),
        'state_m': nrm((N_A_LAYERS, DEC_BATCH, NH_A)),
        'page_table': page_table,
        'c_prompt': nrm((BATCH, D_MODEL)),
        'c_sample': nrm((DEC_BATCH, D_MODEL)),
        'w_ada': nrm((DEPTH, D_MODEL, 6 * D_MODEL), 0.5 * D_MODEL ** -0.5),
        'b_ada': nrm((DEPTH, 6 * D_MODEL), 0.02),
        'norm_mix': 1.0 + nrm((DEPTH, D_MODEL), 0.05),
        'norm_ffn': 1.0 + nrm((DEPTH, D_MODEL), 0.05),
        'w_ff1': nrm((DEPTH, D_MODEL, D_FF), D_MODEL ** -0.5),
        'w_ff2': nrm((DEPTH, D_FF, D_MODEL), D_FF ** -0.5),
        'w_in_a': nrm((N_A_LAYERS, D_MODEL, 4 * inner + 2 * NH_A), D_MODEL ** -0.5),
        'b_if_a': b_if_a,
        'head_norm_a': 1.0 + nrm((N_A_LAYERS, NH_A, DV_A), 0.05),
        'w_out_a': nrm((N_A_LAYERS, inner, D_MODEL), inner ** -0.5),
        'w_ada_kv': nrm((D_MODEL, 2 * D_MODEL), 0.5 * D_MODEL ** -0.5),
        'b_ada_kv': nrm((2 * D_MODEL,), 0.02),
        'norm_kv': 1.0 + nrm((D_MODEL,), 0.05),
        'w_kv': nrm((D_MODEL, 3 * NKV_B * 2 * HD_B), D_MODEL ** -0.5),
        'k_norm': 1.0 + nrm((3, HD_B), 0.05),
        'cmp_pos': nrm((BLOCK_B, 2, HD_B), 0.5),
        'w_cmp1': nrm((2, BLOCK_B, HD_B, CMP_HIDDEN), (BLOCK_B * HD_B) ** -0.5),
        'w_cmp2': nrm((2, CMP_HIDDEN, HD_B), CMP_HIDDEN ** -0.5),
        'w_q_b': nrm((N_B_LAYERS, D_MODEL, 3 * NH_B * HD_B + 3 * NH_B), D_MODEL ** -0.5),
        'b_gate_b': nrm((N_B_LAYERS, 3 * NH_B), 0.1),
        'q_norm_b': 1.0 + nrm((N_B_LAYERS, 3, HD_B), 0.05),
        'w_o_b': nrm((N_B_LAYERS, NH_B * HD_B, D_MODEL), (NH_B * HD_B) ** -0.5),
        'rel_bias': nrm((NUM_BUCKETS, NH_B), 0.5),
    }


def reference(x_prompt, x_sample, cache_cmp, cache_sel, cache_win, state_C, state_n, state_m, page_table,
              c_prompt, c_sample, w_ada, b_ada, norm_mix, norm_ffn, w_ff1, w_ff2, w_in_a, b_if_a,
              head_norm_a, w_out_a, w_ada_kv, b_ada_kv, norm_kv, w_kv, k_norm, cmp_pos, w_cmp1, w_cmp2,
              w_q_b, b_gate_b, q_norm_b, w_o_b, rel_bias):
    p = dict(w_ada=w_ada, b_ada=b_ada, norm_mix=norm_mix, norm_ffn=norm_ffn, w_ff1=w_ff1, w_ff2=w_ff2,
             w_in_a=w_in_a, b_if_a=b_if_a, head_norm_a=head_norm_a, w_out_a=w_out_a,
             w_ada_kv=w_ada_kv, b_ada_kv=b_ada_kv, norm_kv=norm_kv, w_kv=w_kv, k_norm=k_norm,
             w_q_b=w_q_b, b_gate_b=b_gate_b, q_norm_b=q_norm_b, w_o_b=w_o_b)
    f32 = jnp.float32

    bp = x_prompt.shape[0]
    zero_state = [(jnp.zeros((bp, NH_A, DK_A, DV_A), f32), jnp.zeros((bp, NH_A, DK_A), f32),
                   jnp.zeros((bp, NH_A), f32)) for _ in range(N_A_LAYERS)]

    def prompt_ctx(cmp_rows, sel_rows, win_rows):
        kc, vc = compress(cmp_rows, w_cmp1, w_cmp2, cmp_pos, k_norm[0])
        cend = jnp.arange(kc.shape[1], dtype=jnp.int32) * BLOCK_B + BLOCK_B - 1
        return lambda qs, gates: nsa_prompt(qs, gates, kc, vc, cend, sel_rows, win_rows, rel_bias)

    y_prompt, p_states, (p_cmp, p_sel, p_win_rows) = trunk(x_prompt, c_prompt, zero_state, prompt_ctx, p)

    bs, ts = x_sample.shape[:2]
    wbuf = cache_win.shape[1]
    s_init = [(state_C[l], state_n[l], state_m[l]) for l in range(N_A_LAYERS)]
    bidx = jnp.arange(bs)[:, None, None, None]
    gidx = jnp.arange(NKV_B)[None, None, :, None]

    def sample_ctx(cmp_new, sel_new, win_new):
        cmp_past = cache_cmp[page_table].reshape(bs, PAST_LEN, NKV_B, 2, HD_B)
        kc_p, vc_p = compress(cmp_past, w_cmp1, w_cmp2, cmp_pos, k_norm[0])
        kc_n, vc_n = compress(cmp_new, w_cmp1, w_cmp2, cmp_pos, k_norm[0])
        kc = jnp.concatenate([kc_p, kc_n], axis=1)
        vc = jnp.concatenate([vc_p, vc_n], axis=1)
        cend = jnp.arange(kc.shape[1], dtype=jnp.int32) * BLOCK_B + BLOCK_B - 1
        win_all = jnp.concatenate([cache_win, win_new], axis=1)
        kwpos = PAST_LEN - wbuf + jnp.arange(wbuf + ts, dtype=jnp.int32)
        qpos = PAST_LEN + jnp.arange(ts, dtype=jnp.int32)

        def fetch(kpos):
            kp = jnp.minimum(kpos, PAST_LEN - 1)
            page = page_table[bidx, kp // PAGE_SIZE]
            rows_past = cache_sel[page, kp % PAGE_SIZE, gidx]
            rows_new = sel_new[bidx, jnp.clip(kpos - PAST_LEN, 0, ts - 1), gidx]
            rows = jnp.where((kpos < PAST_LEN)[..., None, None], rows_past, rows_new)
            return rows[..., 0, :], rows[..., 1, :]

        return lambda qs, gates: nsa_attend(qs, gates, qpos, kc, vc, cend, fetch,
                                            win_all[..., 0, :], win_all[..., 1, :], kwpos, rel_bias)

    y_sample, s_states, (s_cmp, s_sel, s_win_rows) = trunk(x_sample, c_sample, s_init, sample_ctx, p)

    p_C = jnp.stack([s[0] for s in p_states])
    p_n = jnp.stack([s[1] for s in p_states])
    p_m = jnp.stack([s[2] for s in p_states])
    s_C = jnp.stack([s[0] for s in s_states])
    s_n = jnp.stack([s[1] for s in s_states])
    s_m = jnp.stack([s[2] for s in s_states])
    p_win = p_win_rows[:, -min(WINDOW_B, x_prompt.shape[1]):]
    s_win = jnp.concatenate([cache_win, s_win_rows], axis=1)[:, -min(WINDOW_B, wbuf + ts):]
    return (y_prompt, y_sample, p_C, p_n, p_m, p_cmp, p_sel, p_win, s_C, s_n, s_m, s_cmp, s_sel, s_win)
```

```python
import functools
import math

import numpy as np
import jax
import jax.numpy as jnp
from jax import lax
from jax.experimental import pallas as pl
from jax.experimental.pallas import tpu as pltpu

F32 = jnp.float32
BF16 = jnp.bfloat16
I32 = jnp.int32

D_MODEL = 1024
NH_A = 4
DK_A = 256
NKV = 4
REP = 4
HD = 64
BLOCK = 64
TOPK = 16
WINDOW = 512
N_BUCKETS = 32
REL_MAX_DIST = 2048
PAGE = 128
EPS = 1e-6
NEG = -1e30

LANES = 128
VMEM_LIMIT = 56 * 1024 * 1024
TQ = 128
KT = 256
N_BIAS_TILES = 14
CMP_BLOCKS_PER_STEP = 64


def _bucket_thresholds():
    exact = N_BUCKETS // 2
    d = np.arange(1, 4 * REL_MAX_DIST, dtype=np.float64)
    big = exact + np.floor(np.log(d / exact) / math.log(REL_MAX_DIST / exact) * (N_BUCKETS - exact)).astype(np.int64)
    b = np.where(d < exact, d.astype(np.int64), np.minimum(big, N_BUCKETS - 1))
    return [int(d[np.argmax(b >= k)]) for k in range(1, N_BUCKETS)]


BUCKET_THR = _bucket_thresholds()
assert 128 * (N_BIAS_TILES - 1) - (TQ - 1) >= BUCKET_THR[-1]


def _cparams(*sem):
    return pltpu.CompilerParams(dimension_semantics=sem, vmem_limit_bytes=VMEM_LIMIT)


def _dot(a, b):
    return jnp.dot(a.astype(BF16), b.astype(BF16), preferred_element_type=F32)


def _dot_nt(a, b):
    return lax.dot_general(a.astype(BF16), b.astype(BF16), (((1,), (1,)), ((), ())),
                           preferred_element_type=F32)


def _split3(x):
    hi = x.astype(BF16)
    r1 = x - hi.astype(F32)
    mid = r1.astype(BF16)
    lo = (r1 - mid.astype(F32)).astype(BF16)
    return hi, mid, lo


def _sigmoid(x):
    return 1.0 / (1.0 + jnp.exp(-x))


def _norm_mod(x, gain, shift, scale):
    ms = jnp.mean(x * x, axis=-1, keepdims=True)
    y = x * lax.rsqrt(ms + EPS) * gain
    return y * (1.0 + scale) + shift


def _seg_mean_sq(y, bd):
    parts = []
    for j in range(y.shape[1] // 256):
        sq = y[:, j * 256:(j + 1) * 256]
        sq = sq * sq
        hi = sq.astype(BF16)
        lo = (sq - hi.astype(F32)).astype(BF16)
        parts.append(jnp.dot(hi, bd, preferred_element_type=F32) + jnp.dot(lo, bd, preferred_element_type=F32))
    return parts[0] if len(parts) == 1 else jnp.concatenate(parts, axis=1)


def _row_spec(tm, n, per_row):
    if per_row:
        return pl.BlockSpec((tm, n), lambda i: (i, 0))
    return pl.BlockSpec((1, n), lambda i: (0, 0))


def _const_spec(shape):
    return pl.BlockSpec(shape, lambda i: tuple(0 for _ in shape))


def _ada_kernel(c_ref, w_ref, b_ref, o_ref):
    c = c_ref[...]
    o_ref[...] = _dot(c * _sigmoid(c), w_ref[...]) + b_ref[...]


def _ada(c, w, b):
    m, k = c.shape
    n = w.shape[1]
    tn = 1024
    return pl.pallas_call(
        _ada_kernel,
        grid=(n // tn,),
        in_specs=[pl.BlockSpec((m, k), lambda j: (0, 0)),
                  pl.BlockSpec((k, tn), lambda j: (0, j)),
                  pl.BlockSpec((1, tn), lambda j: (0, j))],
        out_specs=pl.BlockSpec((m, tn), lambda j: (0, j)),
        out_shape=jax.ShapeDtypeStruct((m, n), F32),
        compiler_params=_cparams("arbitrary"),
        name="ada",
    )(c, w, b.reshape(1, n))


def _proj_plain_kernel(x_ref, gain_ref, sh_ref, sc_ref, w_ref, o_ref):
    xn = _norm_mod(x_ref[...], gain_ref[...], sh_ref[...], sc_ref[...])
    o_ref[...] = jnp.dot(xn.astype(BF16), w_ref[...], preferred_element_type=F32)


def _proj_plain(x, gain, shift, scale, w_bf, tm):
    m, d = x.shape
    n = w_bf.shape[1]
    per_row = shift.shape[0] != 1
    return pl.pallas_call(
        _proj_plain_kernel,
        grid=(m // tm,),
        in_specs=[pl.BlockSpec((tm, d), lambda i: (i, 0)),
                  _const_spec((1, d)),
                  _row_spec(tm, d, per_row), _row_spec(tm, d, per_row),
                  _const_spec((d, n))],
        out_specs=pl.BlockSpec((tm, n), lambda i: (i, 0)),
        out_shape=jax.ShapeDtypeStruct((m, n), F32),
        compiler_params=_cparams("arbitrary"),
        name="proj_mlstm",
    )(x, gain, shift, scale, w_bf)


def _proj_q_kernel(x_ref, gain_ref, sh_ref, sc_ref, w_ref, bd_ref, qg_ref, bg_ref, q_ref, gt_ref):
    xn = _norm_mod(x_ref[...], gain_ref[...], sh_ref[...], sc_ref[...])
    y = jnp.dot(xn.astype(BF16), w_ref[...], preferred_element_type=F32)
    nq = q_ref.shape[1]
    yq = y[:, :nq]
    ms = _seg_mean_sq(yq, bd_ref[...])
    q_ref[...] = yq * lax.rsqrt(ms + EPS) * qg_ref[...]
    gates = _sigmoid(y[:, nq:] + bg_ref[...])
    gt_ref[...] = gates.T


def _proj_q(x, gain, shift, scale, w_bf, bd, qgain_row, bgate_row, tm):
    m, d = x.shape
    n = w_bf.shape[1]
    nq = n - LANES
    per_row = shift.shape[0] != 1
    return pl.pallas_call(
        _proj_q_kernel,
        grid=(m // tm,),
        in_specs=[pl.BlockSpec((tm, d), lambda i: (i, 0)),
                  _const_spec((1, d)),
                  _row_spec(tm, d, per_row), _row_spec(tm, d, per_row),
                  _const_spec((d, n)), _const_spec((256, 256)),
                  _const_spec((1, nq)), _const_spec((1, LANES))],
        out_specs=[pl.BlockSpec((tm, nq), lambda i: (i, 0)),
                   pl.BlockSpec((LANES, tm), lambda i: (0, i))],
        out_shape=[jax.ShapeDtypeStruct((m, nq), F32),
                   jax.ShapeDtypeStruct((LANES, m), F32)],
        compiler_params=_cparams("arbitrary"),
        name="proj_q",
    )(x, gain, shift, scale, w_bf, bd, qgain_row, bgate_row)


def _proj_kv_kernel(x_ref, gain_ref, sh_ref, sc_ref, w_ref, bd_ref, kg_ref, km_ref,
                    cmp_ref, sel_ref, win_ref, selb_ref, selt_ref, winb_ref, wint_ref):
    xn = _norm_mod(x_ref[...], gain_ref[...], sh_ref[...], sc_ref[...])
    y = jnp.dot(xn.astype(BF16), w_ref[...], preferred_element_type=F32)
    w = cmp_ref.shape[1]
    cmp_ref[...] = y[:, :w]
    ykn = y[:, w:]
    ms = _seg_mean_sq(ykn, bd_ref[...])
    ykn = jnp.where(km_ref[...] > 0.5, ykn * lax.rsqrt(ms + EPS) * kg_ref[...], ykn)
    sel = ykn[:, :w]
    win = ykn[:, w:]
    sel_ref[...] = sel
    win_ref[...] = win
    selb_ref[...] = sel.astype(BF16)
    winb_ref[...] = win.astype(BF16)
    selt_ref[...] = sel.T.astype(BF16)
    wint_ref[...] = win.T.astype(BF16)


def _proj_kv(x, gain, shift, scale, w_bf, bd, kgain_row, kmask_row, tm):
    m, d = x.shape
    n = w_bf.shape[1]
    w = n // 3
    per_row = shift.shape[0] != 1
    row = pl.BlockSpec((tm, w), lambda i: (i, 0))
    col = pl.BlockSpec((w, tm), lambda i: (0, i))
    return pl.pallas_call(
        _proj_kv_kernel,
        grid=(m // tm,),
        in_specs=[pl.BlockSpec((tm, d), lambda i: (i, 0)),
                  _const_spec((1, d)),
                  _row_spec(tm, d, per_row), _row_spec(tm, d, per_row),
                  _const_spec((d, n)), _const_spec((256, 256)),
                  _const_spec((1, 2 * w)), _const_spec((1, 2 * w))],
        out_specs=[row, row, row, row, col, row, col],
        out_shape=[jax.ShapeDtypeStruct((m, w), F32), jax.ShapeDtypeStruct((m, w), F32),
                   jax.ShapeDtypeStruct((m, w), F32),
                   jax.ShapeDtypeStruct((m, w), BF16), jax.ShapeDtypeStruct((w, m), BF16),
                   jax.ShapeDtypeStruct((m, w), BF16), jax.ShapeDtypeStruct((w, m), BF16)],
        compiler_params=_cparams("arbitrary"),
        name="proj_kv",
    )(x, gain, shift, scale, w_bf, bd, kgain_row, kmask_row)


def _out_proj_kernel(x_ref, g_ref, y_ref, w_ref, o_ref):
    o_ref[...] = x_ref[...] + g_ref[...] * jnp.dot(y_ref[...].astype(BF16), w_ref[...],
                                                   preferred_element_type=F32)


def _out_proj(x, g, y, w_bf, tm):
    m, d = x.shape
    k = y.shape[1]
    per_row = g.shape[0] != 1
    return pl.pallas_call(
        _out_proj_kernel,
        grid=(m // tm,),
        in_specs=[pl.BlockSpec((tm, d), lambda i: (i, 0)),
                  _row_spec(tm, d, per_row),
                  pl.BlockSpec((tm, k), lambda i: (i, 0)),
                  _const_spec((k, d))],
        out_specs=pl.BlockSpec((tm, d), lambda i: (i, 0)),
        out_shape=jax.ShapeDtypeStruct((m, d), F32),
        compiler_params=_cparams("arbitrary"),
        name="out_proj",
    )(x, g, y, w_bf)


def _ffn_kernel(x_ref, gain_ref, sh_ref, sc_ref, g_ref, w1_ref, w2_ref, o_ref, *, fc):
    x = x_ref[...]
    xn = _norm_mod(x, gain_ref[...], sh_ref[...], sc_ref[...]).astype(BF16)
    acc = jnp.zeros(x.shape, F32)
    for c in range(w1_ref.shape[1] // fc):
        h = jnp.dot(xn, w1_ref[:, c * fc:(c + 1) * fc], preferred_element_type=F32)
        h = jnp.maximum(h, 0.0)
        acc = acc + jnp.dot((h * h).astype(BF16), w2_ref[c * fc:(c + 1) * fc, :],
                            preferred_element_type=F32)
    o_ref[...] = x + g_ref[...] * acc


def _ffn(x, gain, shift, scale, g, w1_bf, w2_bf, tm):
    m, d = x.shape
    f = w1_bf.shape[1]
    per_row = shift.shape[0] != 1
    return pl.pallas_call(
        functools.partial(_ffn_kernel, fc=1024),
        grid=(m // tm,),
        in_specs=[pl.BlockSpec((tm, d), lambda i: (i, 0)),
                  _const_spec((1, d)),
                  _row_spec(tm, d, per_row), _row_spec(tm, d, per_row), _row_spec(tm, d, per_row),
                  _const_spec((d, f)), _const_spec((f, d))],
        out_specs=pl.BlockSpec((tm, d), lambda i: (i, 0)),
        out_shape=jax.ShapeDtypeStruct((m, d), F32),
        compiler_params=_cparams("arbitrary"),
        name="ffn",
    )(x, gain, shift, scale, g, w1_bf, w2_bf)


def _mlstm_kernel(q_ref, k_ref, v_ref, o_ref, gt_ref, bif_ref, hg_ref, c0_ref, n0_ref, m0_ref,
                  y_ref, cout_ref, nout_ref, mout_ref, c_scr, n_scr, m_scr, *, rows, valid, chunk, zero_init):
    ci = pl.program_id(1)
    nci = pl.num_programs(1)

    @pl.when(ci == 0)
    def _():
        if zero_init:
            c_scr[...] = jnp.zeros(c_scr.shape, F32)
            n_scr[...] = jnp.zeros(n_scr.shape, F32)
            m_scr[...] = jnp.zeros(m_scr.shape, F32)
        else:
            c_scr[...] = c0_ref[0]
            n_scr[...] = n0_ref[0]
            m_scr[...] = m0_ref[0]

    L = chunk

    def padded(ref_val, fill):
        if rows == L:
            return ref_val
        pad = jnp.full((L - rows, ref_val.shape[1]), fill, ref_val.dtype)
        return jnp.concatenate([ref_val, pad], axis=0)

    q = padded(q_ref[0], 0.0)
    k = padded(k_ref[0], 0.0)
    v = padded(v_ref[0], 0.0)
    og = padded(o_ref[0], 0.0)
    g = padded(gt_ref[0], 0.0) + bif_ref[...]
    lf = -(jnp.maximum(-g, 0.0) + jnp.log1p(jnp.exp(-jnp.abs(g))))
    li = g
    if valid != L:
        is_real = lax.broadcasted_iota(I32, (L, LANES), 0) < valid
        lf = jnp.where(is_real, lf, 0.0)
        li = jnp.where(is_real, li, NEG)
    r_io = lax.broadcasted_iota(I32, (L, L), 0)
    c_io = lax.broadcasted_iota(I32, (L, L), 1)
    causal = c_io <= r_io
    tril = jnp.where(causal, 1.0, 0.0).astype(BF16)
    hi, mid, lo = _split3(lf)
    b = (jnp.dot(tril, hi, preferred_element_type=F32) + jnp.dot(tril, mid, preferred_element_type=F32)
         + jnp.dot(tril, lo, preferred_element_type=F32))
    lane = lax.broadcasted_iota(I32, (L, LANES), 1)
    mixed_t = jnp.where(lane < NH_A, li, b).T
    for h in range(NH_A):
        sl = slice(h * DK_A, (h + 1) * DK_A)
        qh = q[:, sl] * (DK_A ** -0.5)
        kh = k[:, sl]
        vh = v[:, sl]
        b_col = b[:, NH_A + h:NH_A + h + 1]
        li_col = li[:, h:h + 1]
        b_row = mixed_t[NH_A + h:NH_A + h + 1, :]
        li_row = mixed_t[h:h + 1, :]
        m_prev = m_scr[h:h + 1, 0:1]
        c_prev = c_scr[h]
        n_prev = n_scr[h]
        dmat = jnp.where(causal, b_col - b_row + li_row, NEG)
        m_inter = b_col + m_prev
        m_t = jnp.maximum(m_inter, jnp.max(dmat, axis=-1, keepdims=True))
        s = _dot_nt(qh, kh) * jnp.exp(dmat - m_t)
        a_inter = jnp.exp(m_inter - m_t)
        num = _dot(s, vh) + a_inter * _dot(qh, c_prev)
        den = jnp.sum(s, axis=-1, keepdims=True) + a_inter * jnp.sum(qh * n_prev, axis=-1, keepdims=True)
        hh = num / jnp.maximum(jnp.abs(den), jnp.exp(-m_t))
        b_last = b_col[L - 1:L, :]
        g_col = b_last - b_col + li_col
        m_new = jnp.maximum(b_last + m_prev, jnp.max(g_col, axis=0, keepdims=True))
        w_col = jnp.exp(g_col - m_new)
        decay = jnp.exp(b_last + m_prev - m_new)
        kw = kh * w_col
        c_scr[h] = decay * c_prev + _dot(kw.T, vh)
        n_scr[h] = decay * n_prev + jnp.sum(kw, axis=0, keepdims=True)
        m_scr[h:h + 1, :] = jnp.broadcast_to(m_new, (1, LANES))
        hn = hh * lax.rsqrt(jnp.mean(hh * hh, axis=-1, keepdims=True) + EPS) * hg_ref[h:h + 1, :]
        yh = hn * _sigmoid(og[:, sl])
        y_ref[0, :, sl] = yh[:rows]

    @pl.when(ci == nci - 1)
    def _():
        cout_ref[0] = c_scr[...]
        nout_ref[0] = n_scr[...]
        mout_ref[0] = m_scr[...]


def _mlstm(proj, b_if_row, head_gain, c0, n0, m0, *, rows, valid, chunk, zero_init):
    batch, t, _ = proj.shape
    nch = t // rows
    inner = NH_A * DK_A

    def colblk(j, width):
        return pl.BlockSpec((1, rows, width), lambda b, c, j=j: (b, c, j))

    state_c = pl.BlockSpec((1, NH_A, DK_A, DK_A), lambda b, c: (b, 0, 0, 0))
    state_n = pl.BlockSpec((1, NH_A, 1, DK_A), lambda b, c: (b, 0, 0, 0))
    state_m = pl.BlockSpec((1, 8, LANES), lambda b, c: (b, 0, 0))
    return pl.pallas_call(
        functools.partial(_mlstm_kernel, rows=rows, valid=valid, chunk=chunk, zero_init=zero_init),
        grid=(batch, nch),
        in_specs=[colblk(0, inner), colblk(1, inner), colblk(2, inner), colblk(3, inner),
                  colblk(4 * inner // LANES, LANES),
                  pl.BlockSpec((1, LANES), lambda b, c: (0, 0)),
                  pl.BlockSpec((NH_A, DK_A), lambda b, c: (0, 0)),
                  state_c, state_n, state_m],
        out_specs=[pl.BlockSpec((1, rows, inner), lambda b, c: (b, c, 0)), state_c, state_n, state_m],
        out_shape=[jax.ShapeDtypeStruct((batch, t, inner), F32),
                   jax.ShapeDtypeStruct((batch, NH_A, DK_A, DK_A), F32),
                   jax.ShapeDtypeStruct((batch, NH_A, 1, DK_A), F32),
                   jax.ShapeDtypeStruct((batch, 8, LANES), F32)],
        scratch_shapes=[pltpu.VMEM((NH_A, DK_A, DK_A), F32), pltpu.VMEM((NH_A, 1, DK_A), F32),
                        pltpu.VMEM((8, LANES), F32)],
        compiler_params=_cparams("arbitrary", "arbitrary"),
        name="mlstm",
    )(proj, proj, proj, proj, proj, b_if_row, head_gain, c0, n0, m0)


def _compress_body(r_refs, pos_ref, w1_ref, w2_ref, kg_ref, o_ref, nb):
    acc = None
    for lp in range(BLOCK // 2):
        pieces = []
        for g in range(NKV):
            xa = r_refs[g][pl.ds(2 * lp, nb, stride=BLOCK), :]
            xb = r_refs[g][pl.ds(2 * lp + 1, nb, stride=BLOCK), :]
            pieces.append(jnp.concatenate([xa, xb], axis=1))
        pa = pos_ref[2 * lp:2 * lp + 1, 0:LANES]
        pb = pos_ref[2 * lp + 1:2 * lp + 2, 0:LANES]
        pieces.append(jnp.broadcast_to(jnp.concatenate([pa, pb], axis=1), (8, 2 * LANES)))
        x = jnp.concatenate(pieces, axis=0).astype(BF16)
        d = jnp.dot(x, w1_ref[lp], preferred_element_type=F32)
        acc = d if acc is None else acc + d
    pre = acc[:NKV * nb] + acc[NKV * nb:NKV * nb + 1]
    hid = pre * _sigmoid(pre)
    out = jnp.dot(hid.astype(BF16), w2_ref[...], preferred_element_type=F32)
    is_k = lax.broadcasted_iota(I32, out.shape, 1) < HD
    ms = jnp.sum(jnp.where(is_k, out * out, 0.0), axis=-1, keepdims=True) * (1.0 / HD)
    out = jnp.where(is_k, out * lax.rsqrt(ms + EPS) * kg_ref[...], out)
    for g in range(NKV):
        o_ref[:, g * LANES:(g + 1) * LANES] = out[g * nb:(g + 1) * nb]


def _compress_kernel(r0, r1, r2, r3, pos_ref, w1_ref, w2_ref, kg_ref, o_ref, *, nb):
    _compress_body((r0, r1, r2, r3), pos_ref, w1_ref, w2_ref, kg_ref, o_ref, nb)


def _compress_paged_kernel(pt_ref, *refs, nb):
    npages = nb * BLOCK // PAGE
    pages = refs[:npages]
    pos_ref, w1_ref, w2_ref, kg_ref, o_ref = refs[npages:npages + 5]
    r_scrs = refs[npages + 5:]
    for p in range(npages):
        for g in range(NKV):
            r_scrs[g][p * PAGE:(p + 1) * PAGE, :] = pages[p][0, :, g * LANES:(g + 1) * LANES]
    _compress_body(r_scrs, pos_ref, w1_ref, w2_ref, kg_ref, o_ref, nb)


def _compress_paged(cache, page_table, pos_rows, w1p, w2p, kgain_row, nb):
    batch, npg = page_table.shape
    w = cache.shape[2]
    npages = nb * BLOCK // PAGE
    steps = npg // npages

    def page_spec(p):
        return pl.BlockSpec((1, PAGE, w), lambda b, j, pt, p=p: (pt[b, j * npages + p], 0, 0))

    def const(shape):
        return pl.BlockSpec(shape, lambda b, j, pt: tuple(0 for _ in shape))

    return pl.pallas_call(
        functools.partial(_compress_paged_kernel, nb=nb),
        grid_spec=pltpu.PrefetchScalarGridSpec(
            num_scalar_prefetch=1,
            grid=(batch, steps),
            in_specs=[page_spec(p) for p in range(npages)]
                     + [const((BLOCK, w)), const(w1p.shape), const(w2p.shape), const((1, LANES))],
            out_specs=pl.BlockSpec((nb, w), lambda b, j, pt: (b * steps + j, 0)),
            scratch_shapes=[pltpu.VMEM((nb * BLOCK, LANES), F32) for _ in range(NKV)]),
        out_shape=jax.ShapeDtypeStruct((batch * npg * PAGE // BLOCK, w), F32),
        compiler_params=_cparams("arbitrary", "arbitrary"),
        name="compress_paged",
    )(page_table, *([cache] * npages), pos_rows, w1p, w2p, kgain_row)


def _compress(rows, pos_rows, w1p, w2p, kgain_row, nb):
    t, w = rows.shape
    nblk = t // BLOCK
    return pl.pallas_call(
        functools.partial(_compress_kernel, nb=nb),
        grid=(nblk // nb,),
        in_specs=[pl.BlockSpec((nb * BLOCK, LANES), lambda i, g=g: (i, g)) for g in range(NKV)]
                 + [_const_spec((BLOCK, w)), _const_spec(w1p.shape), _const_spec(w2p.shape),
                    _const_spec((1, LANES))],
        out_specs=pl.BlockSpec((nb, w), lambda i: (i, 0)),
        out_shape=jax.ShapeDtypeStruct((nblk, w), F32),
        compiler_params=_cparams("arbitrary"),
        name="compress",
    )(rows, rows, rows, rows, pos_rows, w1p, w2p, kgain_row)


def _bias_from_dist(dist, tab_ref, h):
    out = jnp.full(dist.shape, tab_ref[0, h], F32)
    for kk in range(1, N_BUCKETS):
        out = jnp.where(dist >= BUCKET_THR[kk - 1], tab_ref[kk, h], out)
    return out


def _bias_tiles_kernel(tab_ref, o_ref):
    a = pl.program_id(0)
    h = pl.program_id(1)
    sj = lax.broadcasted_iota(I32, (LANES, TQ), 0)
    ti = lax.broadcasted_iota(I32, (LANES, TQ), 1)
    o_ref[0, 0] = _bias_from_dist(a * LANES + ti - sj, tab_ref, h)


def _bias_tiles(rel_bias):
    nh = rel_bias.shape[1]
    return pl.pallas_call(
        _bias_tiles_kernel,
        grid=(N_BIAS_TILES, nh),
        in_specs=[pl.BlockSpec(memory_space=pltpu.SMEM)],
        out_specs=pl.BlockSpec((1, 1, LANES, TQ), lambda a, h: (a, h, 0, 0)),
        out_shape=jax.ShapeDtypeStruct((N_BIAS_TILES, nh, LANES, TQ), F32),
        compiler_params=_cparams("arbitrary", "arbitrary"),
        name="bias_tiles",
    )(rel_bias)


def _stack_heads(q):
    z = jnp.zeros((q.shape[0], HD), q.dtype)
    return jnp.concatenate([jnp.concatenate([q[:, r * HD:(r + 1) * HD], z], axis=1) for r in range(REP)],
                           axis=0)


def _tile_heads(x):
    return jnp.concatenate([x] * REP, axis=1)


def _pad_rows(x, n):
    if x.shape[0] == n:
        return x
    return jnp.concatenate([x, jnp.zeros((n - x.shape[0],) + x.shape[1:], x.dtype)], axis=0)


def _cmp_branch(qc, kcvc, tab_ref, g, pos):
    tq = qc.shape[0]
    nb = kcvc.shape[0]
    lt = _dot_nt(kcvc, _stack_heads(qc))
    n_io = lax.broadcasted_iota(I32, (nb, tq), 0)
    dist = pos - (n_io * BLOCK + (BLOCK - 1))
    vis = dist >= 0
    probs = []
    for r in range(REP):
        x = lt[:, r * tq:(r + 1) * tq] + _bias_from_dist(dist, tab_ref, g * REP + r)
        x = jnp.where(vis, x, NEG)
        e = jnp.exp(x - jnp.max(x, axis=0, keepdims=True))
        p = e / jnp.sum(e, axis=0, keepdims=True)
        probs.append(jnp.where(vis, p, 0.0))
    score = probs[0] + probs[1] + probs[2] + probs[3]
    oc_t = _dot(kcvc.T[HD:, :], jnp.concatenate(probs, axis=1))
    cand = n_io < pos // BLOCK
    return oc_t, jnp.where(cand, score, -jnp.inf)


def _topk_rows(score, n_top, pick_fn):
    nb, tq = score.shape
    n_f = lax.broadcasted_iota(I32, (nb, tq), 0).astype(F32)
    s = score
    for it in range(n_top):
        mx = jnp.max(s, axis=0, keepdims=True)
        idx = jnp.min(jnp.where(s == mx, n_f, float(nb)), axis=0, keepdims=True)
        ok = mx > -jnp.inf
        hit = n_f == idx
        pick_fn(it, idx, ok, hit)
        s = jnp.where(hit, -jnp.inf, s)


def _flash_update(carry, x, vt):
    m, l, acc = carry
    m_new = jnp.maximum(m, jnp.max(x, axis=0, keepdims=True))
    alpha = jnp.exp(m - m_new)
    p = jnp.exp(x - m_new)
    l = alpha * l + jnp.sum(p, axis=0, keepdims=True)
    acc = alpha * acc + jnp.dot(vt, p.astype(BF16), preferred_element_type=F32)
    return m_new, l, acc


def _flash_init(nl):
    return (jnp.full((1, nl), NEG, F32), jnp.zeros((1, nl), F32), jnp.zeros((HD, nl), F32))


def _untranspose_heads(o_t, tq):
    halves = []
    for p in range(REP // 2):
        pair = jnp.concatenate([o_t[:, (2 * p) * tq:(2 * p + 1) * tq],
                                o_t[:, (2 * p + 1) * tq:(2 * p + 2) * tq]], axis=0)
        halves.append(pair.T)
    return jnp.concatenate(halves, axis=1)


def _gate_row(gt_ref, br, g):
    return jnp.concatenate([gt_ref[pl.ds(br * NKV * REP + g * REP + r, 1), :] for r in range(REP)], axis=1)


def _attn_prompt_kernel(tab_ref, qc_ref, qs_ref, qw_ref, gt_ref, kcvc_ref, ksel_ref, vselt_ref,
                        kw0, kw1, kw2, kw3, kw4, vw0, vw1, vw2, vw3, vw4, bt_ref, o_ref, sel_scr):
    g = pl.program_id(0)
    i = pl.program_id(1)
    tq = TQ
    nl = REP * tq
    t0 = i * tq
    pos = t0 + lax.broadcasted_iota(I32, (1, tq), 1)
    pos4 = _tile_heads(pos)

    oc_t, score = _cmp_branch(qc_ref[...], kcvc_ref[...], tab_ref, g, pos)
    nb = score.shape[0]
    n_io = lax.broadcasted_iota(I32, (nb, tq), 0)
    sel_scr[...] = jnp.where(n_io == pos // BLOCK, 1.0, 0.0)

    def pick(it, idx, ok, hit):
        sel_scr[...] = jnp.where(hit, jnp.maximum(sel_scr[...], jnp.where(ok, 1.0, 0.0)), sel_scr[...])

    _topk_rows(score, TOPK - 1, pick)

    qs_bf = _stack_heads(qs_ref[...]).astype(BF16)
    kio = lax.broadcasted_iota(I32, (KT, nl), 0)

    def sel_body(j, carry):
        s0 = pl.multiple_of(j * KT, KT)
        lt = _dot_nt(ksel_ref[pl.ds(s0, KT), :], qs_bf)
        bias_rows = []
        for a in range(KT // LANES):
            didx = jnp.clip(i - (KT // LANES) * j - a, 0, N_BIAS_TILES - 1)
            bias_rows.append(jnp.concatenate([bt_ref[didx, r] for r in range(REP)], axis=1))
        lt = lt + jnp.concatenate(bias_rows, axis=0)
        blk_rows = [jnp.broadcast_to(sel_scr[pl.ds((KT // BLOCK) * j + b, 1), :], (BLOCK, tq))
                    for b in range(KT // BLOCK)]
        selm = _tile_heads(jnp.concatenate(blk_rows, axis=0))
        x = jnp.where(kio + s0 <= pos4, jnp.where(selm > 0.5, lt, NEG), NEG)
        vt = vselt_ref[HD:2 * HD, pl.ds(s0, KT)]
        return _flash_update(carry, x, vt)

    n_tiles = (t0 + tq + KT - 1) // KT
    ms, ls, accs = lax.fori_loop(0, n_tiles, sel_body, _flash_init(nl))

    qw_bf = _stack_heads(qw_ref[...]).astype(BF16)
    kw_refs = (kw0, kw1, kw2, kw3, kw4)
    vw_refs = (vw0, vw1, vw2, vw3, vw4)
    sj = lax.broadcasted_iota(I32, (LANES, nl), 0)
    ti4 = _tile_heads(lax.broadcasted_iota(I32, (1, tq), 1))
    carry = _flash_init(nl)
    for j in range(5):
        delta = LANES * (4 - j)
        lt = _dot_nt(kw_refs[j][...], qw_bf)
        lt = lt + jnp.concatenate([bt_ref[4 - j, r] for r in range(REP)], axis=1)
        dist = delta + ti4 - sj
        ok = jnp.where((dist >= 0) & (dist <= WINDOW), 1.0, 0.0) * jnp.where(i - 4 + j >= 0, 1.0, 0.0)
        x = jnp.where(ok > 0.5, lt, NEG)
        carry = _flash_update(carry, x, vw_refs[j][HD:2 * HD, :])
    mw, lw, accw = carry

    out_t = (_gate_row(gt_ref, 0, g) * oc_t + _gate_row(gt_ref, 1, g) * (accs / ls)
             + _gate_row(gt_ref, 2, g) * (accw / lw))
    o_ref[...] = _untranspose_heads(out_t, tq)


def _attn_prompt(rel_bias, qs, gates_t, kcvc, sel_bf, selt_bf, win_bf, wint_bf, bias_tiles):
    t = qs.shape[0]
    nb = kcvc.shape[0]
    width = NKV * REP * HD

    def qspec(br):
        return pl.BlockSpec((TQ, REP * HD), lambda g, i, br=br: (i, br * NKV + g))

    def kw_spec(j):
        return pl.BlockSpec((LANES, LANES), lambda g, i, j=j: (jnp.maximum(i - 4 + j, 0), g))

    def vw_spec(j):
        return pl.BlockSpec((LANES, LANES), lambda g, i, j=j: (g, jnp.maximum(i - 4 + j, 0)))

    return pl.pallas_call(
        _attn_prompt_kernel,
        grid=(NKV, t // TQ),
        in_specs=[pl.BlockSpec(memory_space=pltpu.SMEM),
                  qspec(0), qspec(1), qspec(2),
                  pl.BlockSpec((LANES, TQ), lambda g, i: (0, i)),
                  pl.BlockSpec((nb, LANES), lambda g, i: (0, g)),
                  pl.BlockSpec((t, LANES), lambda g, i: (0, g)),
                  pl.BlockSpec((LANES, t), lambda g, i: (g, 0))]
                 + [kw_spec(j) for j in range(5)] + [vw_spec(j) for j in range(5)]
                 + [pl.BlockSpec((N_BIAS_TILES, REP, LANES, TQ), lambda g, i: (0, g, 0, 0))],
        out_specs=pl.BlockSpec((TQ, REP * HD), lambda g, i: (i, g)),
        out_shape=jax.ShapeDtypeStruct((t, width), F32),
        scratch_shapes=[pltpu.VMEM((nb, TQ), F32)],
        compiler_params=_cparams("arbitrary", "arbitrary"),
        name="attn_prompt",
    )(rel_bias, qs, qs, qs, gates_t, kcvc, sel_bf, selt_bf,
      win_bf, win_bf, win_bf, win_bf, win_bf, wint_bf, wint_bf, wint_bf, wint_bf, wint_bf, bias_tiles)


def _attn_s1_kernel(tab_ref, qc_ref, kcvc_ref, oct_ref, idx_ref, *, q0):
    g = pl.program_id(1)
    tq = TQ
    pos = q0 + lax.broadcasted_iota(I32, (1, tq), 1)
    oc_t, score = _cmp_branch(_pad_rows(qc_ref[0], tq), kcvc_ref[...], tab_ref, g, pos)
    oct_ref[0] = oc_t

    def pick(it, idx, ok, hit):
        idx_ref[0, it:it + 1, :] = jnp.where(ok, idx, -1.0).astype(I32)

    _topk_rows(score, TOPK - 1, pick)
    idx_ref[0, TOPK - 1:TOPK, :] = jnp.full((1, tq), -1, I32)


def _attn_s1(rel_bias, qs8, kcvc, q0):
    batch = qs8.shape[0]
    nb = kcvc.shape[0] // batch
    return pl.pallas_call(
        functools.partial(_attn_s1_kernel, q0=q0),
        grid=(batch, NKV),
        in_specs=[pl.BlockSpec(memory_space=pltpu.SMEM),
                  pl.BlockSpec((1, 8, REP * HD), lambda b, g: (b, 0, g)),
                  pl.BlockSpec((nb, LANES), lambda b, g: (b, g))],
        out_specs=[pl.BlockSpec((1, HD, REP * TQ), lambda b, g: (b * NKV + g, 0, 0)),
                   pl.BlockSpec((1, TOPK, TQ), lambda b, g: (b * NKV + g, 0, 0))],
        out_shape=[jax.ShapeDtypeStruct((batch * NKV, HD, REP * TQ), F32),
                   jax.ShapeDtypeStruct((batch * NKV, TOPK, TQ), I32)],
        compiler_params=_cparams("arbitrary", "arbitrary"),
        name="attn_sample_select",
    )(rel_bias, qs8, kcvc)


def _attn_s2_kernel(ids_ref, phys_ref, qs_ref, qw_ref, gt_ref, oct_ref, *refs, q0, nvalid):
    nsel = nvalid * (TOPK - 1)
    kb = refs[:nsel]
    snew_ref, cwin_ref, wnew_ref, bt_ref, o_ref = refs[nsel:]
    b = pl.program_id(0)
    g = pl.program_id(1)
    tq = TQ
    nl = REP * tq
    ti4 = _tile_heads(lax.broadcasted_iota(I32, (1, tq), 1))
    sj = lax.broadcasted_iota(I32, (LANES, nl), 0)
    base = (b * NKV + g) * nvalid * TOPK

    def bias_tile(a):
        return jnp.concatenate([bt_ref[a, r] for r in range(REP)], axis=1)

    def new_rows_step(carry, rows_ref, q_bf):
        blk = _pad_rows(rows_ref[0], LANES)
        x = jnp.where((sj <= ti4) & (sj < nvalid), _dot_nt(blk, q_bf) + bias_tile(0), NEG)
        return _flash_update(carry, x, blk.T[HD:, :].astype(BF16))

    qs_bf = _stack_heads(_pad_rows(qs_ref[0], tq)).astype(BF16)
    carry = _flash_init(nl)
    for t in range(nvalid):
        for k in range(TOPK - 1):
            n = ids_ref[base + t * TOPK + k]
            nc = jnp.maximum(n, 0)
            blk = _pad_rows(kb[t * (TOPK - 1) + k][0], LANES)
            a = jnp.clip(q0 // LANES - nc // 2, 0, N_BIAS_TILES - 1)
            off = pl.multiple_of((nc % 2) * BLOCK, BLOCK)
            bias = jnp.concatenate([bt_ref[a, r, pl.ds(off, BLOCK), :] for r in range(REP)], axis=1)
            lt = _dot_nt(blk, qs_bf) + _pad_rows(bias, LANES)
            x = jnp.where((sj < BLOCK) & (ti4 == t), lt, NEG)
            x = jnp.where(n >= 0, x, NEG)
            carry = _flash_update(carry, x, blk.T[HD:, :].astype(BF16))
    ms, ls, accs = new_rows_step(carry, snew_ref, qs_bf)

    qw_bf = _stack_heads(_pad_rows(qw_ref[0], tq)).astype(BF16)
    carry = _flash_init(nl)
    for j in range(WINDOW // LANES):
        blk = cwin_ref[0, j * LANES:(j + 1) * LANES, :]
        dist = (WINDOW - LANES * j) + ti4 - sj
        lt = _dot_nt(blk, qw_bf) + bias_tile(WINDOW // LANES - j)
        x = jnp.where((dist >= 0) & (dist <= WINDOW), lt, NEG)
        carry = _flash_update(carry, x, blk.T[HD:, :].astype(BF16))
    mw, lw, accw = new_rows_step(carry, wnew_ref, qw_bf)

    gt2 = gt_ref.at[0]
    out_t = (_gate_row(gt2, 0, g) * oct_ref[0] + _gate_row(gt2, 1, g) * (accs / ls)
             + _gate_row(gt2, 2, g) * (accw / lw))
    o_ref[0] = _untranspose_heads(out_t, tq)[:8]


def _attn_s2(ids, phys, qs8, gates_t, oct, cache_sel3, sel_new8, cache_win3, win_new8, bias_tiles, q0, nvalid):
    batch = qs8.shape[0]
    nsel = nvalid * (TOPK - 1)

    def kb_spec(t, k):
        return pl.BlockSpec((1, BLOCK, LANES),
                            lambda b, g, ids_r, phys_r, t=t, k=k:
                            (phys_r[((b * NKV + g) * nvalid + t) * TOPK + k], 0, g))

    def bg(shape, col):
        return pl.BlockSpec(shape, lambda b, g, ids_r, phys_r, col=col: (b, 0, col * NKV + g))

    return pl.pallas_call(
        functools.partial(_attn_s2_kernel, q0=q0, nvalid=nvalid),
        grid_spec=pltpu.PrefetchScalarGridSpec(
            num_scalar_prefetch=2,
            grid=(batch, NKV),
            in_specs=[bg((1, 8, REP * HD), 1), bg((1, 8, REP * HD), 2),
                      pl.BlockSpec((1, LANES, TQ), lambda b, g, ids_r, phys_r: (b, 0, 0)),
                      pl.BlockSpec((1, HD, REP * TQ), lambda b, g, ids_r, phys_r: (b * NKV + g, 0, 0))]
                     + [kb_spec(t, k) for t in range(nvalid) for k in range(TOPK - 1)]
                     + [bg((1, 8, LANES), 0), bg((1, WINDOW, LANES), 0), bg((1, 8, LANES), 0),
                        pl.BlockSpec((N_BIAS_TILES, REP, LANES, TQ), lambda b, g, ids_r, phys_r: (0, g, 0, 0))],
            out_specs=bg((1, 8, REP * HD), 0)),
        out_shape=jax.ShapeDtypeStruct((batch, 8, NKV * REP * HD), F32),
        compiler_params=_cparams("arbitrary", "arbitrary"),
        name="attn_sample",
    )(ids, phys, qs8, qs8, gates_t, oct, *([cache_sel3] * nsel), sel_new8, cache_win3, win_new8, bias_tiles)


def _pad_cols(w, n):
    return jnp.pad(w, ((0, 0), (0, n - w.shape[1])))


def _prep(p):
    inner = NH_A * DK_A
    width = NKV * REP * HD
    q = {}
    q['w_in'] = _pad_cols(p['w_in_a'][0], 4 * inner + LANES).astype(BF16)
    q['b_if'] = _pad_cols(p['b_if_a'][0][None, :], LANES)
    q['w_out'] = p['w_out_a'][0].astype(BF16)
    q['w_ff1'] = [p['w_ff1'][l].astype(BF16) for l in range(2)]
    q['w_ff2'] = [p['w_ff2'][l].astype(BF16) for l in range(2)]
    q['w_kv'] = p['w_kv'].astype(BF16)
    q['w_q'] = _pad_cols(p['w_q_b'][0], 3 * width + LANES).astype(BF16)
    q['b_gate'] = _pad_cols(p['b_gate_b'][0][None, :], LANES)
    q['w_o'] = p['w_o_b'][0].astype(BF16)
    q['q_gain'] = (jnp.tile(p['q_norm_b'][0][:, None, :], (1, NKV * REP, 1)) * (HD ** -0.5)).reshape(1, 3 * width)
    seg = np.arange(256) // HD
    q['bd'] = jnp.asarray((seg[:, None] == seg[None, :]).astype(np.float32) / HD, dtype=BF16)
    ones = jnp.ones((HD,), F32)
    q['k_gain'] = jnp.concatenate([jnp.tile(jnp.concatenate([p['k_norm'][br], ones]), NKV) for br in (1, 2)])[None, :]
    q['k_mask'] = jnp.tile(jnp.concatenate([ones, 0.0 * ones]), 2 * NKV)[None, :]
    q['k_gain0'] = jnp.concatenate([p['k_norm'][0], ones])[None, :]
    w1 = p['w_cmp1']
    z = jnp.zeros_like(w1[0])
    per_l = jnp.concatenate([jnp.concatenate([w1[0], z], axis=-1), jnp.concatenate([z, w1[1]], axis=-1)], axis=1)
    q['w_cmp1'] = per_l.reshape(BLOCK // 2, 2 * 2 * HD, 2 * w1.shape[-1]).astype(BF16)
    w2 = p['w_cmp2']
    z2 = jnp.zeros_like(w2[0])
    q['w_cmp2'] = jnp.concatenate([jnp.concatenate([w2[0], z2], axis=1), jnp.concatenate([z2, w2[1]], axis=1)],
                                  axis=0).astype(BF16)
    q['pos_rows'] = jnp.tile(p['cmp_pos'].reshape(BLOCK, 2 * HD), (1, NKV))
    return q


def _mods(ada_rows, rep):
    k = ada_rows.shape[1] // D_MODEL
    out = []
    for j in range(k):
        a = ada_rows[:, j * D_MODEL:(j + 1) * D_MODEL]
        out.append(jnp.repeat(a, rep, axis=0) if rep > 1 else a)
    return out


def _layer0(x, mods, p, q, mlstm_fn, tm, tmf):
    sh1, sc1, g1, sh2, sc2, g2 = mods
    proj = _proj_plain(x, p['norm_mix'][0][None, :], sh1, sc1, q['w_in'], tm)
    y, states = mlstm_fn(proj)
    x = _out_proj(x, g1, y, q['w_out'], tm)
    x = _ffn(x, p['norm_ffn'][0][None, :], sh2, sc2, g2, q['w_ff1'][0], q['w_ff2'][0], tmf)
    return x, states


def _layer1_tail(x, attn, mods, p, q, tm, tmf):
    _, _, g1, sh2, sc2, g2 = mods
    x = _out_proj(x, g1, attn, q['w_o'], tm)
    return _ffn(x, p['norm_ffn'][1][None, :], sh2, sc2, g2, q['w_ff1'][1], q['w_ff2'][1], tmf)


def kernel(x_prompt, x_sample, cache_cmp, cache_sel, cache_win, state_C, state_n, state_m, page_table,
           c_prompt, c_sample, w_ada, b_ada, norm_mix, norm_ffn, w_ff1, w_ff2, w_in_a, b_if_a,
           head_norm_a, w_out_a, w_ada_kv, b_ada_kv, norm_kv, w_kv, k_norm, cmp_pos, w_cmp1, w_cmp2,
           w_q_b, b_gate_b, q_norm_b, w_o_b, rel_bias):
    p = dict(norm_mix=norm_mix, norm_ffn=norm_ffn, w_ff1=w_ff1, w_ff2=w_ff2, w_in_a=w_in_a, b_if_a=b_if_a,
             w_out_a=w_out_a, w_kv=w_kv, k_norm=k_norm, cmp_pos=cmp_pos, w_cmp1=w_cmp1, w_cmp2=w_cmp2,
             w_q_b=w_q_b, b_gate_b=b_gate_b, q_norm_b=q_norm_b, w_o_b=w_o_b)
    q = _prep(p)
    bp, tp, d = x_prompt.shape
    bs, ts, _ = x_sample.shape
    past = page_table.shape[1] * PAGE
    wbuf = cache_win.shape[1]
    assert bp == 1 and ts < BLOCK and ts <= 8 and wbuf == WINDOW and past % LANES == 0
    width = NKV * REP * HD
    kvw = NKV * 2 * HD

    nc = bp + bs
    c_all = jnp.pad(jnp.concatenate([c_prompt, c_sample], axis=0), ((0, -nc % 8), (0, 0)))
    ada = [_ada(c_all, w_ada[l], b_ada[l]) for l in range(2)]
    ada_kv = _ada(c_all, w_ada_kv, b_ada_kv)
    bias_tiles = _bias_tiles(rel_bias)
    head_gain = head_norm_a[0]

    xp = x_prompt.reshape(tp, d)
    zc = jnp.zeros((bp, NH_A, DK_A, DK_A), F32)
    zn = jnp.zeros((bp, NH_A, 1, DK_A), F32)
    zm = jnp.zeros((bp, 8, LANES), F32)

    def mlstm_prompt(proj):
        y, c, n, m = _mlstm(proj.reshape(bp, tp, -1), q['b_if'], head_gain, zc, zn, zm,
                            rows=256, valid=256, chunk=256, zero_init=True)
        return y.reshape(tp, -1), (c, n, m)

    xp, (pc, pn, pm) = _layer0(xp, _mods(ada[0][:bp], 1), p, q, mlstm_prompt, 256, 512)
    sh, sc = _mods(ada_kv[:bp], 1)
    p_cmp, p_sel, p_win, sel_bf, selt_bf, win_bf, wint_bf = _proj_kv(
        xp, norm_kv[None, :], sh, sc, q['w_kv'], q['bd'], q['k_gain'], q['k_mask'], 256)
    kcvc_p = _compress(p_cmp, q['pos_rows'], q['w_cmp1'], q['w_cmp2'], q['k_gain0'],
                       min(CMP_BLOCKS_PER_STEP, tp // BLOCK))
    mods1 = _mods(ada[1][:bp], 1)
    qs_p, gt_p = _proj_q(xp, norm_mix[1][None, :], mods1[0], mods1[1], q['w_q'], q['bd'], q['q_gain'],
                         q['b_gate'], 256)
    attn_p = _attn_prompt(rel_bias, qs_p, gt_p, kcvc_p, sel_bf, selt_bf, win_bf, wint_bf, bias_tiles)
    y_prompt = _layer1_tail(xp, attn_p, mods1, p, q, 256, 512).reshape(bp, tp, d)

    ms = bs * ts
    xs = x_sample.reshape(ms, d)
    m0 = jnp.broadcast_to(jnp.pad(state_m[0], ((0, 0), (0, 8 - NH_A)))[:, :, None], (bs, 8, LANES))

    def mlstm_sample(proj):
        proj8 = jnp.pad(proj.reshape(bs, ts, -1), ((0, 0), (0, 8 - ts), (0, 0)))
        y, c, n, m = _mlstm(proj8, q['b_if'], head_gain, state_C[0], state_n[0][:, :, None, :], m0,
                            rows=8, valid=ts, chunk=LANES, zero_init=False)
        return y[:, :ts].reshape(ms, -1), (c, n, m)

    xs, (sc_, sn_, sm_) = _layer0(xs, _mods(ada[0][bp:nc], ts), p, q, mlstm_sample, ms, ms)
    sh, sc = _mods(ada_kv[bp:nc], ts)
    s_cmp, s_sel, s_win = _proj_kv(xs, norm_kv[None, :], sh, sc, q['w_kv'], q['bd'], q['k_gain'],
                                   q['k_mask'], ms)[:3]
    kcvc_s = _compress_paged(cache_cmp.reshape(cache_cmp.shape[0], PAGE, kvw), page_table, q['pos_rows'],
                             q['w_cmp1'], q['w_cmp2'], q['k_gain0'], min(CMP_BLOCKS_PER_STEP, past // BLOCK))
    mods1 = _mods(ada[1][bp:nc], ts)
    qs_s, gt_s = _proj_q(xs, norm_mix[1][None, :], mods1[0], mods1[1], q['w_q'], q['bd'], q['q_gain'],
                         q['b_gate'], ms)

    def pad8(a):
        return jnp.pad(a.reshape(bs, ts, -1), ((0, 0), (0, 8 - ts), (0, 0)))

    qs8 = pad8(qs_s)
    oct_s, idx_s = _attn_s1(rel_bias, qs8, kcvc_s, past)
    ids = jnp.transpose(idx_s.reshape(bs, NKV, TOPK, TQ)[:, :, :, :ts], (0, 1, 3, 2))
    idc = jnp.maximum(ids, 0)
    pages = jnp.take_along_axis(page_table, (idc // 2).reshape(bs, -1), axis=1).reshape(ids.shape)
    phys = pages * 2 + idc % 2
    gt8 = jnp.pad(jnp.transpose(gt_s.reshape(LANES, bs, ts), (1, 0, 2)), ((0, 0), (0, 0), (0, TQ - ts)))
    attn_s = _attn_s2(ids.reshape(-1), phys.reshape(-1), qs8, gt8, oct_s,
                      cache_sel.reshape(cache_sel.shape[0] * 2, BLOCK, kvw), pad8(s_sel),
                      cache_win.reshape(bs, wbuf, kvw), pad8(s_win), bias_tiles, past, ts)
    y_sample = _layer1_tail(xs, attn_s[:, :ts].reshape(ms, width), mods1, p, q, ms, ms).reshape(bs, ts, d)

    rows5 = (NKV, 2, HD)
    p_win_out = p_win[tp - min(WINDOW, tp):].reshape(bp, -1, *rows5)
    s_win_all = jnp.concatenate([cache_win, s_win.reshape(bs, ts, *rows5)], axis=1)
    return (y_prompt, y_sample,
            pc[None], pn.reshape(1, bp, NH_A, DK_A), pm[None, :, :NH_A, 0],
            p_cmp.reshape(bp, tp, *rows5), p_sel.reshape(bp, tp, *rows5), p_win_out,
            sc_[None], sn_.reshape(1, bs, NH_A, DK_A), sm_[None, :, :NH_A, 0],
            s_cmp.reshape(bs, ts, *rows5), s_sel.reshape(bs, ts, *rows5),
            s_win_all[:, -min(WINDOW, wbuf + ts):])
```

```python
import functools
import math

import numpy as np
import jax
import jax.numpy as jnp
from jax import lax
from jax.experimental import pallas as pl
from jax.experimental.pallas import tpu as pltpu

F32 = jnp.float32
BF16 = jnp.bfloat16
I32 = jnp.int32

D_MODEL = 1024
NH_A = 4
DK_A = 256
NKV = 4
REP = 4
HD = 64
BLOCK = 64
TOPK = 16
WINDOW = 512
N_BUCKETS = 32
REL_MAX_DIST = 2048
PAGE = 128
EPS = 1e-6
NEG = -1e30

LANES = 128
VMEM_LIMIT = 56 * 1024 * 1024
TQ = 128
KT = 256
N_BIAS_TILES = 14
CMP_BLOCKS_PER_STEP = 64


def _bucket_thresholds():
    exact = N_BUCKETS // 2
    d = np.arange(1, 4 * REL_MAX_DIST, dtype=np.float64)
    big = exact + np.floor(np.log(d / exact) / math.log(REL_MAX_DIST / exact) * (N_BUCKETS - exact)).astype(np.int64)
    b = np.where(d < exact, d.astype(np.int64), np.minimum(big, N_BUCKETS - 1))
    return [int(d[np.argmax(b >= k)]) for k in range(1, N_BUCKETS)]


BUCKET_THR = _bucket_thresholds()
assert 128 * (N_BIAS_TILES - 1) - (TQ - 1) >= BUCKET_THR[-1]


def _cparams(*sem):
    return pltpu.CompilerParams(dimension_semantics=sem, vmem_limit_bytes=VMEM_LIMIT)


def _dot(a, b):
    return jnp.dot(a.astype(BF16), b.astype(BF16), preferred_element_type=F32)


def _dot_nt(a, b):
    return lax.dot_general(a.astype(BF16), b.astype(BF16), (((1,), (1,)), ((), ())),
                           preferred_element_type=F32)


def _split3(x):
    hi = x.astype(BF16)
    r1 = x - hi.astype(F32)
    mid = r1.astype(BF16)
    lo = (r1 - mid.astype(F32)).astype(BF16)
    return hi, mid, lo


def _sigmoid(x):
    return 1.0 / (1.0 + jnp.exp(-x))


def _norm_mod(x, gain, shift, scale):
    ms = jnp.mean(x * x, axis=-1, keepdims=True)
    y = x * lax.rsqrt(ms + EPS) * gain
    return y * (1.0 + scale) + shift


def _seg_mean_sq(y, bd):
    parts = []
    for j in range(y.shape[1] // 256):
        sq = y[:, j * 256:(j + 1) * 256]
        sq = sq * sq
        hi = sq.astype(BF16)
        lo = (sq - hi.astype(F32)).astype(BF16)
        parts.append(jnp.dot(hi, bd, preferred_element_type=F32) + jnp.dot(lo, bd, preferred_element_type=F32))
    return parts[0] if len(parts) == 1 else jnp.concatenate(parts, axis=1)


def _row_spec(tm, n, per_row):
    if per_row:
        return pl.BlockSpec((tm, n), lambda i: (i, 0))
    return pl.BlockSpec((1, n), lambda i: (0, 0))


def _const_spec(shape):
    return pl.BlockSpec(shape, lambda i: tuple(0 for _ in shape))


def _ada_kernel(c_ref, w_ref, b_ref, o_ref):
    c = c_ref[...]
    o_ref[...] = _dot(c * _sigmoid(c), w_ref[...]) + b_ref[...]


def _ada(c, w, b):
    m, k = c.shape
    n = w.shape[1]
    tn = 1024
    return pl.pallas_call(
        _ada_kernel,
        grid=(n // tn,),
        in_specs=[pl.BlockSpec((m, k), lambda j: (0, 0)),
                  pl.BlockSpec((k, tn), lambda j: (0, j)),
                  pl.BlockSpec((1, tn), lambda j: (0, j))],
        out_specs=pl.BlockSpec((m, tn), lambda j: (0, j)),
        out_shape=jax.ShapeDtypeStruct((m, n), F32),
        compiler_params=_cparams("arbitrary"),
        name="ada",
    )(c, w, b.reshape(1, n))


def _proj_plain_kernel(x_ref, gain_ref, sh_ref, sc_ref, w_ref, o_ref):
    xn = _norm_mod(x_ref[...], gain_ref[...], sh_ref[...], sc_ref[...])
    o_ref[...] = jnp.dot(xn.astype(BF16), w_ref[...], preferred_element_type=F32)


def _proj_plain(x, gain, shift, scale, w_bf, tm):
    m, d = x.shape
    n = w_bf.shape[1]
    per_row = shift.shape[0] != 1
    return pl.pallas_call(
        _proj_plain_kernel,
        grid=(m // tm,),
        in_specs=[pl.BlockSpec((tm, d), lambda i: (i, 0)),
                  _const_spec((1, d)),
                  _row_spec(tm, d, per_row), _row_spec(tm, d, per_row),
                  _const_spec((d, n))],
        out_specs=pl.BlockSpec((tm, n), lambda i: (i, 0)),
        out_shape=jax.ShapeDtypeStruct((m, n), F32),
        compiler_params=_cparams("arbitrary"),
        name="proj_mlstm",
    )(x, gain, shift, scale, w_bf)


def _proj_q_kernel(x_ref, gain_ref, sh_ref, sc_ref, w_ref, bd_ref, qg_ref, bg_ref, q_ref, gt_ref):
    xn = _norm_mod(x_ref[...], gain_ref[...], sh_ref[...], sc_ref[...])
    y = jnp.dot(xn.astype(BF16), w_ref[...], preferred_element_type=F32)
    nq = q_ref.shape[1]
    yq = y[:, :nq]
    ms = _seg_mean_sq(yq, bd_ref[...])
    q_ref[...] = yq * lax.rsqrt(ms + EPS) * qg_ref[...]
    gates = _sigmoid(y[:, nq:] + bg_ref[...])
    gt_ref[...] = gates.T


def _proj_q(x, gain, shift, scale, w_bf, bd, qgain_row, bgate_row, tm):
    m, d = x.shape
    n = w_bf.shape[1]
    nq = n - LANES
    per_row = shift.shape[0] != 1
    return pl.pallas_call(
        _proj_q_kernel,
        grid=(m // tm,),
        in_specs=[pl.BlockSpec((tm, d), lambda i: (i, 0)),
                  _const_spec((1, d)),
                  _row_spec(tm, d, per_row), _row_spec(tm, d, per_row),
                  _const_spec((d, n)), _const_spec((256, 256)),
                  _const_spec((1, nq)), _const_spec((1, LANES))],
        out_specs=[pl.BlockSpec((tm, nq), lambda i: (i, 0)),
                   pl.BlockSpec((LANES, tm), lambda i: (0, i))],
        out_shape=[jax.ShapeDtypeStruct((m, nq), F32),
                   jax.ShapeDtypeStruct((LANES, m), F32)],
        compiler_params=_cparams("arbitrary"),
        name="proj_q",
    )(x, gain, shift, scale, w_bf, bd, qgain_row, bgate_row)


def _proj_kv_kernel(x_ref, gain_ref, sh_ref, sc_ref, w_ref, bd_ref, kg_ref, km_ref,
                    cmp_ref, sel_ref, win_ref, cmpt_ref, selt_ref, wint_ref,
                    selb_ref, seltb_ref, winb_ref, wintb_ref):
    xn = _norm_mod(x_ref[...], gain_ref[...], sh_ref[...], sc_ref[...])
    y = jnp.dot(xn.astype(BF16), w_ref[...], preferred_element_type=F32)
    w = cmp_ref.shape[1]
    cmp = y[:, :w]
    ykn = y[:, w:]
    ms = _seg_mean_sq(ykn, bd_ref[...])
    ykn = jnp.where(km_ref[...] > 0.5, ykn * lax.rsqrt(ms + EPS) * kg_ref[...], ykn)
    sel = ykn[:, :w]
    win = ykn[:, w:]
    sel_t = sel.T
    win_t = win.T
    cmp_ref[...] = cmp
    sel_ref[...] = sel
    win_ref[...] = win
    cmpt_ref[...] = cmp.T
    selt_ref[...] = sel_t
    wint_ref[...] = win_t
    selb_ref[...] = sel.astype(BF16)
    winb_ref[...] = win.astype(BF16)
    seltb_ref[...] = sel_t.astype(BF16)
    wintb_ref[...] = win_t.astype(BF16)


def _proj_kv(x, gain, shift, scale, w_bf, bd, kgain_row, kmask_row, tm):
    m, d = x.shape
    n = w_bf.shape[1]
    w = n // 3
    per_row = shift.shape[0] != 1
    row = pl.BlockSpec((tm, w), lambda i: (i, 0))
    col = pl.BlockSpec((w, tm), lambda i: (0, i))
    return pl.pallas_call(
        _proj_kv_kernel,
        grid=(m // tm,),
        in_specs=[pl.BlockSpec((tm, d), lambda i: (i, 0)),
                  _const_spec((1, d)),
                  _row_spec(tm, d, per_row), _row_spec(tm, d, per_row),
                  _const_spec((d, n)), _const_spec((256, 256)),
                  _const_spec((1, 2 * w)), _const_spec((1, 2 * w))],
        out_specs=[row, row, row, col, col, col, row, col, row, col],
        out_shape=[jax.ShapeDtypeStruct((m, w), F32)] * 3 + [jax.ShapeDtypeStruct((w, m), F32)] * 3
                  + [jax.ShapeDtypeStruct((m, w), BF16), jax.ShapeDtypeStruct((w, m), BF16),
                     jax.ShapeDtypeStruct((m, w), BF16), jax.ShapeDtypeStruct((w, m), BF16)],
        compiler_params=_cparams("arbitrary"),
        name="proj_kv",
    )(x, gain, shift, scale, w_bf, bd, kgain_row, kmask_row)


def _out_proj_kernel(x_ref, g_ref, y_ref, w_ref, o_ref):
    o_ref[...] = x_ref[...] + g_ref[...] * jnp.dot(y_ref[...].astype(BF16), w_ref[...],
                                                   preferred_element_type=F32)


def _out_proj(x, g, y, w_bf, tm):
    m, d = x.shape
    k = y.shape[1]
    per_row = g.shape[0] != 1
    return pl.pallas_call(
        _out_proj_kernel,
        grid=(m // tm,),
        in_specs=[pl.BlockSpec((tm, d), lambda i: (i, 0)),
                  _row_spec(tm, d, per_row),
                  pl.BlockSpec((tm, k), lambda i: (i, 0)),
                  _const_spec((k, d))],
        out_specs=pl.BlockSpec((tm, d), lambda i: (i, 0)),
        out_shape=jax.ShapeDtypeStruct((m, d), F32),
        compiler_params=_cparams("arbitrary"),
        name="out_proj",
    )(x, g, y, w_bf)


def _ffn_kernel(x_ref, gain_ref, sh_ref, sc_ref, g_ref, w1_ref, w2_ref, o_ref, *, fc):
    x = x_ref[...]
    xn = _norm_mod(x, gain_ref[...], sh_ref[...], sc_ref[...]).astype(BF16)
    acc = jnp.zeros(x.shape, F32)
    for c in range(w1_ref.shape[1] // fc):
        h = jnp.dot(xn, w1_ref[:, c * fc:(c + 1) * fc], preferred_element_type=F32)
        h = jnp.maximum(h, 0.0)
        acc = acc + jnp.dot((h * h).astype(BF16), w2_ref[c * fc:(c + 1) * fc, :],
                            preferred_element_type=F32)
    o_ref[...] = x + g_ref[...] * acc


def _ffn(x, gain, shift, scale, g, w1_bf, w2_bf, tm):
    m, d = x.shape
    f = w1_bf.shape[1]
    per_row = shift.shape[0] != 1
    return pl.pallas_call(
        functools.partial(_ffn_kernel, fc=1024),
        grid=(m // tm,),
        in_specs=[pl.BlockSpec((tm, d), lambda i: (i, 0)),
                  _const_spec((1, d)),
                  _row_spec(tm, d, per_row), _row_spec(tm, d, per_row), _row_spec(tm, d, per_row),
                  _const_spec((d, f)), _const_spec((f, d))],
        out_specs=pl.BlockSpec((tm, d), lambda i: (i, 0)),
        out_shape=jax.ShapeDtypeStruct((m, d), F32),
        compiler_params=_cparams("arbitrary"),
        name="ffn",
    )(x, gain, shift, scale, g, w1_bf, w2_bf)


def _mlstm_kernel(q_ref, k_ref, v_ref, o_ref, gt_ref, bif_ref, hg_ref, c0_ref, n0_ref, m0_ref,
                  y_ref, cout_ref, nout_ref, mout_ref, c_scr, n_scr, m_scr, *, rows, valid, chunk, zero_init):
    ci = pl.program_id(1)
    nci = pl.num_programs(1)

    @pl.when(ci == 0)
    def _():
        if zero_init:
            c_scr[...] = jnp.zeros(c_scr.shape, F32)
            n_scr[...] = jnp.zeros(n_scr.shape, F32)
            m_scr[...] = jnp.zeros(m_scr.shape, F32)
        else:
            c_scr[...] = c0_ref[0]
            n_scr[...] = n0_ref[0]
            m_scr[...] = m0_ref[0]

    L = chunk

    def padded(ref_val, fill):
        if rows == L:
            return ref_val
        pad = jnp.full((L - rows, ref_val.shape[1]), fill, ref_val.dtype)
        return jnp.concatenate([ref_val, pad], axis=0)

    q = padded(q_ref[0], 0.0)
    k = padded(k_ref[0], 0.0)
    v = padded(v_ref[0], 0.0)
    og = padded(o_ref[0], 0.0)
    g = padded(gt_ref[0], 0.0) + bif_ref[...]
    lf = -(jnp.maximum(-g, 0.0) + jnp.log1p(jnp.exp(-jnp.abs(g))))
    li = g
    if valid != L:
        is_real = lax.broadcasted_iota(I32, (L, LANES), 0) < valid
        lf = jnp.where(is_real, lf, 0.0)
        li = jnp.where(is_real, li, NEG)
    r_io = lax.broadcasted_iota(I32, (L, L), 0)
    c_io = lax.broadcasted_iota(I32, (L, L), 1)
    causal = c_io <= r_io
    tril = jnp.where(causal, 1.0, 0.0).astype(BF16)
    hi, mid, lo = _split3(lf)
    b = (jnp.dot(tril, hi, preferred_element_type=F32) + jnp.dot(tril, mid, preferred_element_type=F32)
         + jnp.dot(tril, lo, preferred_element_type=F32))
    lane = lax.broadcasted_iota(I32, (L, LANES), 1)
    mixed_t = jnp.where(lane < NH_A, li, b).T
    for h in range(NH_A):
        sl = slice(h * DK_A, (h + 1) * DK_A)
        qh = q[:, sl] * (DK_A ** -0.5)
        kh = k[:, sl]
        vh = v[:, sl]
        b_col = b[:, NH_A + h:NH_A + h + 1]
        li_col = li[:, h:h + 1]
        b_row = mixed_t[NH_A + h:NH_A + h + 1, :]
        li_row = mixed_t[h:h + 1, :]
        m_prev = m_scr[h:h + 1, 0:1]
        c_prev = c_scr[h]
        n_prev = n_scr[h]
        dmat = jnp.where(causal, b_col - b_row + li_row, NEG)
        m_inter = b_col + m_prev
        m_t = jnp.maximum(m_inter, jnp.max(dmat, axis=-1, keepdims=True))
        s = _dot_nt(qh, kh) * jnp.exp(dmat - m_t)
        a_inter = jnp.exp(m_inter - m_t)
        num = _dot(s, vh) + a_inter * _dot(qh, c_prev)
        den = jnp.sum(s, axis=-1, keepdims=True) + a_inter * jnp.sum(qh * n_prev, axis=-1, keepdims=True)
        hh = num / jnp.maximum(jnp.abs(den), jnp.exp(-m_t))
        b_last = b_col[L - 1:L, :]
        g_col = b_last - b_col + li_col
        m_new = jnp.maximum(b_last + m_prev, jnp.max(g_col, axis=0, keepdims=True))
        w_col = jnp.exp(g_col - m_new)
        decay = jnp.exp(b_last + m_prev - m_new)
        kw = kh * w_col
        c_scr[h] = decay * c_prev + _dot(kw.T, vh)
        n_scr[h] = decay * n_prev + jnp.sum(kw, axis=0, keepdims=True)
        m_scr[h:h + 1, :] = jnp.broadcast_to(m_new, (1, LANES))
        hn = hh * lax.rsqrt(jnp.mean(hh * hh, axis=-1, keepdims=True) + EPS) * hg_ref[h:h + 1, :]
        yh = hn * _sigmoid(og[:, sl])
        y_ref[0, :, sl] = yh[:rows]

    @pl.when(ci == nci - 1)
    def _():
        cout_ref[0] = c_scr[...]
        nout_ref[0] = n_scr[...]
        mout_ref[0] = m_scr[...]


def _mlstm(proj, b_if_row, head_gain, c0, n0, m0, *, rows, valid, chunk, zero_init):
    batch, t, _ = proj.shape
    nch = t // rows
    inner = NH_A * DK_A

    def colblk(j, width):
        return pl.BlockSpec((1, rows, width), lambda b, c, j=j: (b, c, j))

    state_c = pl.BlockSpec((1, NH_A, DK_A, DK_A), lambda b, c: (b, 0, 0, 0))
    state_n = pl.BlockSpec((1, NH_A, 1, DK_A), lambda b, c: (b, 0, 0, 0))
    state_m = pl.BlockSpec((1, 8, LANES), lambda b, c: (b, 0, 0))
    return pl.pallas_call(
        functools.partial(_mlstm_kernel, rows=rows, valid=valid, chunk=chunk, zero_init=zero_init),
        grid=(batch, nch),
        in_specs=[colblk(0, inner), colblk(1, inner), colblk(2, inner), colblk(3, inner),
                  colblk(4 * inner // LANES, LANES),
                  pl.BlockSpec((1, LANES), lambda b, c: (0, 0)),
                  pl.BlockSpec((NH_A, DK_A), lambda b, c: (0, 0)),
                  state_c, state_n, state_m],
        out_specs=[pl.BlockSpec((1, rows, inner), lambda b, c: (b, c, 0)), state_c, state_n, state_m],
        out_shape=[jax.ShapeDtypeStruct((batch, t, inner), F32),
                   jax.ShapeDtypeStruct((batch, NH_A, DK_A, DK_A), F32),
                   jax.ShapeDtypeStruct((batch, NH_A, 1, DK_A), F32),
                   jax.ShapeDtypeStruct((batch, 8, LANES), F32)],
        scratch_shapes=[pltpu.VMEM((NH_A, DK_A, DK_A), F32), pltpu.VMEM((NH_A, 1, DK_A), F32),
                        pltpu.VMEM((8, LANES), F32)],
        compiler_params=_cparams("arbitrary", "arbitrary"),
        name="mlstm",
    )(proj, proj, proj, proj, proj, b_if_row, head_gain, c0, n0, m0)


def _compress_body(r_refs, pos_ref, w1_ref, w2_ref, kg_ref, o_ref, nb):
    acc = None
    for lp in range(BLOCK // 2):
        pieces = []
        for g in range(NKV):
            xa = r_refs[g][pl.ds(2 * lp, nb, stride=BLOCK), :]
            xb = r_refs[g][pl.ds(2 * lp + 1, nb, stride=BLOCK), :]
            pieces.append(jnp.concatenate([xa, xb], axis=1))
        pa = pos_ref[2 * lp:2 * lp + 1, 0:LANES]
        pb = pos_ref[2 * lp + 1:2 * lp + 2, 0:LANES]
        pieces.append(jnp.broadcast_to(jnp.concatenate([pa, pb], axis=1), (8, 2 * LANES)))
        x = jnp.concatenate(pieces, axis=0).astype(BF16)
        d = jnp.dot(x, w1_ref[lp], preferred_element_type=F32)
        acc = d if acc is None else acc + d
    pre = acc[:NKV * nb] + acc[NKV * nb:NKV * nb + 1]
    hid = pre * _sigmoid(pre)
    out = jnp.dot(hid.astype(BF16), w2_ref[...], preferred_element_type=F32)
    is_k = lax.broadcasted_iota(I32, out.shape, 1) < HD
    ms = jnp.sum(jnp.where(is_k, out * out, 0.0), axis=-1, keepdims=True) * (1.0 / HD)
    out = jnp.where(is_k, out * lax.rsqrt(ms + EPS) * kg_ref[...], out)
    for g in range(NKV):
        o_ref[:, g * LANES:(g + 1) * LANES] = out[g * nb:(g + 1) * nb]


def _compress_kernel(r0, r1, r2, r3, pos_ref, w1_ref, w2_ref, kg_ref, o_ref, *, nb):
    _compress_body((r0, r1, r2, r3), pos_ref, w1_ref, w2_ref, kg_ref, o_ref, nb)


def _compress_paged_kernel(pt_ref, *refs, nb):
    npages = nb * BLOCK // PAGE
    pages = refs[:npages]
    pos_ref, w1_ref, w2_ref, kg_ref, o_ref = refs[npages:npages + 5]
    r_scrs = refs[npages + 5:]
    for p in range(npages):
        for g in range(NKV):
            r_scrs[g][p * PAGE:(p + 1) * PAGE, :] = pages[p][0, g * LANES:(g + 1) * LANES, :].T
    _compress_body(r_scrs, pos_ref, w1_ref, w2_ref, kg_ref, o_ref, nb)


def _compress_paged(cache_t, page_table, pos_rows, w1p, w2p, kgain_row, nb):
    batch, npg = page_table.shape
    w = cache_t.shape[1]
    npages = nb * BLOCK // PAGE
    steps = npg // npages

    def page_spec(p):
        return pl.BlockSpec((1, w, PAGE), lambda b, j, pt, p=p: (pt[b, j * npages + p], 0, 0))

    def const(shape):
        return pl.BlockSpec(shape, lambda b, j, pt: tuple(0 for _ in shape))

    return pl.pallas_call(
        functools.partial(_compress_paged_kernel, nb=nb),
        grid_spec=pltpu.PrefetchScalarGridSpec(
            num_scalar_prefetch=1,
            grid=(batch, steps),
            in_specs=[page_spec(p) for p in range(npages)]
                     + [const((BLOCK, w)), const(w1p.shape), const(w2p.shape), const((1, LANES))],
            out_specs=pl.BlockSpec((nb, w), lambda b, j, pt: (b * steps + j, 0)),
            scratch_shapes=[pltpu.VMEM((nb * BLOCK, LANES), F32) for _ in range(NKV)]),
        out_shape=jax.ShapeDtypeStruct((batch * npg * PAGE // BLOCK, w), F32),
        compiler_params=_cparams("arbitrary", "arbitrary"),
        name="compress_paged",
    )(page_table, *([cache_t] * npages), pos_rows, w1p, w2p, kgain_row)


def _compress(rows, pos_rows, w1p, w2p, kgain_row, nb):
    t, w = rows.shape
    nblk = t // BLOCK
    return pl.pallas_call(
        functools.partial(_compress_kernel, nb=nb),
        grid=(nblk // nb,),
        in_specs=[pl.BlockSpec((nb * BLOCK, LANES), lambda i, g=g: (i, g)) for g in range(NKV)]
                 + [_const_spec((BLOCK, w)), _const_spec(w1p.shape), _const_spec(w2p.shape),
                    _const_spec((1, LANES))],
        out_specs=pl.BlockSpec((nb, w), lambda i: (i, 0)),
        out_shape=jax.ShapeDtypeStruct((nblk, w), F32),
        compiler_params=_cparams("arbitrary"),
        name="compress",
    )(rows, rows, rows, rows, pos_rows, w1p, w2p, kgain_row)


def _bias_from_dist(dist, tab_ref, h):
    out = jnp.full(dist.shape, tab_ref[0, h], F32)
    for kk in range(1, N_BUCKETS):
        out = jnp.where(dist >= BUCKET_THR[kk - 1], tab_ref[kk, h], out)
    return out


def _bias_tiles_kernel(tab_ref, o_ref):
    a = pl.program_id(0)
    h = pl.program_id(1)
    sj = lax.broadcasted_iota(I32, (LANES, TQ), 0)
    ti = lax.broadcasted_iota(I32, (LANES, TQ), 1)
    o_ref[0, 0] = _bias_from_dist(a * LANES + ti - sj, tab_ref, h)


def _bias_tiles(rel_bias):
    nh = rel_bias.shape[1]
    return pl.pallas_call(
        _bias_tiles_kernel,
        grid=(N_BIAS_TILES, nh),
        in_specs=[pl.BlockSpec(memory_space=pltpu.SMEM)],
        out_specs=pl.BlockSpec((1, 1, LANES, TQ), lambda a, h: (a, h, 0, 0)),
        out_shape=jax.ShapeDtypeStruct((N_BIAS_TILES, nh, LANES, TQ), F32),
        compiler_params=_cparams("arbitrary", "arbitrary"),
        name="bias_tiles",
    )(rel_bias)


def _stack_heads(q):
    z = jnp.zeros((q.shape[0], HD), q.dtype)
    return jnp.concatenate([jnp.concatenate([q[:, r * HD:(r + 1) * HD], z], axis=1) for r in range(REP)],
                           axis=0)


def _tile_heads(x):
    return jnp.concatenate([x] * REP, axis=1)


def _pad_rows(x, n):
    if x.shape[0] == n:
        return x
    return jnp.concatenate([x, jnp.zeros((n - x.shape[0],) + x.shape[1:], x.dtype)], axis=0)


def _cmp_branch(qc, kcvc, tab_ref, g, pos):
    tq = qc.shape[0]
    nb = kcvc.shape[0]
    lt = _dot_nt(kcvc, _stack_heads(qc))
    n_io = lax.broadcasted_iota(I32, (nb, tq), 0)
    dist = pos - (n_io * BLOCK + (BLOCK - 1))
    vis = dist >= 0
    probs = []
    for r in range(REP):
        x = lt[:, r * tq:(r + 1) * tq] + _bias_from_dist(dist, tab_ref, g * REP + r)
        x = jnp.where(vis, x, NEG)
        e = jnp.exp(x - jnp.max(x, axis=0, keepdims=True))
        p = e / jnp.sum(e, axis=0, keepdims=True)
        probs.append(jnp.where(vis, p, 0.0))
    score = probs[0] + probs[1] + probs[2] + probs[3]
    oc_t = _dot(kcvc.T[HD:, :], jnp.concatenate(probs, axis=1))
    cand = n_io < pos // BLOCK
    return oc_t, jnp.where(cand, score, -jnp.inf)


def _topk_rows(score, n_top, pick_fn):
    nb, tq = score.shape
    n_f = lax.broadcasted_iota(I32, (nb, tq), 0).astype(F32)
    s = score
    for it in range(n_top):
        mx = jnp.max(s, axis=0, keepdims=True)
        idx = jnp.min(jnp.where(s == mx, n_f, float(nb)), axis=0, keepdims=True)
        ok = mx > -jnp.inf
        hit = n_f == idx
        pick_fn(it, idx, ok, hit)
        s = jnp.where(hit, -jnp.inf, s)


def _flash_update(carry, x, vt):
    m, l, acc = carry
    m_new = jnp.maximum(m, jnp.max(x, axis=0, keepdims=True))
    alpha = jnp.exp(m - m_new)
    p = jnp.exp(x - m_new)
    l = alpha * l + jnp.sum(p, axis=0, keepdims=True)
    acc = alpha * acc + jnp.dot(vt, p.astype(BF16), preferred_element_type=F32)
    return m_new, l, acc


def _flash_init(nl):
    return (jnp.full((1, nl), NEG, F32), jnp.zeros((1, nl), F32), jnp.zeros((HD, nl), F32))


def _untranspose_heads(o_t, tq):
    halves = []
    for p in range(REP // 2):
        pair = jnp.concatenate([o_t[:, (2 * p) * tq:(2 * p + 1) * tq],
                                o_t[:, (2 * p + 1) * tq:(2 * p + 2) * tq]], axis=0)
        halves.append(pair.T)
    return jnp.concatenate(halves, axis=1)


def _gate_row(gt_ref, br, g):
    return jnp.concatenate([gt_ref[pl.ds(br * NKV * REP + g * REP + r, 1), :] for r in range(REP)], axis=1)


def _attn_prompt_kernel(tab_ref, qc_ref, qs_ref, qw_ref, gt_ref, kcvc_ref, ksel_ref, vselt_ref,
                        kw0, kw1, kw2, kw3, kw4, vw0, vw1, vw2, vw3, vw4, bt_ref, o_ref, sel_scr):
    g = pl.program_id(0)
    i = pl.program_id(1)
    tq = TQ
    nl = REP * tq
    t0 = i * tq
    pos = t0 + lax.broadcasted_iota(I32, (1, tq), 1)
    pos4 = _tile_heads(pos)

    oc_t, score = _cmp_branch(qc_ref[...], kcvc_ref[...], tab_ref, g, pos)
    nb = score.shape[0]
    n_io = lax.broadcasted_iota(I32, (nb, tq), 0)
    sel_scr[...] = jnp.where(n_io == pos // BLOCK, 1.0, 0.0)

    def pick(it, idx, ok, hit):
        sel_scr[...] = jnp.where(hit, jnp.maximum(sel_scr[...], jnp.where(ok, 1.0, 0.0)), sel_scr[...])

    _topk_rows(score, TOPK - 1, pick)

    qs_bf = _stack_heads(qs_ref[...]).astype(BF16)
    kio = lax.broadcasted_iota(I32, (KT, nl), 0)

    def sel_body(j, carry):
        s0 = pl.multiple_of(j * KT, KT)
        lt = _dot_nt(ksel_ref[pl.ds(s0, KT), :], qs_bf)
        bias_rows = []
        for a in range(KT // LANES):
            didx = jnp.clip(i - (KT // LANES) * j - a, 0, N_BIAS_TILES - 1)
            bias_rows.append(jnp.concatenate([bt_ref[didx, r] for r in range(REP)], axis=1))
        lt = lt + jnp.concatenate(bias_rows, axis=0)
        blk_rows = [jnp.broadcast_to(sel_scr[pl.ds((KT // BLOCK) * j + b, 1), :], (BLOCK, tq))
                    for b in range(KT // BLOCK)]
        selm = _tile_heads(jnp.concatenate(blk_rows, axis=0))
        x = jnp.where(kio + s0 <= pos4, jnp.where(selm > 0.5, lt, NEG), NEG)
        vt = vselt_ref[HD:2 * HD, pl.ds(s0, KT)]
        return _flash_update(carry, x, vt)

    n_tiles = (t0 + tq + KT - 1) // KT
    ms, ls, accs = lax.fori_loop(0, n_tiles, sel_body, _flash_init(nl))

    qw_bf = _stack_heads(qw_ref[...]).astype(BF16)
    kw_refs = (kw0, kw1, kw2, kw3, kw4)
    vw_refs = (vw0, vw1, vw2, vw3, vw4)
    sj = lax.broadcasted_iota(I32, (LANES, nl), 0)
    ti4 = _tile_heads(lax.broadcasted_iota(I32, (1, tq), 1))
    carry = _flash_init(nl)
    for j in range(5):
        delta = LANES * (4 - j)
        lt = _dot_nt(kw_refs[j][...], qw_bf)
        lt = lt + jnp.concatenate([bt_ref[4 - j, r] for r in range(REP)], axis=1)
        dist = delta + ti4 - sj
        ok = jnp.where((dist >= 0) & (dist <= WINDOW), 1.0, 0.0) * jnp.where(i - 4 + j >= 0, 1.0, 0.0)
        x = jnp.where(ok > 0.5, lt, NEG)
        carry = _flash_update(carry, x, vw_refs[j][HD:2 * HD, :])
    mw, lw, accw = carry

    out_t = (_gate_row(gt_ref, 0, g) * oc_t + _gate_row(gt_ref, 1, g) * (accs / ls)
             + _gate_row(gt_ref, 2, g) * (accw / lw))
    o_ref[...] = _untranspose_heads(out_t, tq)


def _attn_prompt(rel_bias, qs, gates_t, kcvc, sel_bf, selt_bf, win_bf, wint_bf, bias_tiles):
    t = qs.shape[0]
    nb = kcvc.shape[0]
    width = NKV * REP * HD

    def qspec(br):
        return pl.BlockSpec((TQ, REP * HD), lambda g, i, br=br: (i, br * NKV + g))

    def kw_spec(j):
        return pl.BlockSpec((LANES, LANES), lambda g, i, j=j: (jnp.maximum(i - 4 + j, 0), g))

    def vw_spec(j):
        return pl.BlockSpec((LANES, LANES), lambda g, i, j=j: (g, jnp.maximum(i - 4 + j, 0)))

    return pl.pallas_call(
        _attn_prompt_kernel,
        grid=(NKV, t // TQ),
        in_specs=[pl.BlockSpec(memory_space=pltpu.SMEM),
                  qspec(0), qspec(1), qspec(2),
                  pl.BlockSpec((LANES, TQ), lambda g, i: (0, i)),
                  pl.BlockSpec((nb, LANES), lambda g, i: (0, g)),
                  pl.BlockSpec((t, LANES), lambda g, i: (0, g)),
                  pl.BlockSpec((LANES, t), lambda g, i: (g, 0))]
                 + [kw_spec(j) for j in range(5)] + [vw_spec(j) for j in range(5)]
                 + [pl.BlockSpec((N_BIAS_TILES, REP, LANES, TQ), lambda g, i: (0, g, 0, 0))],
        out_specs=pl.BlockSpec((TQ, REP * HD), lambda g, i: (i, g)),
        out_shape=jax.ShapeDtypeStruct((t, width), F32),
        scratch_shapes=[pltpu.VMEM((nb, TQ), F32)],
        compiler_params=_cparams("arbitrary", "arbitrary"),
        name="attn_prompt",
    )(rel_bias, qs, qs, qs, gates_t, kcvc, sel_bf, selt_bf,
      win_bf, win_bf, win_bf, win_bf, win_bf, wint_bf, wint_bf, wint_bf, wint_bf, wint_bf, bias_tiles)


def _attn_s1_kernel(tab_ref, qc_ref, kcvc_ref, oct_ref, idx_ref, *, q0):
    g = pl.program_id(1)
    tq = TQ
    pos = q0 + lax.broadcasted_iota(I32, (1, tq), 1)
    oc_t, score = _cmp_branch(_pad_rows(qc_ref[0], tq), kcvc_ref[...], tab_ref, g, pos)
    oct_ref[0] = oc_t

    def pick(it, idx, ok, hit):
        idx_ref[0, it:it + 1, :] = jnp.where(ok, idx, -1.0).astype(I32)

    _topk_rows(score, TOPK - 1, pick)
    idx_ref[0, TOPK - 1:TOPK, :] = jnp.full((1, tq), -1, I32)


def _attn_s1(rel_bias, qs8, kcvc, q0):
    batch = qs8.shape[0]
    nb = kcvc.shape[0] // batch
    return pl.pallas_call(
        functools.partial(_attn_s1_kernel, q0=q0),
        grid=(batch, NKV),
        in_specs=[pl.BlockSpec(memory_space=pltpu.SMEM),
                  pl.BlockSpec((1, 8, REP * HD), lambda b, g: (b, 0, g)),
                  pl.BlockSpec((nb, LANES), lambda b, g: (b, g))],
        out_specs=[pl.BlockSpec((1, HD, REP * TQ), lambda b, g: (b * NKV + g, 0, 0)),
                   pl.BlockSpec((1, TOPK, TQ), lambda b, g: (b * NKV + g, 0, 0))],
        out_shape=[jax.ShapeDtypeStruct((batch * NKV, HD, REP * TQ), F32),
                   jax.ShapeDtypeStruct((batch * NKV, TOPK, TQ), I32)],
        compiler_params=_cparams("arbitrary", "arbitrary"),
        name="attn_sample_select",
    )(rel_bias, qs8, kcvc)


def _attn_s2_kernel(ids_ref, phys_ref, qs_ref, qw_ref, gt_ref, oct_ref, *refs, q0, nvalid):
    nsel = nvalid * (TOPK - 1)
    kb = refs[:nsel]
    snew_ref, cwin_ref, wnew_ref, bt_ref, o_ref = refs[nsel:]
    b = pl.program_id(0)
    g = pl.program_id(1)
    tq = TQ
    nl = REP * tq
    ti4 = _tile_heads(lax.broadcasted_iota(I32, (1, tq), 1))
    sj = lax.broadcasted_iota(I32, (LANES, nl), 0)
    base = (b * NKV + g) * nvalid * TOPK

    def bias_tile(a):
        return jnp.concatenate([bt_ref[a, r] for r in range(REP)], axis=1)

    def new_rows_step(carry, rows_ref, q_bf):
        blk = _pad_rows(rows_ref[0], LANES)
        x = jnp.where((sj <= ti4) & (sj < nvalid), _dot_nt(blk, q_bf) + bias_tile(0), NEG)
        return _flash_update(carry, x, blk.T[HD:, :].astype(BF16))

    def cached_step(carry, tile_t, q_bf, a, ok):
        x = jnp.where(ok, _dot_nt(tile_t.T, q_bf) + bias_tile(a), NEG)
        return _flash_update(carry, x, tile_t[HD:, :].astype(BF16))

    qs_bf = _stack_heads(_pad_rows(qs_ref[0], tq)).astype(BF16)
    carry = _flash_init(nl)
    for t in range(nvalid):
        for k in range(TOPK - 1):
            n = ids_ref[base + t * TOPK + k]
            nc = jnp.maximum(n, 0)
            a = jnp.clip(q0 // LANES - nc // 2, 0, N_BIAS_TILES - 1)
            lo = jnp.where(n >= 0, (nc % 2) * BLOCK, LANES)
            ok = (sj >= lo) & (sj < lo + BLOCK) & (ti4 == t)
            carry = cached_step(carry, kb[t * (TOPK - 1) + k][0], qs_bf, a, ok)
    ms, ls, accs = new_rows_step(carry, snew_ref, qs_bf)

    qw_bf = _stack_heads(_pad_rows(qw_ref[0], tq)).astype(BF16)
    carry = _flash_init(nl)
    for j in range(WINDOW // LANES):
        dist = (WINDOW - LANES * j) + ti4 - sj
        carry = cached_step(carry, cwin_ref[0, :, j * LANES:(j + 1) * LANES], qw_bf,
                            WINDOW // LANES - j, (dist >= 0) & (dist <= WINDOW))
    mw, lw, accw = new_rows_step(carry, wnew_ref, qw_bf)

    gt2 = gt_ref.at[0]
    out_t = (_gate_row(gt2, 0, g) * oct_ref[0] + _gate_row(gt2, 1, g) * (accs / ls)
             + _gate_row(gt2, 2, g) * (accw / lw))
    o_ref[0] = _untranspose_heads(out_t, tq)[:8]


def _attn_s2(ids, phys, qs8, gates_t, oct, cache_sel_t, sel_new8, cache_win_t, win_new8, bias_tiles, q0, nvalid):
    batch = qs8.shape[0]
    nsel = nvalid * (TOPK - 1)

    def kb_spec(t, k):
        return pl.BlockSpec((1, LANES, PAGE),
                            lambda b, g, ids_r, phys_r, t=t, k=k:
                            (phys_r[((b * NKV + g) * nvalid + t) * TOPK + k], g, 0))

    def bg(shape, col):
        return pl.BlockSpec(shape, lambda b, g, ids_r, phys_r, col=col: (b, 0, col * NKV + g))

    return pl.pallas_call(
        functools.partial(_attn_s2_kernel, q0=q0, nvalid=nvalid),
        grid_spec=pltpu.PrefetchScalarGridSpec(
            num_scalar_prefetch=2,
            grid=(batch, NKV),
            in_specs=[bg((1, 8, REP * HD), 1), bg((1, 8, REP * HD), 2),
                      pl.BlockSpec((1, LANES, TQ), lambda b, g, ids_r, phys_r: (b, 0, 0)),
                      pl.BlockSpec((1, HD, REP * TQ), lambda b, g, ids_r, phys_r: (b * NKV + g, 0, 0))]
                     + [kb_spec(t, k) for t in range(nvalid) for k in range(TOPK - 1)]
                     + [bg((1, 8, LANES), 0),
                        pl.BlockSpec((1, LANES, WINDOW), lambda b, g, ids_r, phys_r: (b, g, 0)),
                        bg((1, 8, LANES), 0),
                        pl.BlockSpec((N_BIAS_TILES, REP, LANES, TQ), lambda b, g, ids_r, phys_r: (0, g, 0, 0))],
            out_specs=bg((1, 8, REP * HD), 0)),
        out_shape=jax.ShapeDtypeStruct((batch, 8, NKV * REP * HD), F32),
        compiler_params=_cparams("arbitrary", "arbitrary"),
        name="attn_sample",
    )(ids, phys, qs8, qs8, gates_t, oct, *([cache_sel_t] * nsel), sel_new8, cache_win_t, win_new8, bias_tiles)


def _pad_cols(w, n):
    return jnp.pad(w, ((0, 0), (0, n - w.shape[1])))


def _prep(p):
    inner = NH_A * DK_A
    width = NKV * REP * HD
    q = {}
    q['w_in'] = _pad_cols(p['w_in_a'][0], 4 * inner + LANES).astype(BF16)
    q['b_if'] = _pad_cols(p['b_if_a'][0][None, :], LANES)
    q['w_out'] = p['w_out_a'][0].astype(BF16)
    q['w_ff1'] = [p['w_ff1'][l].astype(BF16) for l in range(2)]
    q['w_ff2'] = [p['w_ff2'][l].astype(BF16) for l in range(2)]
    q['w_kv'] = p['w_kv'].astype(BF16)
    q['w_q'] = _pad_cols(p['w_q_b'][0], 3 * width + LANES).astype(BF16)
    q['b_gate'] = _pad_cols(p['b_gate_b'][0][None, :], LANES)
    q['w_o'] = p['w_o_b'][0].astype(BF16)
    q['q_gain'] = (jnp.tile(p['q_norm_b'][0][:, None, :], (1, NKV * REP, 1)) * (HD ** -0.5)).reshape(1, 3 * width)
    seg = np.arange(256) // HD
    q['bd'] = jnp.asarray((seg[:, None] == seg[None, :]).astype(np.float32) / HD, dtype=BF16)
    ones = jnp.ones((HD,), F32)
    q['k_gain'] = jnp.concatenate([jnp.tile(jnp.concatenate([p['k_norm'][br], ones]), NKV) for br in (1, 2)])[None, :]
    q['k_mask'] = jnp.tile(jnp.concatenate([ones, 0.0 * ones]), 2 * NKV)[None, :]
    q['k_gain0'] = jnp.concatenate([p['k_norm'][0], ones])[None, :]
    w1 = p['w_cmp1']
    z = jnp.zeros_like(w1[0])
    per_l = jnp.concatenate([jnp.concatenate([w1[0], z], axis=-1), jnp.concatenate([z, w1[1]], axis=-1)], axis=1)
    q['w_cmp1'] = per_l.reshape(BLOCK // 2, 2 * 2 * HD, 2 * w1.shape[-1]).astype(BF16)
    w2 = p['w_cmp2']
    z2 = jnp.zeros_like(w2[0])
    q['w_cmp2'] = jnp.concatenate([jnp.concatenate([w2[0], z2], axis=1), jnp.concatenate([z2, w2[1]], axis=1)],
                                  axis=0).astype(BF16)
    q['pos_rows'] = jnp.tile(p['cmp_pos'].reshape(BLOCK, 2 * HD), (1, NKV))
    return q


def _feature_major(cache):
    n, t = cache.shape[:2]
    return jnp.transpose(cache, (0, 2, 3, 4, 1)).reshape(n, -1, t)


def _mods(ada_rows, rep):
    k = ada_rows.shape[1] // D_MODEL
    out = []
    for j in range(k):
        a = ada_rows[:, j * D_MODEL:(j + 1) * D_MODEL]
        out.append(jnp.repeat(a, rep, axis=0) if rep > 1 else a)
    return out


def _layer0(x, mods, p, q, mlstm_fn, tm, tmf):
    sh1, sc1, g1, sh2, sc2, g2 = mods
    proj = _proj_plain(x, p['norm_mix'][0][None, :], sh1, sc1, q['w_in'], tm)
    y, states = mlstm_fn(proj)
    x = _out_proj(x, g1, y, q['w_out'], tm)
    x = _ffn(x, p['norm_ffn'][0][None, :], sh2, sc2, g2, q['w_ff1'][0], q['w_ff2'][0], tmf)
    return x, states


def _layer1_tail(x, attn, mods, p, q, tm, tmf):
    _, _, g1, sh2, sc2, g2 = mods
    x = _out_proj(x, g1, attn, q['w_o'], tm)
    return _ffn(x, p['norm_ffn'][1][None, :], sh2, sc2, g2, q['w_ff1'][1], q['w_ff2'][1], tmf)


def kernel(x_prompt, x_sample, cache_cmp, cache_sel, cache_win, state_C, state_n, state_m, page_table,
           c_prompt, c_sample, w_ada, b_ada, norm_mix, norm_ffn, w_ff1, w_ff2, w_in_a, b_if_a,
           head_norm_a, w_out_a, w_ada_kv, b_ada_kv, norm_kv, w_kv, k_norm, cmp_pos, w_cmp1, w_cmp2,
           w_q_b, b_gate_b, q_norm_b, w_o_b, rel_bias):
    p = dict(norm_mix=norm_mix, norm_ffn=norm_ffn, w_ff1=w_ff1, w_ff2=w_ff2, w_in_a=w_in_a, b_if_a=b_if_a,
             w_out_a=w_out_a, w_kv=w_kv, k_norm=k_norm, cmp_pos=cmp_pos, w_cmp1=w_cmp1, w_cmp2=w_cmp2,
             w_q_b=w_q_b, b_gate_b=b_gate_b, q_norm_b=q_norm_b, w_o_b=w_o_b)
    q = _prep(p)
    bp, tp, d = x_prompt.shape
    bs, ts, _ = x_sample.shape
    past = page_table.shape[1] * PAGE
    wbuf = cache_win.shape[1]
    assert bp == 1 and ts < BLOCK and ts <= 8 and wbuf == WINDOW and past % LANES == 0
    width = NKV * REP * HD
    kvw = NKV * 2 * HD

    nc = bp + bs
    c_all = jnp.pad(jnp.concatenate([c_prompt, c_sample], axis=0), ((0, -nc % 8), (0, 0)))
    ada = [_ada(c_all, w_ada[l], b_ada[l]) for l in range(2)]
    ada_kv = _ada(c_all, w_ada_kv, b_ada_kv)
    bias_tiles = _bias_tiles(rel_bias)
    head_gain = head_norm_a[0]

    xp = x_prompt.reshape(tp, d)
    zc = jnp.zeros((bp, NH_A, DK_A, DK_A), F32)
    zn = jnp.zeros((bp, NH_A, 1, DK_A), F32)
    zm = jnp.zeros((bp, 8, LANES), F32)

    def mlstm_prompt(proj):
        y, c, n, m = _mlstm(proj.reshape(bp, tp, -1), q['b_if'], head_gain, zc, zn, zm,
                            rows=256, valid=256, chunk=256, zero_init=True)
        return y.reshape(tp, -1), (c, n, m)

    xp, (pc, pn, pm) = _layer0(xp, _mods(ada[0][:bp], 1), p, q, mlstm_prompt, 256, 512)
    sh, sc = _mods(ada_kv[:bp], 1)
    p_cmp, _, _, p_cmp_t, p_sel_t, p_win_t, sel_bf, selt_bf, win_bf, wint_bf = _proj_kv(
        xp, norm_kv[None, :], sh, sc, q['w_kv'], q['bd'], q['k_gain'], q['k_mask'], 256)
    kcvc_p = _compress(p_cmp, q['pos_rows'], q['w_cmp1'], q['w_cmp2'], q['k_gain0'],
                       min(CMP_BLOCKS_PER_STEP, tp // BLOCK))
    mods1 = _mods(ada[1][:bp], 1)
    qs_p, gt_p = _proj_q(xp, norm_mix[1][None, :], mods1[0], mods1[1], q['w_q'], q['bd'], q['q_gain'],
                         q['b_gate'], 256)
    attn_p = _attn_prompt(rel_bias, qs_p, gt_p, kcvc_p, sel_bf, selt_bf, win_bf, wint_bf, bias_tiles)
    y_prompt = _layer1_tail(xp, attn_p, mods1, p, q, 256, 512).reshape(bp, tp, d)

    ms = bs * ts
    xs = x_sample.reshape(ms, d)
    m0 = jnp.broadcast_to(jnp.pad(state_m[0], ((0, 0), (0, 8 - NH_A)))[:, :, None], (bs, 8, LANES))

    def mlstm_sample(proj):
        proj8 = jnp.pad(proj.reshape(bs, ts, -1), ((0, 0), (0, 8 - ts), (0, 0)))
        y, c, n, m = _mlstm(proj8, q['b_if'], head_gain, state_C[0], state_n[0][:, :, None, :], m0,
                            rows=8, valid=ts, chunk=LANES, zero_init=False)
        return y[:, :ts].reshape(ms, -1), (c, n, m)

    xs, (sc_, sn_, sm_) = _layer0(xs, _mods(ada[0][bp:nc], ts), p, q, mlstm_sample, ms, ms)
    sh, sc = _mods(ada_kv[bp:nc], ts)
    s_cmp, s_sel, s_win = _proj_kv(xs, norm_kv[None, :], sh, sc, q['w_kv'], q['bd'], q['k_gain'],
                                   q['k_mask'], ms)[:3]
    kcvc_s = _compress_paged(_feature_major(cache_cmp), page_table, q['pos_rows'],
                             q['w_cmp1'], q['w_cmp2'], q['k_gain0'], min(CMP_BLOCKS_PER_STEP, past // BLOCK))
    mods1 = _mods(ada[1][bp:nc], ts)
    qs_s, gt_s = _proj_q(xs, norm_mix[1][None, :], mods1[0], mods1[1], q['w_q'], q['bd'], q['q_gain'],
                         q['b_gate'], ms)

    def pad8(a):
        return jnp.pad(a.reshape(bs, ts, -1), ((0, 0), (0, 8 - ts), (0, 0)))

    qs8 = pad8(qs_s)
    oct_s, idx_s = _attn_s1(rel_bias, qs8, kcvc_s, past)
    ids = jnp.transpose(idx_s.reshape(bs, NKV, TOPK, TQ)[:, :, :, :ts], (0, 1, 3, 2))
    idc = jnp.maximum(ids, 0)
    pages = jnp.take_along_axis(page_table, (idc // (PAGE // BLOCK)).reshape(bs, -1), axis=1).reshape(ids.shape)
    gt8 = jnp.pad(jnp.transpose(gt_s.reshape(LANES, bs, ts), (1, 0, 2)), ((0, 0), (0, 0), (0, TQ - ts)))
    attn_s = _attn_s2(ids.reshape(-1), pages.reshape(-1), qs8, gt8, oct_s,
                      _feature_major(cache_sel), pad8(s_sel), _feature_major(cache_win), pad8(s_win),
                      bias_tiles, past, ts)
    y_sample = _layer1_tail(xs, attn_s[:, :ts].reshape(ms, width), mods1, p, q, ms, ms).reshape(bs, ts, d)

    rows5 = (NKV, 2, HD)

    def token_major(a_t):
        return jnp.transpose(a_t.reshape(*rows5, a_t.shape[1]), (3, 0, 1, 2))[None]

    s_win_all = jnp.concatenate([cache_win, s_win.reshape(bs, ts, *rows5)], axis=1)
    return (y_prompt, y_sample,
            pc[None], pn.reshape(1, bp, NH_A, DK_A), pm[None, :, :NH_A, 0],
            token_major(p_cmp_t), token_major(p_sel_t), token_major(p_win_t[:, tp - min(WINDOW, tp):]),
            sc_[None], sn_.reshape(1, bs, NH_A, DK_A), sm_[None, :, :NH_A, 0],
            s_cmp.reshape(bs, ts, *rows5), s_sel.reshape(bs, ts, *rows5),
            s_win_all[:, -min(WINDOW, wbuf + ts):])
```

```python
import functools
import math

import numpy as np
import jax
import jax.numpy as jnp
from jax import lax
from jax.experimental import pallas as pl
from jax.experimental.pallas import tpu as pltpu

F32 = jnp.float32
BF16 = jnp.bfloat16
I32 = jnp.int32

D_MODEL = 1024
NH_A = 4
DK_A = 256
NKV = 4
REP = 4
HD = 64
BLOCK = 64
TOPK = 16
WINDOW = 512
N_BUCKETS = 32
REL_MAX_DIST = 2048
PAGE = 128
EPS = 1e-6
NEG = -1e30
LOG2E = math.log2(math.e)

LANES = 128
VMEM_LIMIT = 56 * 1024 * 1024
TQ = 128
TQ_SAMPLE = 32
KT = 256
SUPER_BLOCKS = 16
SUPER = SUPER_BLOCKS * BLOCK
SUPERS_PER_STEP = 2
N_BIAS_TILES = 14
CMP_BLOCKS_PER_STEP = 64
CMP_BIAS_ROWS = 40


def _bucket_thresholds():
    exact = N_BUCKETS // 2
    d = np.arange(1, 4 * REL_MAX_DIST, dtype=np.float64)
    big = exact + np.floor(np.log(d / exact) / math.log(REL_MAX_DIST / exact) * (N_BUCKETS - exact)).astype(np.int64)
    b = np.where(d < exact, d.astype(np.int64), np.minimum(big, N_BUCKETS - 1))
    return [int(d[np.argmax(b >= k)]) for k in range(1, N_BUCKETS)]


BUCKET_THR = _bucket_thresholds()
assert 128 * (N_BIAS_TILES - 1) - (TQ - 1) >= BUCKET_THR[-1]


def _cparams(*sem):
    return pltpu.CompilerParams(dimension_semantics=sem, vmem_limit_bytes=VMEM_LIMIT)


def _dot(a, b):
    return jnp.dot(a.astype(BF16), b.astype(BF16), preferred_element_type=F32)


def _dot_nt(a, b):
    return lax.dot_general(a.astype(BF16), b.astype(BF16), (((1,), (1,)), ((), ())),
                           preferred_element_type=F32)


def _split3(x):
    hi = x.astype(BF16)
    r1 = x - hi.astype(F32)
    mid = r1.astype(BF16)
    lo = (r1 - mid.astype(F32)).astype(BF16)
    return hi, mid, lo


def _sigmoid(x):
    return 1.0 / (1.0 + jnp.exp(-x))


def _norm_mod(x, gain, shift, scale):
    ms = jnp.mean(x * x, axis=-1, keepdims=True)
    y = x * lax.rsqrt(ms + EPS) * gain
    return y * (1.0 + scale) + shift


def _seg_mean_sq(y, bd):
    parts = []
    for j in range(y.shape[1] // 256):
        sq = y[:, j * 256:(j + 1) * 256]
        sq = sq * sq
        hi = sq.astype(BF16)
        lo = (sq - hi.astype(F32)).astype(BF16)
        parts.append(jnp.dot(hi, bd, preferred_element_type=F32) + jnp.dot(lo, bd, preferred_element_type=F32))
    return parts[0] if len(parts) == 1 else jnp.concatenate(parts, axis=1)


def _row_spec(tm, n, per_row):
    if per_row:
        return pl.BlockSpec((tm, n), lambda i: (i, 0))
    return pl.BlockSpec((1, n), lambda i: (0, 0))


def _const_spec(shape):
    return pl.BlockSpec(shape, lambda i: tuple(0 for _ in shape))


def _ada_kernel(c_ref, w_ref, b_ref, o_ref):
    c = c_ref[...]
    o_ref[...] = _dot(c * _sigmoid(c), w_ref[...]) + b_ref[...]


def _ada(c, w, b):
    m, k = c.shape
    n = w.shape[1]
    tn = 1024
    return pl.pallas_call(
        _ada_kernel,
        grid=(n // tn,),
        in_specs=[pl.BlockSpec((m, k), lambda j: (0, 0)),
                  pl.BlockSpec((k, tn), lambda j: (0, j)),
                  pl.BlockSpec((1, tn), lambda j: (0, j))],
        out_specs=pl.BlockSpec((m, tn), lambda j: (0, j)),
        out_shape=jax.ShapeDtypeStruct((m, n), F32),
        compiler_params=_cparams("arbitrary"),
        name="ada",
    )(c, w, b.reshape(1, n))


def _proj_plain_kernel(x_ref, gain_ref, sh_ref, sc_ref, w_ref, o_ref):
    xn = _norm_mod(x_ref[...], gain_ref[...], sh_ref[...], sc_ref[...])
    o_ref[...] = jnp.dot(xn.astype(BF16), w_ref[...], preferred_element_type=F32)


def _proj_plain(x, gain, shift, scale, w_bf, tm):
    m, d = x.shape
    n = w_bf.shape[1]
    per_row = shift.shape[0] != 1
    return pl.pallas_call(
        _proj_plain_kernel,
        grid=(m // tm,),
        in_specs=[pl.BlockSpec((tm, d), lambda i: (i, 0)),
                  _const_spec((1, d)),
                  _row_spec(tm, d, per_row), _row_spec(tm, d, per_row),
                  _const_spec((d, n))],
        out_specs=pl.BlockSpec((tm, n), lambda i: (i, 0)),
        out_shape=jax.ShapeDtypeStruct((m, n), F32),
        compiler_params=_cparams("arbitrary"),
        name="proj_mlstm",
    )(x, gain, shift, scale, w_bf)


def _proj_q_kernel(x_ref, gain_ref, sh_ref, sc_ref, w_ref, bd_ref, qg_ref, bg_ref, q_ref, gt_ref):
    xn = _norm_mod(x_ref[...], gain_ref[...], sh_ref[...], sc_ref[...])
    y = jnp.dot(xn.astype(BF16), w_ref[...], preferred_element_type=F32)
    nq = q_ref.shape[1]
    yq = y[:, :nq]
    ms = _seg_mean_sq(yq, bd_ref[...])
    q_ref[...] = yq * lax.rsqrt(ms + EPS) * qg_ref[...]
    gates = _sigmoid(y[:, nq:] + bg_ref[...])
    gt_ref[...] = gates.T


def _proj_q(x, gain, shift, scale, w_bf, bd, qgain_row, bgate_row, tm):
    m, d = x.shape
    n = w_bf.shape[1]
    nq = n - LANES
    per_row = shift.shape[0] != 1
    return pl.pallas_call(
        _proj_q_kernel,
        grid=(m // tm,),
        in_specs=[pl.BlockSpec((tm, d), lambda i: (i, 0)),
                  _const_spec((1, d)),
                  _row_spec(tm, d, per_row), _row_spec(tm, d, per_row),
                  _const_spec((d, n)), _const_spec((256, 256)),
                  _const_spec((1, nq)), _const_spec((1, LANES))],
        out_specs=[pl.BlockSpec((tm, nq), lambda i: (i, 0)),
                   pl.BlockSpec((LANES, tm), lambda i: (0, i))],
        out_shape=[jax.ShapeDtypeStruct((m, nq), F32),
                   jax.ShapeDtypeStruct((LANES, m), F32)],
        compiler_params=_cparams("arbitrary"),
        name="proj_q",
    )(x, gain, shift, scale, w_bf, bd, qgain_row, bgate_row)


def _proj_kv_kernel(x_ref, gain_ref, sh_ref, sc_ref, w_ref, bd_ref, kg_ref, km_ref,
                    cmp_ref, sel_ref, win_ref, cmpt_ref, selt_ref, wint_ref,
                    selb_ref, seltb_ref, winb_ref, wintb_ref):
    xn = _norm_mod(x_ref[...], gain_ref[...], sh_ref[...], sc_ref[...])
    y = jnp.dot(xn.astype(BF16), w_ref[...], preferred_element_type=F32)
    w = cmp_ref.shape[1]
    cmp = y[:, :w]
    ykn = y[:, w:]
    ms = _seg_mean_sq(ykn, bd_ref[...])
    ykn = jnp.where(km_ref[...] > 0.5, ykn * lax.rsqrt(ms + EPS) * kg_ref[...], ykn)
    sel = ykn[:, :w]
    win = ykn[:, w:]
    sel_t = sel.T
    win_t = win.T
    cmp_ref[...] = cmp
    sel_ref[...] = sel
    win_ref[...] = win
    cmpt_ref[...] = cmp.T
    selt_ref[...] = sel_t
    wint_ref[...] = win_t
    tm = sel.shape[0]
    tok = pl.program_id(0) * tm + lax.broadcasted_iota(I32, sel.shape, 0)
    lane = lax.broadcasted_iota(I32, sel.shape, 1) % LANES
    onehot = jnp.where(lane - HD == (tok // BLOCK) % SUPER_BLOCKS, 1.0, 0.0)
    selb_ref[...] = jnp.where(lane < HD, sel, onehot).astype(BF16)
    winb_ref[...] = win.astype(BF16)
    seltb_ref[...] = sel_t.astype(BF16)
    wintb_ref[...] = win_t.astype(BF16)


def _proj_kv(x, gain, shift, scale, w_bf, bd, kgain_row, kmask_row, tm):
    m, d = x.shape
    n = w_bf.shape[1]
    w = n // 3
    per_row = shift.shape[0] != 1
    row = pl.BlockSpec((tm, w), lambda i: (i, 0))
    col = pl.BlockSpec((w, tm), lambda i: (0, i))
    return pl.pallas_call(
        _proj_kv_kernel,
        grid=(m // tm,),
        in_specs=[pl.BlockSpec((tm, d), lambda i: (i, 0)),
                  _const_spec((1, d)),
                  _row_spec(tm, d, per_row), _row_spec(tm, d, per_row),
                  _const_spec((d, n)), _const_spec((256, 256)),
                  _const_spec((1, 2 * w)), _const_spec((1, 2 * w))],
        out_specs=[row, row, row, col, col, col, row, col, row, col],
        out_shape=[jax.ShapeDtypeStruct((m, w), F32)] * 3 + [jax.ShapeDtypeStruct((w, m), F32)] * 3
                  + [jax.ShapeDtypeStruct((m, w), BF16), jax.ShapeDtypeStruct((w, m), BF16),
                     jax.ShapeDtypeStruct((m, w), BF16), jax.ShapeDtypeStruct((w, m), BF16)],
        compiler_params=_cparams("arbitrary"),
        name="proj_kv",
    )(x, gain, shift, scale, w_bf, bd, kgain_row, kmask_row)


def _out_proj_kernel(x_ref, g_ref, y_ref, w_ref, o_ref):
    o_ref[...] = x_ref[...] + g_ref[...] * jnp.dot(y_ref[...].astype(BF16), w_ref[...],
                                                   preferred_element_type=F32)


def _out_proj(x, g, y, w_bf, tm):
    m, d = x.shape
    k = y.shape[1]
    per_row = g.shape[0] != 1
    return pl.pallas_call(
        _out_proj_kernel,
        grid=(m // tm,),
        in_specs=[pl.BlockSpec((tm, d), lambda i: (i, 0)),
                  _row_spec(tm, d, per_row),
                  pl.BlockSpec((tm, k), lambda i: (i, 0)),
                  _const_spec((k, d))],
        out_specs=pl.BlockSpec((tm, d), lambda i: (i, 0)),
        out_shape=jax.ShapeDtypeStruct((m, d), F32),
        compiler_params=_cparams("arbitrary"),
        name="out_proj",
    )(x, g, y, w_bf)


def _ffn_kernel(x_ref, gain_ref, sh_ref, sc_ref, g_ref, w1_ref, w2_ref, o_ref, *, fc):
    x = x_ref[...]
    xn = _norm_mod(x, gain_ref[...], sh_ref[...], sc_ref[...]).astype(BF16)
    acc = jnp.zeros(x.shape, F32)
    for c in range(w1_ref.shape[1] // fc):
        h = jnp.dot(xn, w1_ref[:, c * fc:(c + 1) * fc], preferred_element_type=F32)
        h = jnp.maximum(h, 0.0)
        acc = acc + jnp.dot((h * h).astype(BF16), w2_ref[c * fc:(c + 1) * fc, :],
                            preferred_element_type=F32)
    o_ref[...] = x + g_ref[...] * acc


def _ffn(x, gain, shift, scale, g, w1_bf, w2_bf, tm):
    m, d = x.shape
    f = w1_bf.shape[1]
    per_row = shift.shape[0] != 1
    return pl.pallas_call(
        functools.partial(_ffn_kernel, fc=1024),
        grid=(m // tm,),
        in_specs=[pl.BlockSpec((tm, d), lambda i: (i, 0)),
                  _const_spec((1, d)),
                  _row_spec(tm, d, per_row), _row_spec(tm, d, per_row), _row_spec(tm, d, per_row),
                  _const_spec((d, f)), _const_spec((f, d))],
        out_specs=pl.BlockSpec((tm, d), lambda i: (i, 0)),
        out_shape=jax.ShapeDtypeStruct((m, d), F32),
        compiler_params=_cparams("arbitrary"),
        name="ffn",
    )(x, gain, shift, scale, g, w1_bf, w2_bf)


def _mlstm_kernel(q_ref, k_ref, v_ref, o_ref, gt_ref, bif_ref, hg_ref, c0_ref, n0_ref, m0_ref,
                  y_ref, cout_ref, nout_ref, mout_ref, c_scr, n_scr, m_scr, *, rows, valid, chunk, zero_init):
    ci = pl.program_id(1)
    nci = pl.num_programs(1)

    @pl.when(ci == 0)
    def _():
        if zero_init:
            c_scr[...] = jnp.zeros(c_scr.shape, F32)
            n_scr[...] = jnp.zeros(n_scr.shape, F32)
            m_scr[...] = jnp.zeros(m_scr.shape, F32)
        else:
            c_scr[...] = c0_ref[0]
            n_scr[...] = n0_ref[0]
            m_scr[...] = m0_ref[0]

    L = chunk

    def padded(ref_val, fill):
        if rows == L:
            return ref_val
        pad = jnp.full((L - rows, ref_val.shape[1]), fill, ref_val.dtype)
        return jnp.concatenate([ref_val, pad], axis=0)

    q = padded(q_ref[0], 0.0)
    k = padded(k_ref[0], 0.0)
    v = padded(v_ref[0], 0.0)
    og = padded(o_ref[0], 0.0)
    g = padded(gt_ref[0], 0.0) + bif_ref[...]
    lf = -(jnp.maximum(-g, 0.0) + jnp.log1p(jnp.exp(-jnp.abs(g))))
    li = g
    if valid != L:
        is_real = lax.broadcasted_iota(I32, (L, LANES), 0) < valid
        lf = jnp.where(is_real, lf, 0.0)
        li = jnp.where(is_real, li, NEG)
    r_io = lax.broadcasted_iota(I32, (L, L), 0)
    c_io = lax.broadcasted_iota(I32, (L, L), 1)
    causal = c_io <= r_io
    tril = jnp.where(causal, 1.0, 0.0).astype(BF16)
    hi, mid, lo = _split3(lf)
    b = (jnp.dot(tril, hi, preferred_element_type=F32) + jnp.dot(tril, mid, preferred_element_type=F32)
         + jnp.dot(tril, lo, preferred_element_type=F32))
    lane = lax.broadcasted_iota(I32, (L, LANES), 1)
    mixed_t = jnp.where(lane < NH_A, li, b).T
    for h in range(NH_A):
        sl = slice(h * DK_A, (h + 1) * DK_A)
        qh = q[:, sl] * (DK_A ** -0.5)
        kh = k[:, sl]
        vh = v[:, sl]
        b_col = b[:, NH_A + h:NH_A + h + 1]
        li_col = li[:, h:h + 1]
        b_row = mixed_t[NH_A + h:NH_A + h + 1, :]
        li_row = mixed_t[h:h + 1, :]
        m_prev = m_scr[h:h + 1, 0:1]
        c_prev = c_scr[h]
        n_prev = n_scr[h]
        dmat = jnp.where(causal, b_col - b_row + li_row, NEG)
        m_inter = b_col + m_prev
        m_t = jnp.maximum(m_inter, jnp.max(dmat, axis=-1, keepdims=True))
        s = _dot_nt(qh, kh) * jnp.exp(dmat - m_t)
        a_inter = jnp.exp(m_inter - m_t)
        num = _dot(s, vh) + a_inter * _dot(qh, c_prev)
        den = jnp.sum(s, axis=-1, keepdims=True) + a_inter * jnp.sum(qh * n_prev, axis=-1, keepdims=True)
        hh = num / jnp.maximum(jnp.abs(den), jnp.exp(-m_t))
        b_last = b_col[L - 1:L, :]
        g_col = b_last - b_col + li_col
        m_new = jnp.maximum(b_last + m_prev, jnp.max(g_col, axis=0, keepdims=True))
        w_col = jnp.exp(g_col - m_new)
        decay = jnp.exp(b_last + m_prev - m_new)
        kw = kh * w_col
        c_scr[h] = decay * c_prev + _dot(kw.T, vh)
        n_scr[h] = decay * n_prev + jnp.sum(kw, axis=0, keepdims=True)
        m_scr[h:h + 1, :] = jnp.broadcast_to(m_new, (1, LANES))
        hn = hh * lax.rsqrt(jnp.mean(hh * hh, axis=-1, keepdims=True) + EPS) * hg_ref[h:h + 1, :]
        yh = hn * _sigmoid(og[:, sl])
        y_ref[0, :, sl] = yh[:rows]

    @pl.when(ci == nci - 1)
    def _():
        cout_ref[0] = c_scr[...]
        nout_ref[0] = n_scr[...]
        mout_ref[0] = m_scr[...]


def _mlstm(proj, b_if_row, head_gain, c0, n0, m0, *, rows, valid, chunk, zero_init):
    batch, t, _ = proj.shape
    nch = t // rows
    inner = NH_A * DK_A

    def colblk(j, width):
        return pl.BlockSpec((1, rows, width), lambda b, c, j=j: (b, c, j))

    state_c = pl.BlockSpec((1, NH_A, DK_A, DK_A), lambda b, c: (b, 0, 0, 0))
    state_n = pl.BlockSpec((1, NH_A, 1, DK_A), lambda b, c: (b, 0, 0, 0))
    state_m = pl.BlockSpec((1, 8, LANES), lambda b, c: (b, 0, 0))
    return pl.pallas_call(
        functools.partial(_mlstm_kernel, rows=rows, valid=valid, chunk=chunk, zero_init=zero_init),
        grid=(batch, nch),
        in_specs=[colblk(0, inner), colblk(1, inner), colblk(2, inner), colblk(3, inner),
                  colblk(4 * inner // LANES, LANES),
                  pl.BlockSpec((1, LANES), lambda b, c: (0, 0)),
                  pl.BlockSpec((NH_A, DK_A), lambda b, c: (0, 0)),
                  state_c, state_n, state_m],
        out_specs=[pl.BlockSpec((1, rows, inner), lambda b, c: (b, c, 0)), state_c, state_n, state_m],
        out_shape=[jax.ShapeDtypeStruct((batch, t, inner), F32),
                   jax.ShapeDtypeStruct((batch, NH_A, DK_A, DK_A), F32),
                   jax.ShapeDtypeStruct((batch, NH_A, 1, DK_A), F32),
                   jax.ShapeDtypeStruct((batch, 8, LANES), F32)],
        scratch_shapes=[pltpu.VMEM((NH_A, DK_A, DK_A), F32), pltpu.VMEM((NH_A, 1, DK_A), F32),
                        pltpu.VMEM((8, LANES), F32)],
        compiler_params=_cparams("arbitrary", "arbitrary"),
        name="mlstm",
    )(proj, proj, proj, proj, proj, b_if_row, head_gain, c0, n0, m0)


def _compress_body(r_refs, pos_ref, w1_ref, w2_ref, kg_ref, o_ref, nb):
    acc = None
    for lp in range(BLOCK // 2):
        pieces = []
        for g in range(NKV):
            xa = r_refs[g][pl.ds(2 * lp, nb, stride=BLOCK), :]
            xb = r_refs[g][pl.ds(2 * lp + 1, nb, stride=BLOCK), :]
            pieces.append(jnp.concatenate([xa, xb], axis=1))
        pa = pos_ref[2 * lp:2 * lp + 1, 0:LANES]
        pb = pos_ref[2 * lp + 1:2 * lp + 2, 0:LANES]
        pieces.append(jnp.broadcast_to(jnp.concatenate([pa, pb], axis=1), (8, 2 * LANES)))
        x = jnp.concatenate(pieces, axis=0).astype(BF16)
        d = jnp.dot(x, w1_ref[lp], preferred_element_type=F32)
        acc = d if acc is None else acc + d
    pre = acc[:NKV * nb] + acc[NKV * nb:NKV * nb + 1]
    hid = pre * _sigmoid(pre)
    out = jnp.dot(hid.astype(BF16), w2_ref[...], preferred_element_type=F32)
    is_k = lax.broadcasted_iota(I32, out.shape, 1) < HD
    ms = jnp.sum(jnp.where(is_k, out * out, 0.0), axis=-1, keepdims=True) * (1.0 / HD)
    out = jnp.where(is_k, out * lax.rsqrt(ms + EPS) * kg_ref[...], out)
    for g in range(NKV):
        o_ref[:, g * LANES:(g + 1) * LANES] = out[g * nb:(g + 1) * nb]


def _compress_kernel(r0, r1, r2, r3, pos_ref, w1_ref, w2_ref, kg_ref, o_ref, *, nb):
    _compress_body((r0, r1, r2, r3), pos_ref, w1_ref, w2_ref, kg_ref, o_ref, nb)


def _compress_paged_kernel(pt_ref, *refs, nb):
    npages = nb * BLOCK // PAGE
    pages = refs[:npages]
    pos_ref, w1_ref, w2_ref, kg_ref, o_ref = refs[npages:npages + 5]
    r_scrs = refs[npages + 5:]
    for p in range(npages):
        for g in range(NKV):
            r_scrs[g][p * PAGE:(p + 1) * PAGE, :] = pages[p][0, g * LANES:(g + 1) * LANES, :].T
    _compress_body(r_scrs, pos_ref, w1_ref, w2_ref, kg_ref, o_ref, nb)


def _compress_paged(cache_t, page_table, pos_rows, w1p, w2p, kgain_row, nb):
    batch, npg = page_table.shape
    w = cache_t.shape[1]
    npages = nb * BLOCK // PAGE
    steps = npg // npages

    def page_spec(p):
        return pl.BlockSpec((1, w, PAGE), lambda b, j, pt, p=p: (pt[b, j * npages + p], 0, 0))

    def const(shape):
        return pl.BlockSpec(shape, lambda b, j, pt: tuple(0 for _ in shape))

    return pl.pallas_call(
        functools.partial(_compress_paged_kernel, nb=nb),
        grid_spec=pltpu.PrefetchScalarGridSpec(
            num_scalar_prefetch=1,
            grid=(batch, steps),
            in_specs=[page_spec(p) for p in range(npages)]
                     + [const((BLOCK, w)), const(w1p.shape), const(w2p.shape), const((1, LANES))],
            out_specs=pl.BlockSpec((nb, w), lambda b, j, pt: (b * steps + j, 0)),
            scratch_shapes=[pltpu.VMEM((nb * BLOCK, LANES), F32) for _ in range(NKV)]),
        out_shape=jax.ShapeDtypeStruct((batch * npg * PAGE // BLOCK, w), F32),
        compiler_params=_cparams("arbitrary", "arbitrary"),
        name="compress_paged",
    )(page_table, *([cache_t] * npages), pos_rows, w1p, w2p, kgain_row)


def _compress(rows, pos_rows, w1p, w2p, kgain_row, nb):
    t, w = rows.shape
    nblk = t // BLOCK
    return pl.pallas_call(
        functools.partial(_compress_kernel, nb=nb),
        grid=(nblk // nb,),
        in_specs=[pl.BlockSpec((nb * BLOCK, LANES), lambda i, g=g: (i, g)) for g in range(NKV)]
                 + [_const_spec((BLOCK, w)), _const_spec(w1p.shape), _const_spec(w2p.shape),
                    _const_spec((1, LANES))],
        out_specs=pl.BlockSpec((nb, w), lambda i: (i, 0)),
        out_shape=jax.ShapeDtypeStruct((nblk, w), F32),
        compiler_params=_cparams("arbitrary"),
        name="compress",
    )(rows, rows, rows, rows, pos_rows, w1p, w2p, kgain_row)


def _bias_from_dist(dist, tab_ref, h):
    out = jnp.full(dist.shape, tab_ref[0, h], F32)
    for kk in range(1, N_BUCKETS):
        out = jnp.where(dist >= BUCKET_THR[kk - 1], tab_ref[kk, h], out)
    return out


def _bias_tiles_kernel(tab_ref, o_ref):
    a = pl.program_id(0)
    h = pl.program_id(1)
    sj = lax.broadcasted_iota(I32, (LANES, TQ), 0)
    ti = lax.broadcasted_iota(I32, (LANES, TQ), 1)
    o_ref[0, 0] = _bias_from_dist(a * LANES + ti - sj, tab_ref, h) * LOG2E


def _bias_tiles(rel_bias):
    nh = rel_bias.shape[1]
    return pl.pallas_call(
        _bias_tiles_kernel,
        grid=(N_BIAS_TILES, nh),
        in_specs=[pl.BlockSpec(memory_space=pltpu.SMEM)],
        out_specs=pl.BlockSpec((1, 1, LANES, TQ), lambda a, h: (a, h, 0, 0)),
        out_shape=jax.ShapeDtypeStruct((N_BIAS_TILES, nh, LANES, TQ), F32),
        compiler_params=_cparams("arbitrary", "arbitrary"),
        name="bias_tiles",
    )(rel_bias)


def _stack_heads(q):
    z = jnp.zeros((q.shape[0], HD), q.dtype)
    return jnp.concatenate([jnp.concatenate([q[:, r * HD:(r + 1) * HD], z], axis=1) for r in range(REP)],
                           axis=0)


def _tile_heads(x):
    return jnp.concatenate([x] * REP, axis=1)


def _pad_rows(x, n):
    if x.shape[0] == n:
        return x
    return jnp.concatenate([x, jnp.zeros((n - x.shape[0],) + x.shape[1:], x.dtype)], axis=0)


def _bias4(dist, tab_ref, g):
    biases = [jnp.full(dist.shape, tab_ref[0, g * REP + r], F32) for r in range(REP)]
    for kk in range(1, N_BUCKETS):
        reached = dist >= BUCKET_THR[kk - 1]
        biases = [jnp.where(reached, tab_ref[kk, g * REP + r], biases[r]) for r in range(REP)]
    return biases


def _cmp_branch(qc, kcvc_ref, tab_ref, g, pos, pos0, x_scr):
    tq = qc.shape[0]
    nb = kcvc_ref.shape[0]
    kcvc = kcvc_ref[...]
    q4 = _stack_heads(qc)
    lt = _dot_nt(kcvc, q4)
    n_io = lax.broadcasted_iota(I32, (nb, tq), 0)
    dist = pos - (n_io * BLOCK + (BLOCK - 1))
    vis = dist >= 0
    if nb <= CMP_BIAS_ROWS:
        biased = [lt[:, r * tq:(r + 1) * tq] + b for r, b in enumerate(_bias4(dist, tab_ref, g))]
    else:
        far = [tab_ref[N_BUCKETS - 1, g * REP + r] for r in range(REP)]
        x_scr[...] = jnp.concatenate([lt[:, r * tq:(r + 1) * tq] + far[r] for r in range(REP)], axis=1)
        first = (pos0 - (BUCKET_THR[-1] + BLOCK - 1)) // BLOCK + 1
        w0 = pl.multiple_of(jnp.clip(first // 8 * 8, 0, nb - CMP_BIAS_ROWS), 8)
        w_io = lax.broadcasted_iota(I32, (CMP_BIAS_ROWS, tq), 0) + w0
        near = _bias4(pos - (w_io * BLOCK + (BLOCK - 1)), tab_ref, g)
        lt_w = _dot_nt(kcvc_ref[pl.ds(w0, CMP_BIAS_ROWS), :], q4)
        x_scr[pl.ds(w0, CMP_BIAS_ROWS), :] = jnp.concatenate(
            [lt_w[:, r * tq:(r + 1) * tq] + near[r] for r in range(REP)], axis=1)
        biased = [x_scr[:, r * tq:(r + 1) * tq] for r in range(REP)]
    probs = []
    for r in range(REP):
        x = jnp.where(vis, biased[r], NEG)
        e = jnp.exp(x - jnp.max(x, axis=0, keepdims=True))
        p = e / jnp.sum(e, axis=0, keepdims=True)
        probs.append(jnp.where(vis, p, 0.0))
    score = probs[0] + probs[1] + probs[2] + probs[3]
    oc_t = _dot(kcvc.T[HD:, :], jnp.concatenate(probs, axis=1))
    cand = n_io < pos // BLOCK
    return oc_t, jnp.where(cand, score, -jnp.inf)


def _topk_rows(score, n_top, pick_fn):
    nb, tq = score.shape
    n_f = lax.broadcasted_iota(I32, (nb, tq), 0).astype(F32)
    s = score
    for it in range(n_top):
        mx = jnp.max(s, axis=0, keepdims=True)
        idx = jnp.min(jnp.where(s == mx, n_f, float(nb)), axis=0, keepdims=True)
        ok = mx > -jnp.inf
        hit = n_f == idx
        pick_fn(it, idx, ok, hit)
        s = jnp.where(hit, -jnp.inf, s)


def _flash_init(nl):
    return (jnp.full((1, nl), NEG, F32), jnp.zeros((1, nl), F32), jnp.zeros((HD, nl), F32))


def _softmax_part(x2, vt):
    m = jnp.max(x2, axis=0, keepdims=True)
    p = jnp.exp2(x2 - m)
    return m, jnp.sum(p, axis=0, keepdims=True), jnp.dot(vt, p.astype(BF16), preferred_element_type=F32)


def _merge_parts(a, b):
    m = jnp.maximum(a[0], b[0])
    fa = jnp.exp2(a[0] - m)
    fb = jnp.exp2(b[0] - m)
    return m, fa * a[1] + fb * b[1], fa * a[2] + fb * b[2]


def _untranspose_heads(o_t, tq):
    halves = []
    for p in range(REP // 2):
        pair = jnp.concatenate([o_t[:, (2 * p) * tq:(2 * p + 1) * tq],
                                o_t[:, (2 * p + 1) * tq:(2 * p + 2) * tq]], axis=0)
        halves.append(pair.T)
    return jnp.concatenate(halves, axis=1)


def _gate_row(gt_ref, br, g):
    return jnp.concatenate([gt_ref[pl.ds(br * NKV * REP + g * REP + r, 1), :] for r in range(REP)], axis=1)


def _attn_prompt_kernel(tab_ref, qc_ref, qs_ref, qw_ref, gt_ref, kcvc_ref, ksel_ref, vselt_ref,
                        kw0, kw1, kw2, kw3, kw4, vw0, vw1, vw2, vw3, vw4, bt_ref, o_ref, sel_scr, xc_scr):
    g = pl.program_id(0)
    i = pl.program_id(1)
    tq = TQ
    nl = REP * tq
    t0 = i * tq
    pos = t0 + lax.broadcasted_iota(I32, (1, tq), 1)

    oc_t, score = _cmp_branch(qc_ref[...], kcvc_ref, tab_ref, g, pos, t0, xc_scr)
    nb = score.shape[0]
    sel_scr[...] = jnp.full((nb, tq), NEG, F32)

    def pick(it, idx, ok, hit):
        sel_scr[...] = jnp.where(hit, jnp.where(ok, 0.0, sel_scr[...]), sel_scr[...])

    _topk_rows(score, TOPK - 1, pick)

    def bias_tile(a):
        return jnp.concatenate([bt_ref[a, r] for r in range(REP)], axis=1)

    qs_t = (_stack_heads(qs_ref[...]) * LOG2E).T[:HD].astype(BF16)
    pad_rows = jnp.zeros((LANES - HD - SUPER_BLOCKS, nl), BF16)

    def sel_body(step, carry):
        logits = []
        for u in range(SUPERS_PER_STEP):
            jj = step * SUPERS_PER_STEP + u
            mrows = _tile_heads(sel_scr[pl.ds(pl.multiple_of(jj * SUPER_BLOCKS, SUPER_BLOCKS), SUPER_BLOCKS), :])
            q_aug = jnp.concatenate([qs_t, mrows.astype(BF16), pad_rows], axis=0)
            s0 = pl.multiple_of(jj * SUPER, SUPER)
            logits.append((jj, s0, jnp.dot(ksel_ref[pl.ds(s0, SUPER), :], q_aug, preferred_element_type=F32)))
        parts = []
        for jj, s0, x_all in logits:
            for c in range(SUPER // KT):
                tile0 = jj * (SUPER // LANES) + c * (KT // LANES)
                x = x_all[c * KT:(c + 1) * KT] + jnp.concatenate(
                    [bias_tile(jnp.clip(i - tile0 - a, 0, N_BIAS_TILES - 1)) for a in range(KT // LANES)], axis=0)
                parts.append(_softmax_part(x, vselt_ref[HD:2 * HD, pl.ds(pl.multiple_of(s0 + c * KT, KT), KT)]))
        for part in parts:
            carry = _merge_parts(carry, part)
        return carry

    n_steps = (2 * i) // (SUPER_BLOCKS * SUPERS_PER_STEP) + 1
    carry = lax.fori_loop(0, n_steps, sel_body, _flash_init(nl))

    sj = lax.broadcasted_iota(I32, (LANES, nl), 0)
    ti4 = _tile_heads(lax.broadcasted_iota(I32, (1, tq), 1))
    t0a = pl.multiple_of(t0, LANES)
    q_diag = jnp.concatenate([qs_t, jnp.zeros((LANES - HD, nl), BF16)], axis=0)
    qw_bf = (_stack_heads(qw_ref[...]) * LOG2E).astype(BF16)
    kw_refs = (kw0, kw1, kw2, kw3, kw4)
    vw_refs = (vw0, vw1, vw2, vw3, vw4)
    x_diag = jnp.dot(ksel_ref[pl.ds(t0a, LANES), :], q_diag, preferred_element_type=F32)
    x_win = [_dot_nt(kw_refs[j][...], qw_bf) for j in range(5)]
    x = jnp.where((sj // BLOCK == ti4 // BLOCK) & (sj <= ti4), x_diag + bias_tile(0), NEG)
    ms, ls, accs = _merge_parts(carry, _softmax_part(x, vselt_ref[HD:2 * HD, pl.ds(t0a, LANES)]))

    parts = []
    for j in range(5):
        dist = LANES * (4 - j) + ti4 - sj
        ok = jnp.where((dist >= 0) & (dist <= WINDOW), 1.0, 0.0) * jnp.where(i - 4 + j >= 0, 1.0, 0.0)
        x = jnp.where(ok > 0.5, x_win[j] + bias_tile(4 - j), NEG)
        parts.append(_softmax_part(x, vw_refs[j][HD:2 * HD, :]))
    carry = parts[4]
    for part in parts[:4]:
        carry = _merge_parts(carry, part)
    mw, lw, accw = carry

    out_t = (_gate_row(gt_ref, 0, g) * oc_t + _gate_row(gt_ref, 1, g) * (accs / ls)
             + _gate_row(gt_ref, 2, g) * (accw / lw))
    o_ref[...] = _untranspose_heads(out_t, tq)


def _attn_prompt(rel_bias, qs, gates_t, kcvc, sel_bf, selt_bf, win_bf, wint_bf, bias_tiles):
    t = qs.shape[0]
    nb = kcvc.shape[0]
    width = NKV * REP * HD

    def qspec(br):
        return pl.BlockSpec((TQ, REP * HD), lambda g, i, br=br: (i, br * NKV + g))

    def kw_spec(j):
        return pl.BlockSpec((LANES, LANES), lambda g, i, j=j: (jnp.maximum(i - 4 + j, 0), g))

    def vw_spec(j):
        return pl.BlockSpec((LANES, LANES), lambda g, i, j=j: (g, jnp.maximum(i - 4 + j, 0)))

    return pl.pallas_call(
        _attn_prompt_kernel,
        grid=(NKV, t // TQ),
        in_specs=[pl.BlockSpec(memory_space=pltpu.SMEM),
                  qspec(0), qspec(1), qspec(2),
                  pl.BlockSpec((LANES, TQ), lambda g, i: (0, i)),
                  pl.BlockSpec((nb, LANES), lambda g, i: (0, g)),
                  pl.BlockSpec((t, LANES), lambda g, i: (0, g)),
                  pl.BlockSpec((LANES, t), lambda g, i: (g, 0))]
                 + [kw_spec(j) for j in range(5)] + [vw_spec(j) for j in range(5)]
                 + [pl.BlockSpec((N_BIAS_TILES, REP, LANES, TQ), lambda g, i: (0, g, 0, 0))],
        out_specs=pl.BlockSpec((TQ, REP * HD), lambda g, i: (i, g)),
        out_shape=jax.ShapeDtypeStruct((t, width), F32),
        scratch_shapes=[pltpu.VMEM((nb, TQ), F32), pltpu.VMEM((nb, REP * TQ), F32)],
        compiler_params=_cparams("arbitrary", "arbitrary"),
        name="attn_prompt",
    )(rel_bias, qs, qs, qs, gates_t, kcvc, sel_bf, selt_bf,
      win_bf, win_bf, win_bf, win_bf, win_bf, wint_bf, wint_bf, wint_bf, wint_bf, wint_bf, bias_tiles)


def _attn_s1_kernel(tab_ref, qc_ref, kcvc_ref, oct_ref, idx_ref, xc_scr, *, q0):
    g = pl.program_id(1)
    tq = TQ
    pos = q0 + lax.broadcasted_iota(I32, (1, tq), 1)
    oc_t, score = _cmp_branch(_pad_rows(qc_ref[0], tq), kcvc_ref, tab_ref, g, pos, q0, xc_scr)
    oct_ref[0] = oc_t

    def pick(it, idx, ok, hit):
        idx_ref[0, it:it + 1, :] = jnp.where(ok, idx, -1.0).astype(I32)

    _topk_rows(score, TOPK - 1, pick)
    idx_ref[0, TOPK - 1:TOPK, :] = jnp.full((1, tq), -1, I32)


def _attn_s1(rel_bias, qs8, kcvc, q0):
    batch = qs8.shape[0]
    nb = kcvc.shape[0] // batch
    return pl.pallas_call(
        functools.partial(_attn_s1_kernel, q0=q0),
        grid=(batch, NKV),
        in_specs=[pl.BlockSpec(memory_space=pltpu.SMEM),
                  pl.BlockSpec((1, 8, REP * HD), lambda b, g: (b, 0, g)),
                  pl.BlockSpec((nb, LANES), lambda b, g: (b, g))],
        out_specs=[pl.BlockSpec((1, HD, REP * TQ), lambda b, g: (b * NKV + g, 0, 0)),
                   pl.BlockSpec((1, TOPK, TQ), lambda b, g: (b * NKV + g, 0, 0))],
        out_shape=[jax.ShapeDtypeStruct((batch * NKV, HD, REP * TQ), F32),
                   jax.ShapeDtypeStruct((batch * NKV, TOPK, TQ), I32)],
        scratch_shapes=[pltpu.VMEM((nb, REP * TQ), F32)],
        compiler_params=_cparams("arbitrary", "arbitrary"),
        name="attn_sample_select",
    )(rel_bias, qs8, kcvc)


def _attn_s2_kernel(ids_ref, phys_ref, qs_ref, qw_ref, gt_ref, oct_ref, *refs, q0, nvalid):
    nsel = nvalid * (TOPK - 1)
    kb = refs[:nsel]
    snew_ref, cwin_ref, wnew_ref, bt_ref, o_ref = refs[nsel:]
    b = pl.program_id(0)
    g = pl.program_id(1)
    tq = TQ_SAMPLE
    nl = REP * tq
    ti4 = _tile_heads(lax.broadcasted_iota(I32, (1, tq), 1))
    sj = lax.broadcasted_iota(I32, (LANES, nl), 0)
    base = (b * NKV + g) * nvalid * TOPK

    def compact(x):
        return jnp.concatenate([x[:, r * TQ:r * TQ + tq] for r in range(REP)], axis=1)

    def bias_tile(a):
        return jnp.concatenate([bt_ref[a, r][:, :tq] for r in range(REP)], axis=1)

    def attend(q_bf, cached, new_ref):
        xs, vts = [], []
        for tile_t, a, ok in cached:
            xs.append(jnp.where(ok, _dot_nt(tile_t.T, q_bf) + bias_tile(a), NEG))
            vts.append(tile_t[HD:, :].astype(BF16))
        blk = _pad_rows(new_ref[0], LANES)
        xs.append(jnp.where((sj <= ti4) & (sj < nvalid), _dot_nt(blk, q_bf) + bias_tile(0), NEG))
        vts.append(blk.T[HD:, :].astype(BF16))
        _, l, acc = _softmax_part(jnp.concatenate(xs, axis=0), jnp.concatenate(vts, axis=1))
        return acc / l

    cached = []
    for t in range(nvalid):
        for k in range(TOPK - 1):
            n = ids_ref[base + t * TOPK + k]
            nc = jnp.maximum(n, 0)
            a = jnp.clip(q0 // LANES - nc // 2, 0, N_BIAS_TILES - 1)
            lo = jnp.where(n >= 0, (nc % 2) * BLOCK, LANES)
            cached.append((kb[t * (TOPK - 1) + k][0], a, (sj >= lo) & (sj < lo + BLOCK) & (ti4 == t)))
    os_t = attend((_stack_heads(_pad_rows(qs_ref[0], tq)) * LOG2E).astype(BF16), cached, snew_ref)

    cached = []
    for j in range(WINDOW // LANES):
        dist = (WINDOW - LANES * j) + ti4 - sj
        cached.append((cwin_ref[0, :, j * LANES:(j + 1) * LANES], WINDOW // LANES - j,
                       (dist >= 0) & (dist <= WINDOW)))
    ow_t = attend((_stack_heads(_pad_rows(qw_ref[0], tq)) * LOG2E).astype(BF16), cached, wnew_ref)

    gt2 = gt_ref.at[0]
    out_t = (compact(_gate_row(gt2, 0, g)) * compact(oct_ref[0]) + compact(_gate_row(gt2, 1, g)) * os_t
             + compact(_gate_row(gt2, 2, g)) * ow_t)
    rows = _pad_rows(out_t, LANES).T
    o_ref[0] = jnp.concatenate([rows[r * tq:r * tq + 8, :HD] for r in range(REP)], axis=1)


def _attn_s2(ids, phys, qs8, gates_t, oct, cache_sel_t, sel_new8, cache_win_t, win_new8, bias_tiles, q0, nvalid):
    batch = qs8.shape[0]
    nsel = nvalid * (TOPK - 1)

    def kb_spec(t, k):
        return pl.BlockSpec((1, LANES, PAGE),
                            lambda b, g, ids_r, phys_r, t=t, k=k:
                            (phys_r[((b * NKV + g) * nvalid + t) * TOPK + k], g, 0))

    def bg(shape, col):
        return pl.BlockSpec(shape, lambda b, g, ids_r, phys_r, col=col: (b, 0, col * NKV + g))

    return pl.pallas_call(
        functools.partial(_attn_s2_kernel, q0=q0, nvalid=nvalid),
        grid_spec=pltpu.PrefetchScalarGridSpec(
            num_scalar_prefetch=2,
            grid=(batch, NKV),
            in_specs=[bg((1, 8, REP * HD), 1), bg((1, 8, REP * HD), 2),
                      pl.BlockSpec((1, LANES, TQ), lambda b, g, ids_r, phys_r: (b, 0, 0)),
                      pl.BlockSpec((1, HD, REP * TQ), lambda b, g, ids_r, phys_r: (b * NKV + g, 0, 0))]
                     + [kb_spec(t, k) for t in range(nvalid) for k in range(TOPK - 1)]
                     + [bg((1, 8, LANES), 0),
                        pl.BlockSpec((1, LANES, WINDOW), lambda b, g, ids_r, phys_r: (b, g, 0)),
                        bg((1, 8, LANES), 0),
                        pl.BlockSpec((N_BIAS_TILES, REP, LANES, TQ), lambda b, g, ids_r, phys_r: (0, g, 0, 0))],
            out_specs=bg((1, 8, REP * HD), 0)),
        out_shape=jax.ShapeDtypeStruct((batch, 8, NKV * REP * HD), F32),
        compiler_params=_cparams("arbitrary", "arbitrary"),
        name="attn_sample",
    )(ids, phys, qs8, qs8, gates_t, oct, *([cache_sel_t] * nsel), sel_new8, cache_win_t, win_new8, bias_tiles)


def _pad_cols(w, n):
    return jnp.pad(w, ((0, 0), (0, n - w.shape[1])))


def _prep(p):
    inner = NH_A * DK_A
    width = NKV * REP * HD
    q = {}
    q['w_in'] = _pad_cols(p['w_in_a'][0], 4 * inner + LANES).astype(BF16)
    q['b_if'] = _pad_cols(p['b_if_a'][0][None, :], LANES)
    q['w_out'] = p['w_out_a'][0].astype(BF16)
    q['w_ff1'] = [p['w_ff1'][l].astype(BF16) for l in range(2)]
    q['w_ff2'] = [p['w_ff2'][l].astype(BF16) for l in range(2)]
    q['w_kv'] = p['w_kv'].astype(BF16)
    q['w_q'] = _pad_cols(p['w_q_b'][0], 3 * width + LANES).astype(BF16)
    q['b_gate'] = _pad_cols(p['b_gate_b'][0][None, :], LANES)
    q['w_o'] = p['w_o_b'][0].astype(BF16)
    q['q_gain'] = (jnp.tile(p['q_norm_b'][0][:, None, :], (1, NKV * REP, 1)) * (HD ** -0.5)).reshape(1, 3 * width)
    seg = np.arange(256) // HD
    q['bd'] = jnp.asarray((seg[:, None] == seg[None, :]).astype(np.float32) / HD, dtype=BF16)
    ones = jnp.ones((HD,), F32)
    q['k_gain'] = jnp.concatenate([jnp.tile(jnp.concatenate([p['k_norm'][br], ones]), NKV) for br in (1, 2)])[None, :]
    q['k_mask'] = jnp.tile(jnp.concatenate([ones, 0.0 * ones]), 2 * NKV)[None, :]
    q['k_gain0'] = jnp.concatenate([p['k_norm'][0], ones])[None, :]
    w1 = p['w_cmp1']
    z = jnp.zeros_like(w1[0])
    per_l = jnp.concatenate([jnp.concatenate([w1[0], z], axis=-1), jnp.concatenate([z, w1[1]], axis=-1)], axis=1)
    q['w_cmp1'] = per_l.reshape(BLOCK // 2, 2 * 2 * HD, 2 * w1.shape[-1]).astype(BF16)
    w2 = p['w_cmp2']
    z2 = jnp.zeros_like(w2[0])
    q['w_cmp2'] = jnp.concatenate([jnp.concatenate([w2[0], z2], axis=1), jnp.concatenate([z2, w2[1]], axis=1)],
                                  axis=0).astype(BF16)
    q['pos_rows'] = jnp.tile(p['cmp_pos'].reshape(BLOCK, 2 * HD), (1, NKV))
    return q


def _feature_major(cache):
    n, t = cache.shape[:2]
    return jnp.transpose(cache, (0, 2, 3, 4, 1)).reshape(n, -1, t)


def _mods(ada_rows, rep):
    k = ada_rows.shape[1] // D_MODEL
    out = []
    for j in range(k):
        a = ada_rows[:, j * D_MODEL:(j + 1) * D_MODEL]
        out.append(jnp.repeat(a, rep, axis=0) if rep > 1 else a)
    return out


def _layer0(x, mods, p, q, mlstm_fn, tm, tmf):
    sh1, sc1, g1, sh2, sc2, g2 = mods
    proj = _proj_plain(x, p['norm_mix'][0][None, :], sh1, sc1, q['w_in'], tm)
    y, states = mlstm_fn(proj)
    x = _out_proj(x, g1, y, q['w_out'], tm)
    x = _ffn(x, p['norm_ffn'][0][None, :], sh2, sc2, g2, q['w_ff1'][0], q['w_ff2'][0], tmf)
    return x, states


def _layer1_tail(x, attn, mods, p, q, tm, tmf):
    _, _, g1, sh2, sc2, g2 = mods
    x = _out_proj(x, g1, attn, q['w_o'], tm)
    return _ffn(x, p['norm_ffn'][1][None, :], sh2, sc2, g2, q['w_ff1'][1], q['w_ff2'][1], tmf)


def kernel(x_prompt, x_sample, cache_cmp, cache_sel, cache_win, state_C, state_n, state_m, page_table,
           c_prompt, c_sample, w_ada, b_ada, norm_mix, norm_ffn, w_ff1, w_ff2, w_in_a, b_if_a,
           head_norm_a, w_out_a, w_ada_kv, b_ada_kv, norm_kv, w_kv, k_norm, cmp_pos, w_cmp1, w_cmp2,
           w_q_b, b_gate_b, q_norm_b, w_o_b, rel_bias):
    p = dict(norm_mix=norm_mix, norm_ffn=norm_ffn, w_ff1=w_ff1, w_ff2=w_ff2, w_in_a=w_in_a, b_if_a=b_if_a,
             w_out_a=w_out_a, w_kv=w_kv, k_norm=k_norm, cmp_pos=cmp_pos, w_cmp1=w_cmp1, w_cmp2=w_cmp2,
             w_q_b=w_q_b, b_gate_b=b_gate_b, q_norm_b=q_norm_b, w_o_b=w_o_b)
    q = _prep(p)
    bp, tp, d = x_prompt.shape
    bs, ts, _ = x_sample.shape
    past = page_table.shape[1] * PAGE
    wbuf = cache_win.shape[1]
    assert bp == 1 and ts < BLOCK and ts <= 8 and wbuf == WINDOW and past % LANES == 0
    assert tp % (SUPER * SUPERS_PER_STEP) == 0
    width = NKV * REP * HD
    kvw = NKV * 2 * HD

    nc = bp + bs
    c_all = jnp.pad(jnp.concatenate([c_prompt, c_sample], axis=0), ((0, -nc % 8), (0, 0)))
    ada = [_ada(c_all, w_ada[l], b_ada[l]) for l in range(2)]
    ada_kv = _ada(c_all, w_ada_kv, b_ada_kv)
    bias_tiles = _bias_tiles(rel_bias)
    head_gain = head_norm_a[0]

    xp = x_prompt.reshape(tp, d)
    zc = jnp.zeros((bp, NH_A, DK_A, DK_A), F32)
    zn = jnp.zeros((bp, NH_A, 1, DK_A), F32)
    zm = jnp.zeros((bp, 8, LANES), F32)

    def mlstm_prompt(proj):
        y, c, n, m = _mlstm(proj.reshape(bp, tp, -1), q['b_if'], head_gain, zc, zn, zm,
                            rows=256, valid=256, chunk=256, zero_init=True)
        return y.reshape(tp, -1), (c, n, m)

    xp, (pc, pn, pm) = _layer0(xp, _mods(ada[0][:bp], 1), p, q, mlstm_prompt, 256, 512)
    sh, sc = _mods(ada_kv[:bp], 1)
    p_cmp, _, _, p_cmp_t, p_sel_t, p_win_t, sel_bf, selt_bf, win_bf, wint_bf = _proj_kv(
        xp, norm_kv[None, :], sh, sc, q['w_kv'], q['bd'], q['k_gain'], q['k_mask'], 256)
    kcvc_p = _compress(p_cmp, q['pos_rows'], q['w_cmp1'], q['w_cmp2'], q['k_gain0'],
                       min(CMP_BLOCKS_PER_STEP, tp // BLOCK))
    mods1 = _mods(ada[1][:bp], 1)
    qs_p, gt_p = _proj_q(xp, norm_mix[1][None, :], mods1[0], mods1[1], q['w_q'], q['bd'], q['q_gain'],
                         q['b_gate'], 256)
    attn_p = _attn_prompt(rel_bias, qs_p, gt_p, kcvc_p, sel_bf, selt_bf, win_bf, wint_bf, bias_tiles)
    y_prompt = _layer1_tail(xp, attn_p, mods1, p, q, 256, 512).reshape(bp, tp, d)

    ms = bs * ts
    xs = x_sample.reshape(ms, d)
    m0 = jnp.broadcast_to(jnp.pad(state_m[0], ((0, 0), (0, 8 - NH_A)))[:, :, None], (bs, 8, LANES))

    def mlstm_sample(proj):
        proj8 = jnp.pad(proj.reshape(bs, ts, -1), ((0, 0), (0, 8 - ts), (0, 0)))
        y, c, n, m = _mlstm(proj8, q['b_if'], head_gain, state_C[0], state_n[0][:, :, None, :], m0,
                            rows=8, valid=ts, chunk=LANES, zero_init=False)
        return y[:, :ts].reshape(ms, -1), (c, n, m)

    xs, (sc_, sn_, sm_) = _layer0(xs, _mods(ada[0][bp:nc], ts), p, q, mlstm_sample, ms, ms)
    sh, sc = _mods(ada_kv[bp:nc], ts)
    s_cmp, s_sel, s_win = _proj_kv(xs, norm_kv[None, :], sh, sc, q['w_kv'], q['bd'], q['k_gain'],
                                   q['k_mask'], ms)[:3]
    kcvc_s = _compress_paged(_feature_major(cache_cmp), page_table, q['pos_rows'],
                             q['w_cmp1'], q['w_cmp2'], q['k_gain0'], min(CMP_BLOCKS_PER_STEP, past // BLOCK))
    mods1 = _mods(ada[1][bp:nc], ts)
    qs_s, gt_s = _proj_q(xs, norm_mix[1][None, :], mods1[0], mods1[1], q['w_q'], q['bd'], q['q_gain'],
                         q['b_gate'], ms)

    def pad8(a):
        return jnp.pad(a.reshape(bs, ts, -1), ((0, 0), (0, 8 - ts), (0, 0)))

    qs8 = pad8(qs_s)
    oct_s, idx_s = _attn_s1(rel_bias, qs8, kcvc_s, past)
    ids = jnp.transpose(idx_s.reshape(bs, NKV, TOPK, TQ)[:, :, :, :ts], (0, 1, 3, 2))
    idc = jnp.maximum(ids, 0)
    pages = jnp.take_along_axis(page_table, (idc // (PAGE // BLOCK)).reshape(bs, -1), axis=1).reshape(ids.shape)
    gt8 = jnp.pad(jnp.transpose(gt_s.reshape(LANES, bs, ts), (1, 0, 2)), ((0, 0), (0, 0), (0, TQ - ts)))
    attn_s = _attn_s2(ids.reshape(-1), pages.reshape(-1), qs8, gt8, oct_s,
                      _feature_major(cache_sel), pad8(s_sel), _feature_major(cache_win), pad8(s_win),
                      bias_tiles, past, ts)
    y_sample = _layer1_tail(xs, attn_s[:, :ts].reshape(ms, width), mods1, p, q, ms, ms).reshape(bs, ts, d)

    rows5 = (NKV, 2, HD)

    def token_major(a_t):
        return jnp.transpose(a_t.reshape(*rows5, a_t.shape[1]), (3, 0, 1, 2))[None]

    s_win_all = jnp.concatenate([cache_win, s_win.reshape(bs, ts, *rows5)], axis=1)
    return (y_prompt, y_sample,
            pc[None], pn.reshape(1, bp, NH_A, DK_A), pm[None, :, :NH_A, 0],
            token_major(p_cmp_t), token_major(p_sel_t), token_major(p_win_t[:, tp - min(WINDOW, tp):]),
            sc_[None], sn_.reshape(1, bs, NH_A, DK_A), sm_[None, :, :NH_A, 0],
            s_cmp.reshape(bs, ts, *rows5), s_sel.reshape(bs, ts, *rows5),
            s_win_all[:, -min(WINDOW, wbuf + ts):])
```

```python
import functools
import math

import numpy as np
import jax
import jax.numpy as jnp
from jax import lax
from jax.experimental import pallas as pl
from jax.experimental.pallas import tpu as pltpu

F32 = jnp.float32
BF16 = jnp.bfloat16
I32 = jnp.int32

D_MODEL = 1024
NH_A = 4
DK_A = 256
NKV = 4
REP = 4
HD = 64
BLOCK = 64
TOPK = 16
WINDOW = 512
N_BUCKETS = 32
REL_MAX_DIST = 2048
PAGE = 128
EPS = 1e-6
NEG = -1e30
LOG2E = math.log2(math.e)

LANES = 128
VMEM_LIMIT = 56 * 1024 * 1024
TQ = 128
TQ_SAMPLE = 32
KT = 256
SUPER_BLOCKS = 16
SUPER = SUPER_BLOCKS * BLOCK
SUPERS_PER_STEP = 2
VT_ROWS = HD + 16
N_BIAS_TILES = 14
CMP_BLOCKS_PER_STEP = 64
CMP_BIAS_ROWS = 40


def _bucket_thresholds():
    exact = N_BUCKETS // 2
    d = np.arange(1, 4 * REL_MAX_DIST, dtype=np.float64)
    big = exact + np.floor(np.log(d / exact) / math.log(REL_MAX_DIST / exact) * (N_BUCKETS - exact)).astype(np.int64)
    b = np.where(d < exact, d.astype(np.int64), np.minimum(big, N_BUCKETS - 1))
    return [int(d[np.argmax(b >= k)]) for k in range(1, N_BUCKETS)]


BUCKET_THR = _bucket_thresholds()
assert 128 * (N_BIAS_TILES - 1) - (TQ - 1) >= BUCKET_THR[-1]


def _cparams(*sem):
    return pltpu.CompilerParams(dimension_semantics=sem, vmem_limit_bytes=VMEM_LIMIT)


def _dot(a, b):
    return jnp.dot(a.astype(BF16), b.astype(BF16), preferred_element_type=F32)


def _dot_nt(a, b):
    return lax.dot_general(a.astype(BF16), b.astype(BF16), (((1,), (1,)), ((), ())),
                           preferred_element_type=F32)


def _split3(x):
    hi = x.astype(BF16)
    r1 = x - hi.astype(F32)
    mid = r1.astype(BF16)
    lo = (r1 - mid.astype(F32)).astype(BF16)
    return hi, mid, lo


def _sigmoid(x):
    return 1.0 / (1.0 + jnp.exp(-x))


def _norm_mod(x, gain, shift, scale):
    ms = jnp.mean(x * x, axis=-1, keepdims=True)
    y = x * lax.rsqrt(ms + EPS) * gain
    return y * (1.0 + scale) + shift


def _seg_mean_sq(y, bd):
    parts = []
    for j in range(y.shape[1] // 256):
        sq = y[:, j * 256:(j + 1) * 256]
        sq = sq * sq
        hi = sq.astype(BF16)
        lo = (sq - hi.astype(F32)).astype(BF16)
        parts.append(jnp.dot(hi, bd, preferred_element_type=F32) + jnp.dot(lo, bd, preferred_element_type=F32))
    return parts[0] if len(parts) == 1 else jnp.concatenate(parts, axis=1)


def _row_spec(tm, n, per_row):
    if per_row:
        return pl.BlockSpec((tm, n), lambda i: (i, 0))
    return pl.BlockSpec((1, n), lambda i: (0, 0))


def _const_spec(shape):
    return pl.BlockSpec(shape, lambda i: tuple(0 for _ in shape))


def _ada_kernel(c_ref, w_ref, b_ref, o_ref):
    c = c_ref[...]
    o_ref[...] = _dot(c * _sigmoid(c), w_ref[...]) + b_ref[...]


def _ada(c, w, b):
    m, k = c.shape
    n = w.shape[1]
    tn = 1024
    return pl.pallas_call(
        _ada_kernel,
        grid=(n // tn,),
        in_specs=[pl.BlockSpec((m, k), lambda j: (0, 0)),
                  pl.BlockSpec((k, tn), lambda j: (0, j)),
                  pl.BlockSpec((1, tn), lambda j: (0, j))],
        out_specs=pl.BlockSpec((m, tn), lambda j: (0, j)),
        out_shape=jax.ShapeDtypeStruct((m, n), F32),
        compiler_params=_cparams("arbitrary"),
        name="ada",
    )(c, w, b.reshape(1, n))


def _proj_plain_kernel(x_ref, gain_ref, sh_ref, sc_ref, w_ref, o_ref):
    xn = _norm_mod(x_ref[...], gain_ref[...], sh_ref[...], sc_ref[...])
    o_ref[...] = jnp.dot(xn.astype(BF16), w_ref[...], preferred_element_type=F32)


def _proj_plain(x, gain, shift, scale, w_bf, tm):
    m, d = x.shape
    n = w_bf.shape[1]
    per_row = shift.shape[0] != 1
    return pl.pallas_call(
        _proj_plain_kernel,
        grid=(m // tm,),
        in_specs=[pl.BlockSpec((tm, d), lambda i: (i, 0)),
                  _const_spec((1, d)),
                  _row_spec(tm, d, per_row), _row_spec(tm, d, per_row),
                  _const_spec((d, n))],
        out_specs=pl.BlockSpec((tm, n), lambda i: (i, 0)),
        out_shape=jax.ShapeDtypeStruct((m, n), F32),
        compiler_params=_cparams("arbitrary"),
        name="proj_mlstm",
    )(x, gain, shift, scale, w_bf)


def _proj_q_kernel(x_ref, gain_ref, sh_ref, sc_ref, w_ref, bd_ref, qg_ref, bg_ref, q_ref, gt_ref):
    xn = _norm_mod(x_ref[...], gain_ref[...], sh_ref[...], sc_ref[...])
    y = jnp.dot(xn.astype(BF16), w_ref[...], preferred_element_type=F32)
    nq = q_ref.shape[1]
    yq = y[:, :nq]
    ms = _seg_mean_sq(yq, bd_ref[...])
    q_ref[...] = yq * lax.rsqrt(ms + EPS) * qg_ref[...]
    gates = _sigmoid(y[:, nq:] + bg_ref[...])
    gt_ref[...] = gates.T


def _proj_q(x, gain, shift, scale, w_bf, bd, qgain_row, bgate_row, tm):
    m, d = x.shape
    n = w_bf.shape[1]
    nq = n - LANES
    per_row = shift.shape[0] != 1
    return pl.pallas_call(
        _proj_q_kernel,
        grid=(m // tm,),
        in_specs=[pl.BlockSpec((tm, d), lambda i: (i, 0)),
                  _const_spec((1, d)),
                  _row_spec(tm, d, per_row), _row_spec(tm, d, per_row),
                  _const_spec((d, n)), _const_spec((256, 256)),
                  _const_spec((1, nq)), _const_spec((1, LANES))],
        out_specs=[pl.BlockSpec((tm, nq), lambda i: (i, 0)),
                   pl.BlockSpec((LANES, tm), lambda i: (0, i))],
        out_shape=[jax.ShapeDtypeStruct((m, nq), F32),
                   jax.ShapeDtypeStruct((LANES, m), F32)],
        compiler_params=_cparams("arbitrary"),
        name="proj_q",
    )(x, gain, shift, scale, w_bf, bd, qgain_row, bgate_row)


def _proj_kv_kernel(x_ref, gain_ref, sh_ref, sc_ref, w_ref, bd_ref, kg_ref, km_ref,
                    cmp_ref, sel_ref, win_ref, cmpt_ref, selt_ref, wint_ref,
                    selb_ref, seltb_ref, winb_ref, wintb_ref):
    xn = _norm_mod(x_ref[...], gain_ref[...], sh_ref[...], sc_ref[...])
    y = jnp.dot(xn.astype(BF16), w_ref[...], preferred_element_type=F32)
    w = cmp_ref.shape[1]
    cmp = y[:, :w]
    ykn = y[:, w:]
    ms = _seg_mean_sq(ykn, bd_ref[...])
    ykn = jnp.where(km_ref[...] > 0.5, ykn * lax.rsqrt(ms + EPS) * kg_ref[...], ykn)
    sel = ykn[:, :w]
    win = ykn[:, w:]
    sel_t = sel.T
    win_t = win.T
    cmp_ref[...] = cmp
    sel_ref[...] = sel
    win_ref[...] = win
    cmpt_ref[...] = cmp.T
    selt_ref[...] = sel_t
    wint_ref[...] = win_t
    tm = sel.shape[0]
    tok = pl.program_id(0) * tm + lax.broadcasted_iota(I32, sel.shape, 0)
    lane = lax.broadcasted_iota(I32, sel.shape, 1) % LANES
    onehot = jnp.where(lane - HD == (tok // BLOCK) % SUPER_BLOCKS, 1.0, 0.0)
    selb_ref[...] = jnp.where(lane < HD, sel, onehot).astype(BF16)
    winb_ref[...] = win.astype(BF16)

    def value_rows(x_t):
        ones_blk = jnp.where(lax.broadcasted_iota(I32, (HD, tm), 0) == 0, 1.0, 0.0)
        return jnp.concatenate(
            [piece for gg in range(NKV) for piece in (x_t[gg * LANES + HD:(gg + 1) * LANES], ones_blk)],
            axis=0).astype(BF16)

    seltb_ref[...] = value_rows(sel_t)
    wintb_ref[...] = value_rows(win_t)


def _proj_kv(x, gain, shift, scale, w_bf, bd, kgain_row, kmask_row, tm):
    m, d = x.shape
    n = w_bf.shape[1]
    w = n // 3
    per_row = shift.shape[0] != 1
    row = pl.BlockSpec((tm, w), lambda i: (i, 0))
    col = pl.BlockSpec((w, tm), lambda i: (0, i))
    return pl.pallas_call(
        _proj_kv_kernel,
        grid=(m // tm,),
        in_specs=[pl.BlockSpec((tm, d), lambda i: (i, 0)),
                  _const_spec((1, d)),
                  _row_spec(tm, d, per_row), _row_spec(tm, d, per_row),
                  _const_spec((d, n)), _const_spec((256, 256)),
                  _const_spec((1, 2 * w)), _const_spec((1, 2 * w))],
        out_specs=[row, row, row, col, col, col, row, col, row, col],
        out_shape=[jax.ShapeDtypeStruct((m, w), F32)] * 3 + [jax.ShapeDtypeStruct((w, m), F32)] * 3
                  + [jax.ShapeDtypeStruct((m, w), BF16), jax.ShapeDtypeStruct((w, m), BF16),
                     jax.ShapeDtypeStruct((m, w), BF16), jax.ShapeDtypeStruct((w, m), BF16)],
        compiler_params=_cparams("arbitrary"),
        name="proj_kv",
    )(x, gain, shift, scale, w_bf, bd, kgain_row, kmask_row)


def _out_proj_kernel(x_ref, g_ref, y_ref, w_ref, o_ref):
    o_ref[...] = x_ref[...] + g_ref[...] * jnp.dot(y_ref[...].astype(BF16), w_ref[...],
                                                   preferred_element_type=F32)


def _out_proj(x, g, y, w_bf, tm):
    m, d = x.shape
    k = y.shape[1]
    per_row = g.shape[0] != 1
    return pl.pallas_call(
        _out_proj_kernel,
        grid=(m // tm,),
        in_specs=[pl.BlockSpec((tm, d), lambda i: (i, 0)),
                  _row_spec(tm, d, per_row),
                  pl.BlockSpec((tm, k), lambda i: (i, 0)),
                  _const_spec((k, d))],
        out_specs=pl.BlockSpec((tm, d), lambda i: (i, 0)),
        out_shape=jax.ShapeDtypeStruct((m, d), F32),
        compiler_params=_cparams("arbitrary"),
        name="out_proj",
    )(x, g, y, w_bf)


def _ffn_kernel(x_ref, gain_ref, sh_ref, sc_ref, g_ref, w1_ref, w2_ref, o_ref, *, fc):
    x = x_ref[...]
    xn = _norm_mod(x, gain_ref[...], sh_ref[...], sc_ref[...]).astype(BF16)
    acc = jnp.zeros(x.shape, F32)
    for c in range(w1_ref.shape[1] // fc):
        h = jnp.dot(xn, w1_ref[:, c * fc:(c + 1) * fc], preferred_element_type=F32)
        h = jnp.maximum(h, 0.0)
        acc = acc + jnp.dot((h * h).astype(BF16), w2_ref[c * fc:(c + 1) * fc, :],
                            preferred_element_type=F32)
    o_ref[...] = x + g_ref[...] * acc


def _ffn(x, gain, shift, scale, g, w1_bf, w2_bf, tm):
    m, d = x.shape
    f = w1_bf.shape[1]
    per_row = shift.shape[0] != 1
    return pl.pallas_call(
        functools.partial(_ffn_kernel, fc=1024),
        grid=(m // tm,),
        in_specs=[pl.BlockSpec((tm, d), lambda i: (i, 0)),
                  _const_spec((1, d)),
                  _row_spec(tm, d, per_row), _row_spec(tm, d, per_row), _row_spec(tm, d, per_row),
                  _const_spec((d, f)), _const_spec((f, d))],
        out_specs=pl.BlockSpec((tm, d), lambda i: (i, 0)),
        out_shape=jax.ShapeDtypeStruct((m, d), F32),
        compiler_params=_cparams("arbitrary"),
        name="ffn",
    )(x, gain, shift, scale, g, w1_bf, w2_bf)


def _mlstm_kernel(q_ref, k_ref, v_ref, o_ref, gt_ref, bif_ref, hg_ref, c0_ref, n0_ref, m0_ref,
                  y_ref, cout_ref, nout_ref, mout_ref, c_scr, n_scr, m_scr, *, rows, valid, chunk, zero_init):
    ci = pl.program_id(1)
    nci = pl.num_programs(1)

    @pl.when(ci == 0)
    def _():
        if zero_init:
            c_scr[...] = jnp.zeros(c_scr.shape, F32)
            n_scr[...] = jnp.zeros(n_scr.shape, F32)
            m_scr[...] = jnp.zeros(m_scr.shape, F32)
        else:
            c_scr[...] = c0_ref[0]
            n_scr[...] = n0_ref[0]
            m_scr[...] = m0_ref[0]

    L = chunk

    def padded(ref_val, fill):
        if rows == L:
            return ref_val
        pad = jnp.full((L - rows, ref_val.shape[1]), fill, ref_val.dtype)
        return jnp.concatenate([ref_val, pad], axis=0)

    q = padded(q_ref[0], 0.0)
    k = padded(k_ref[0], 0.0)
    v = padded(v_ref[0], 0.0)
    og = padded(o_ref[0], 0.0)
    g = padded(gt_ref[0], 0.0) + bif_ref[...]
    lf = -(jnp.maximum(-g, 0.0) + jnp.log1p(jnp.exp(-jnp.abs(g))))
    li = g
    if valid != L:
        is_real = lax.broadcasted_iota(I32, (L, LANES), 0) < valid
        lf = jnp.where(is_real, lf, 0.0)
        li = jnp.where(is_real, li, NEG)
    r_io = lax.broadcasted_iota(I32, (L, L), 0)
    c_io = lax.broadcasted_iota(I32, (L, L), 1)
    causal = c_io <= r_io
    tril = jnp.where(causal, 1.0, 0.0).astype(BF16)
    hi, mid, lo = _split3(lf)
    b = (jnp.dot(tril, hi, preferred_element_type=F32) + jnp.dot(tril, mid, preferred_element_type=F32)
         + jnp.dot(tril, lo, preferred_element_type=F32))
    lane = lax.broadcasted_iota(I32, (L, LANES), 1)
    mixed_t = jnp.where(lane < NH_A, li, b).T
    for h in range(NH_A):
        sl = slice(h * DK_A, (h + 1) * DK_A)
        qh = q[:, sl] * (DK_A ** -0.5)
        kh = k[:, sl]
        vh = v[:, sl]
        b_col = b[:, NH_A + h:NH_A + h + 1]
        li_col = li[:, h:h + 1]
        b_row = mixed_t[NH_A + h:NH_A + h + 1, :]
        li_row = mixed_t[h:h + 1, :]
        m_prev = m_scr[h:h + 1, 0:1]
        c_prev = c_scr[h]
        n_prev = n_scr[h]
        dmat = jnp.where(causal, b_col - b_row + li_row, NEG)
        m_inter = b_col + m_prev
        m_t = jnp.maximum(m_inter, jnp.max(dmat, axis=-1, keepdims=True))
        s = _dot_nt(qh, kh) * jnp.exp(dmat - m_t)
        a_inter = jnp.exp(m_inter - m_t)
        num = _dot(s, vh) + a_inter * _dot(qh, c_prev)
        den = jnp.sum(s, axis=-1, keepdims=True) + a_inter * jnp.sum(qh * n_prev, axis=-1, keepdims=True)
        hh = num / jnp.maximum(jnp.abs(den), jnp.exp(-m_t))
        b_last = b_col[L - 1:L, :]
        g_col = b_last - b_col + li_col
        m_new = jnp.maximum(b_last + m_prev, jnp.max(g_col, axis=0, keepdims=True))
        w_col = jnp.exp(g_col - m_new)
        decay = jnp.exp(b_last + m_prev - m_new)
        kw = kh * w_col
        c_scr[h] = decay * c_prev + _dot(kw.T, vh)
        n_scr[h] = decay * n_prev + jnp.sum(kw, axis=0, keepdims=True)
        m_scr[h:h + 1, :] = jnp.broadcast_to(m_new, (1, LANES))
        hn = hh * lax.rsqrt(jnp.mean(hh * hh, axis=-1, keepdims=True) + EPS) * hg_ref[h:h + 1, :]
        yh = hn * _sigmoid(og[:, sl])
        y_ref[0, :, sl] = yh[:rows]

    @pl.when(ci == nci - 1)
    def _():
        cout_ref[0] = c_scr[...]
        nout_ref[0] = n_scr[...]
        mout_ref[0] = m_scr[...]


def _mlstm(proj, b_if_row, head_gain, c0, n0, m0, *, rows, valid, chunk, zero_init):
    batch, t, _ = proj.shape
    nch = t // rows
    inner = NH_A * DK_A

    def colblk(j, width):
        return pl.BlockSpec((1, rows, width), lambda b, c, j=j: (b, c, j))

    state_c = pl.BlockSpec((1, NH_A, DK_A, DK_A), lambda b, c: (b, 0, 0, 0))
    state_n = pl.BlockSpec((1, NH_A, 1, DK_A), lambda b, c: (b, 0, 0, 0))
    state_m = pl.BlockSpec((1, 8, LANES), lambda b, c: (b, 0, 0))
    return pl.pallas_call(
        functools.partial(_mlstm_kernel, rows=rows, valid=valid, chunk=chunk, zero_init=zero_init),
        grid=(batch, nch),
        in_specs=[colblk(0, inner), colblk(1, inner), colblk(2, inner), colblk(3, inner),
                  colblk(4 * inner // LANES, LANES),
                  pl.BlockSpec((1, LANES), lambda b, c: (0, 0)),
                  pl.BlockSpec((NH_A, DK_A), lambda b, c: (0, 0)),
                  state_c, state_n, state_m],
        out_specs=[pl.BlockSpec((1, rows, inner), lambda b, c: (b, c, 0)), state_c, state_n, state_m],
        out_shape=[jax.ShapeDtypeStruct((batch, t, inner), F32),
                   jax.ShapeDtypeStruct((batch, NH_A, DK_A, DK_A), F32),
                   jax.ShapeDtypeStruct((batch, NH_A, 1, DK_A), F32),
                   jax.ShapeDtypeStruct((batch, 8, LANES), F32)],
        scratch_shapes=[pltpu.VMEM((NH_A, DK_A, DK_A), F32), pltpu.VMEM((NH_A, 1, DK_A), F32),
                        pltpu.VMEM((8, LANES), F32)],
        compiler_params=_cparams("arbitrary", "arbitrary"),
        name="mlstm",
    )(proj, proj, proj, proj, proj, b_if_row, head_gain, c0, n0, m0)


def _compress_body(r_refs, pos_ref, w1_ref, w2_ref, kg_ref, o_ref, nb):
    acc = None
    for lp in range(BLOCK // 2):
        pieces = []
        for g in range(NKV):
            xa = r_refs[g][pl.ds(2 * lp, nb, stride=BLOCK), :]
            xb = r_refs[g][pl.ds(2 * lp + 1, nb, stride=BLOCK), :]
            pieces.append(jnp.concatenate([xa, xb], axis=1))
        pa = pos_ref[2 * lp:2 * lp + 1, 0:LANES]
        pb = pos_ref[2 * lp + 1:2 * lp + 2, 0:LANES]
        pieces.append(jnp.broadcast_to(jnp.concatenate([pa, pb], axis=1), (8, 2 * LANES)))
        x = jnp.concatenate(pieces, axis=0).astype(BF16)
        d = jnp.dot(x, w1_ref[lp], preferred_element_type=F32)
        acc = d if acc is None else acc + d
    pre = acc[:NKV * nb] + acc[NKV * nb:NKV * nb + 1]
    hid = pre * _sigmoid(pre)
    out = jnp.dot(hid.astype(BF16), w2_ref[...], preferred_element_type=F32)
    is_k = lax.broadcasted_iota(I32, out.shape, 1) < HD
    ms = jnp.sum(jnp.where(is_k, out * out, 0.0), axis=-1, keepdims=True) * (1.0 / HD)
    out = jnp.where(is_k, out * lax.rsqrt(ms + EPS) * kg_ref[...], out)
    for g in range(NKV):
        o_ref[:, g * LANES:(g + 1) * LANES] = out[g * nb:(g + 1) * nb]


def _compress_kernel(r0, r1, r2, r3, pos_ref, w1_ref, w2_ref, kg_ref, o_ref, *, nb):
    _compress_body((r0, r1, r2, r3), pos_ref, w1_ref, w2_ref, kg_ref, o_ref, nb)


def _compress_paged_kernel(pt_ref, *refs, nb):
    npages = nb * BLOCK // PAGE
    pages = refs[:npages]
    pos_ref, w1_ref, w2_ref, kg_ref, o_ref = refs[npages:npages + 5]
    r_scrs = refs[npages + 5:]
    for p in range(npages):
        for g in range(NKV):
            r_scrs[g][p * PAGE:(p + 1) * PAGE, :] = pages[p][0, g * LANES:(g + 1) * LANES, :].T
    _compress_body(r_scrs, pos_ref, w1_ref, w2_ref, kg_ref, o_ref, nb)


def _compress_paged(cache_t, page_table, pos_rows, w1p, w2p, kgain_row, nb):
    batch, npg = page_table.shape
    w = cache_t.shape[1]
    npages = nb * BLOCK // PAGE
    steps = npg // npages

    def page_spec(p):
        return pl.BlockSpec((1, w, PAGE), lambda b, j, pt, p=p: (pt[b, j * npages + p], 0, 0))

    def const(shape):
        return pl.BlockSpec(shape, lambda b, j, pt: tuple(0 for _ in shape))

    return pl.pallas_call(
        functools.partial(_compress_paged_kernel, nb=nb),
        grid_spec=pltpu.PrefetchScalarGridSpec(
            num_scalar_prefetch=1,
            grid=(batch, steps),
            in_specs=[page_spec(p) for p in range(npages)]
                     + [const((BLOCK, w)), const(w1p.shape), const(w2p.shape), const((1, LANES))],
            out_specs=pl.BlockSpec((nb, w), lambda b, j, pt: (b * steps + j, 0)),
            scratch_shapes=[pltpu.VMEM((nb * BLOCK, LANES), F32) for _ in range(NKV)]),
        out_shape=jax.ShapeDtypeStruct((batch * npg * PAGE // BLOCK, w), F32),
        compiler_params=_cparams("arbitrary", "arbitrary"),
        name="compress_paged",
    )(page_table, *([cache_t] * npages), pos_rows, w1p, w2p, kgain_row)


def _compress(rows, pos_rows, w1p, w2p, kgain_row, nb):
    t, w = rows.shape
    nblk = t // BLOCK
    return pl.pallas_call(
        functools.partial(_compress_kernel, nb=nb),
        grid=(nblk // nb,),
        in_specs=[pl.BlockSpec((nb * BLOCK, LANES), lambda i, g=g: (i, g)) for g in range(NKV)]
                 + [_const_spec((BLOCK, w)), _const_spec(w1p.shape), _const_spec(w2p.shape),
                    _const_spec((1, LANES))],
        out_specs=pl.BlockSpec((nb, w), lambda i: (i, 0)),
        out_shape=jax.ShapeDtypeStruct((nblk, w), F32),
        compiler_params=_cparams("arbitrary"),
        name="compress",
    )(rows, rows, rows, rows, pos_rows, w1p, w2p, kgain_row)


def _bias_from_dist(dist, tab_ref, h):
    out = jnp.full(dist.shape, tab_ref[0, h], F32)
    for kk in range(1, N_BUCKETS):
        out = jnp.where(dist >= BUCKET_THR[kk - 1], tab_ref[kk, h], out)
    return out


def _bias_tiles_kernel(tab_ref, kq_ref, qk_ref):
    a = pl.program_id(0)
    row = lax.broadcasted_iota(I32, (LANES, TQ), 0)
    col = lax.broadcasted_iota(I32, (LANES, TQ), 1)

    def head(h, _):
        kq_ref[0, h] = _bias_from_dist(a * LANES + col - row, tab_ref, h) * LOG2E
        qk_ref[0, h] = _bias_from_dist(a * LANES + row - col, tab_ref, h) * LOG2E
        return 0

    lax.fori_loop(0, kq_ref.shape[1], head, 0)


def _bias_tiles(rel_bias):
    nh = rel_bias.shape[1]
    spec = pl.BlockSpec((1, nh, LANES, TQ), lambda a: (a, 0, 0, 0))
    shape = jax.ShapeDtypeStruct((N_BIAS_TILES, nh, LANES, TQ), F32)
    return pl.pallas_call(
        _bias_tiles_kernel,
        grid=(N_BIAS_TILES,),
        in_specs=[pl.BlockSpec(memory_space=pltpu.SMEM)],
        out_specs=[spec, spec],
        out_shape=[shape, shape],
        compiler_params=_cparams("arbitrary"),
        name="bias_tiles",
    )(rel_bias)


def _stack_heads(q):
    z = jnp.zeros((q.shape[0], HD), q.dtype)
    return jnp.concatenate([jnp.concatenate([q[:, r * HD:(r + 1) * HD], z], axis=1) for r in range(REP)],
                           axis=0)


def _tile_heads(x):
    return jnp.concatenate([x] * REP, axis=1)


def _pad_rows(x, n):
    if x.shape[0] == n:
        return x
    return jnp.concatenate([x, jnp.zeros((n - x.shape[0],) + x.shape[1:], x.dtype)], axis=0)


def _bias4(dist, tab_ref, g):
    biases = [jnp.full(dist.shape, tab_ref[0, g * REP + r], F32) for r in range(REP)]
    for kk in range(1, N_BUCKETS):
        reached = dist >= BUCKET_THR[kk - 1]
        biases = [jnp.where(reached, tab_ref[kk, g * REP + r], biases[r]) for r in range(REP)]
    return biases


def _cmp_branch(qc, kcvc_ref, tab_ref, g, pos, pos0, x_scr):
    tq = qc.shape[0]
    nb = kcvc_ref.shape[0]
    kcvc = kcvc_ref[...]
    q4 = _stack_heads(qc)
    lt = _dot_nt(kcvc, q4)
    n_io = lax.broadcasted_iota(I32, (nb, tq), 0)
    dist = pos - (n_io * BLOCK + (BLOCK - 1))
    vis = dist >= 0
    if nb <= CMP_BIAS_ROWS:
        biased = [lt[:, r * tq:(r + 1) * tq] + b for r, b in enumerate(_bias4(dist, tab_ref, g))]
    else:
        far = [tab_ref[N_BUCKETS - 1, g * REP + r] for r in range(REP)]
        x_scr[...] = jnp.concatenate([lt[:, r * tq:(r + 1) * tq] + far[r] for r in range(REP)], axis=1)
        first = (pos0 - (BUCKET_THR[-1] + BLOCK - 1)) // BLOCK + 1
        w0 = pl.multiple_of(jnp.clip(first // 8 * 8, 0, nb - CMP_BIAS_ROWS), 8)
        w_io = lax.broadcasted_iota(I32, (CMP_BIAS_ROWS, tq), 0) + w0
        near = _bias4(pos - (w_io * BLOCK + (BLOCK - 1)), tab_ref, g)
        lt_w = _dot_nt(kcvc_ref[pl.ds(w0, CMP_BIAS_ROWS), :], q4)
        x_scr[pl.ds(w0, CMP_BIAS_ROWS), :] = jnp.concatenate(
            [lt_w[:, r * tq:(r + 1) * tq] + near[r] for r in range(REP)], axis=1)
        biased = [x_scr[:, r * tq:(r + 1) * tq] for r in range(REP)]
    probs = []
    for r in range(REP):
        x = jnp.where(vis, biased[r], NEG)
        e = jnp.exp(x - jnp.max(x, axis=0, keepdims=True))
        p = e / jnp.sum(e, axis=0, keepdims=True)
        probs.append(jnp.where(vis, p, 0.0))
    score = probs[0] + probs[1] + probs[2] + probs[3]
    oc_t = _dot(kcvc.T[HD:, :], jnp.concatenate(probs, axis=1))
    cand = n_io < pos // BLOCK
    return oc_t, jnp.where(cand, score, -jnp.inf)


def _topk_rows(score, n_top, pick_fn):
    nb, tq = score.shape
    n_f = lax.broadcasted_iota(I32, (nb, tq), 0).astype(F32)
    s = score
    for it in range(n_top):
        mx = jnp.max(s, axis=0, keepdims=True)
        idx = jnp.min(jnp.where(s == mx, n_f, float(nb)), axis=0, keepdims=True)
        ok = mx > -jnp.inf
        hit = n_f == idx
        pick_fn(it, idx, ok, hit)
        s = jnp.where(hit, -jnp.inf, s)


def _flash_init(nl):
    return jnp.full((1, nl), NEG, F32), jnp.zeros((VT_ROWS, nl), F32)


def _online_step(carry, tiles):
    m_c, acc_c = carry
    m_new = m_c
    for x2, _ in tiles:
        m_new = jnp.maximum(m_new, jnp.max(x2, axis=0, keepdims=True))
    acc = jnp.exp2(m_c - m_new) * acc_c
    for x2, vt in tiles:
        acc = acc + jnp.dot(vt, jnp.exp2(x2 - m_new).astype(BF16), preferred_element_type=F32)
    return m_new, acc


def _softmax_part(x2, vt):
    m = jnp.max(x2, axis=0, keepdims=True)
    return m, jnp.dot(vt, jnp.exp2(x2 - m).astype(BF16), preferred_element_type=F32)


def _merge_parts(a, b):
    m = jnp.maximum(a[0], b[0])
    return m, jnp.exp2(a[0] - m) * a[1] + jnp.exp2(b[0] - m) * b[1]


def _normalise(acc):
    return acc[:HD] / acc[HD:HD + 1]


def _untranspose_heads(o_t, tq):
    halves = []
    for p in range(REP // 2):
        pair = jnp.concatenate([o_t[:, (2 * p) * tq:(2 * p + 1) * tq],
                                o_t[:, (2 * p + 1) * tq:(2 * p + 2) * tq]], axis=0)
        halves.append(pair.T)
    return jnp.concatenate(halves, axis=1)


def _gate_row(gt_ref, br, g):
    return jnp.concatenate([gt_ref[pl.ds(br * NKV * REP + g * REP + r, 1), :] for r in range(REP)], axis=1)


def _attn_prompt_kernel(tab_ref, qc_ref, qs_ref, qw_ref, gt_ref, kcvc_ref, ksel_ref, vselt_ref,
                        kw0, kw1, kw2, kw3, kw4, vw0, vw1, vw2, vw3, vw4, bt_ref, o_ref, sel_scr, xc_scr):
    g = pl.program_id(0)
    i = pl.program_id(1)
    tq = TQ
    nl = REP * tq
    t0 = i * tq
    pos = t0 + lax.broadcasted_iota(I32, (1, tq), 1)

    oc_t, score = _cmp_branch(qc_ref[...], kcvc_ref, tab_ref, g, pos, t0, xc_scr)
    nb = score.shape[0]
    sel_scr[...] = jnp.full((nb, tq), NEG, F32)

    def pick(it, idx, ok, hit):
        sel_scr[...] = jnp.where(hit, jnp.where(ok, 0.0, sel_scr[...]), sel_scr[...])

    _topk_rows(score, TOPK - 1, pick)

    def bias_tile(a):
        return jnp.concatenate([bt_ref[a, r] for r in range(REP)], axis=1)

    qs_t = (_stack_heads(qs_ref[...]) * LOG2E).T[:HD].astype(BF16)
    pad_rows = jnp.zeros((LANES - HD - SUPER_BLOCKS, nl), BF16)

    def sel_body(step, carry):
        logits = []
        for u in range(SUPERS_PER_STEP):
            jj = step * SUPERS_PER_STEP + u
            mrows = _tile_heads(sel_scr[pl.ds(pl.multiple_of(jj * SUPER_BLOCKS, SUPER_BLOCKS), SUPER_BLOCKS), :])
            q_aug = jnp.concatenate([qs_t, mrows.astype(BF16), pad_rows], axis=0)
            s0 = pl.multiple_of(jj * SUPER, SUPER)
            logits.append((jj, s0, jnp.dot(ksel_ref[pl.ds(s0, SUPER), :], q_aug, preferred_element_type=F32)))
        parts = []
        for jj, s0, x_all in logits:
            for c in range(SUPER // KT):
                tile0 = jj * (SUPER // LANES) + c * (KT // LANES)
                x = x_all[c * KT:(c + 1) * KT] + jnp.concatenate(
                    [bias_tile(jnp.clip(i - tile0 - a, 0, N_BIAS_TILES - 1)) for a in range(KT // LANES)], axis=0)
                parts.append(_softmax_part(x, vselt_ref[0:VT_ROWS, pl.ds(pl.multiple_of(s0 + c * KT, KT), KT)]))
        for part in parts:
            carry = _merge_parts(carry, part)
        return carry

    n_steps = (2 * i) // (SUPER_BLOCKS * SUPERS_PER_STEP) + 1
    carry = lax.fori_loop(0, n_steps, sel_body, _flash_init(nl))

    sj = lax.broadcasted_iota(I32, (LANES, nl), 0)
    ti4 = _tile_heads(lax.broadcasted_iota(I32, (1, tq), 1))
    t0a = pl.multiple_of(t0, LANES)
    q_diag = jnp.concatenate([qs_t, jnp.zeros((LANES - HD, nl), BF16)], axis=0)
    qw_bf = (_stack_heads(qw_ref[...]) * LOG2E).astype(BF16)
    kw_refs = (kw0, kw1, kw2, kw3, kw4)
    vw_refs = (vw0, vw1, vw2, vw3, vw4)
    x_diag = jnp.dot(ksel_ref[pl.ds(t0a, LANES), :], q_diag, preferred_element_type=F32)
    x_win = [_dot_nt(kw_refs[j][...], qw_bf) for j in range(5)]
    x = jnp.where((sj // BLOCK == ti4 // BLOCK) & (sj <= ti4), x_diag + bias_tile(0), NEG)
    _, accs = _online_step(carry, [(x, vselt_ref[0:VT_ROWS, pl.ds(t0a, LANES)])])

    tiles = []
    for j in range(5):
        dist = LANES * (4 - j) + ti4 - sj
        ok = jnp.where((dist >= 0) & (dist <= WINDOW), 1.0, 0.0) * jnp.where(i - 4 + j >= 0, 1.0, 0.0)
        tiles.append((jnp.where(ok > 0.5, x_win[j] + bias_tile(4 - j), NEG), vw_refs[j][0:VT_ROWS, :]))
    _, accw = _online_step(_flash_init(nl), tiles)

    out_t = (_gate_row(gt_ref, 0, g) * oc_t + _gate_row(gt_ref, 1, g) * _normalise(accs)
             + _gate_row(gt_ref, 2, g) * _normalise(accw))
    o_ref[...] = _untranspose_heads(out_t, tq)


def _attn_prompt(rel_bias, qs, gates_t, kcvc, sel_bf, selt_bf, win_bf, wint_bf, bias_tiles):
    t = qs.shape[0]
    nb = kcvc.shape[0]
    width = NKV * REP * HD

    def qspec(br):
        return pl.BlockSpec((TQ, REP * HD), lambda g, i, br=br: (i, br * NKV + g))

    def kw_spec(j):
        return pl.BlockSpec((LANES, LANES), lambda g, i, j=j: (jnp.maximum(i - 4 + j, 0), g))

    def vw_spec(j):
        return pl.BlockSpec((LANES, LANES), lambda g, i, j=j: (g, jnp.maximum(i - 4 + j, 0)))

    return pl.pallas_call(
        _attn_prompt_kernel,
        grid=(NKV, t // TQ),
        in_specs=[pl.BlockSpec(memory_space=pltpu.SMEM),
                  qspec(0), qspec(1), qspec(2),
                  pl.BlockSpec((LANES, TQ), lambda g, i: (0, i)),
                  pl.BlockSpec((nb, LANES), lambda g, i: (0, g)),
                  pl.BlockSpec((t, LANES), lambda g, i: (0, g)),
                  pl.BlockSpec((LANES, t), lambda g, i: (g, 0))]
                 + [kw_spec(j) for j in range(5)] + [vw_spec(j) for j in range(5)]
                 + [pl.BlockSpec((N_BIAS_TILES, REP, LANES, TQ), lambda g, i: (0, g, 0, 0))],
        out_specs=pl.BlockSpec((TQ, REP * HD), lambda g, i: (i, g)),
        out_shape=jax.ShapeDtypeStruct((t, width), F32),
        scratch_shapes=[pltpu.VMEM((nb, TQ), F32), pltpu.VMEM((nb, REP * TQ), F32)],
        compiler_params=_cparams("arbitrary", "arbitrary"),
        name="attn_prompt",
    )(rel_bias, qs, qs, qs, gates_t, kcvc, sel_bf, selt_bf,
      win_bf, win_bf, win_bf, win_bf, win_bf, wint_bf, wint_bf, wint_bf, wint_bf, wint_bf, bias_tiles)


def _attn_s1_kernel(tab_ref, qc_ref, kcvc_ref, oct_ref, idx_ref, xc_scr, *, q0):
    g = pl.program_id(1)
    tq = TQ
    pos = q0 + lax.broadcasted_iota(I32, (1, tq), 1)
    oc_t, score = _cmp_branch(_pad_rows(qc_ref[0], tq), kcvc_ref, tab_ref, g, pos, q0, xc_scr)
    oct_ref[0] = oc_t

    def pick(it, idx, ok, hit):
        idx_ref[0, it:it + 1, :] = jnp.where(ok, idx, -1.0).astype(I32)

    _topk_rows(score, TOPK - 1, pick)
    idx_ref[0, TOPK - 1:TOPK, :] = jnp.full((1, tq), -1, I32)


def _attn_s1(rel_bias, qs8, kcvc, q0):
    batch = qs8.shape[0]
    nb = kcvc.shape[0] // batch
    return pl.pallas_call(
        functools.partial(_attn_s1_kernel, q0=q0),
        grid=(batch, NKV),
        in_specs=[pl.BlockSpec(memory_space=pltpu.SMEM),
                  pl.BlockSpec((1, 8, REP * HD), lambda b, g: (b, 0, g)),
                  pl.BlockSpec((nb, LANES), lambda b, g: (b, g))],
        out_specs=[pl.BlockSpec((1, HD, REP * TQ), lambda b, g: (b * NKV + g, 0, 0)),
                   pl.BlockSpec((1, TOPK, TQ), lambda b, g: (b * NKV + g, 0, 0))],
        out_shape=[jax.ShapeDtypeStruct((batch * NKV, HD, REP * TQ), F32),
                   jax.ShapeDtypeStruct((batch * NKV, TOPK, TQ), I32)],
        scratch_shapes=[pltpu.VMEM((nb, REP * TQ), F32)],
        compiler_params=_cparams("arbitrary", "arbitrary"),
        name="attn_sample_select",
    )(rel_bias, qs8, kcvc)


def _attn_s2_kernel(ids_ref, phys_ref, qs_ref, qw_ref, gt_ref, oct_ref, *refs, q0, nvalid):
    nsel = nvalid * (TOPK - 1)
    kb = refs[:nsel]
    snew_ref, cwin_ref, wnew_ref, bt_ref, o_ref = refs[nsel:]
    b = pl.program_id(0)
    g = pl.program_id(1)
    tq = TQ_SAMPLE
    nr = REP * tq
    tr = lax.broadcasted_iota(I32, (nr, LANES), 0) % tq
    sj = lax.broadcasted_iota(I32, (nr, LANES), 1)
    base = (b * NKV + g) * nvalid * TOPK

    def compact(x):
        return jnp.concatenate([x[:, r * TQ:r * TQ + tq] for r in range(REP)], axis=1)

    def bias_rows(a):
        return jnp.concatenate([bt_ref[a, r][:tq, :] for r in range(REP)], axis=0)

    def attend(q_bf, cached, new_ref):
        xs, vts = [], []
        for tile_t, a, ok in cached:
            xs.append(jnp.where(ok, _dot(q_bf, tile_t) + bias_rows(a), NEG))
            vts.append(tile_t.astype(BF16))
        blk = _pad_rows(new_ref[0], LANES).astype(BF16)
        xs.append(jnp.where((sj <= tr) & (sj < nvalid), _dot_nt(q_bf, blk) + bias_rows(0), NEG))
        x = jnp.concatenate(xs, axis=1)
        p = jnp.exp2(x - jnp.max(x, axis=1, keepdims=True))
        l = jnp.sum(p, axis=1, keepdims=True)
        p = p.astype(BF16)
        nk = x.shape[1] - LANES
        acc = _dot_nt(p[:, :nk], jnp.concatenate(vts, axis=1)) + _dot(p[:, nk:], blk)
        return acc[:, HD:] / l

    cached = []
    for t in range(nvalid):
        for k in range(TOPK - 1):
            n = ids_ref[base + t * TOPK + k]
            nc = jnp.maximum(n, 0)
            a = jnp.clip(q0 // LANES - nc // 2, 0, N_BIAS_TILES - 1)
            lo = jnp.where(n >= 0, (nc % 2) * BLOCK, LANES)
            cached.append((kb[t * (TOPK - 1) + k][0], a, (sj >= lo) & (sj < lo + BLOCK) & (tr == t)))
    o_sel = attend((_stack_heads(_pad_rows(qs_ref[0], tq)) * LOG2E).astype(BF16), cached, snew_ref)

    cached = []
    for j in range(WINDOW // LANES):
        dist = (WINDOW - LANES * j) + tr - sj
        cached.append((cwin_ref[0, :, j * LANES:(j + 1) * LANES], WINDOW // LANES - j,
                       (dist >= 0) & (dist <= WINDOW)))
    o_win = attend((_stack_heads(_pad_rows(qw_ref[0], tq)) * LOG2E).astype(BF16), cached, wnew_ref)

    gt2 = gt_ref.at[0]
    o_cmp = _pad_rows(compact(oct_ref[0]), LANES).T[:, :HD]
    row8 = lax.broadcasted_iota(I32, (8, nr), 0)
    g8 = jnp.zeros((8, nr), F32)
    for br in range(3):
        g8 = jnp.where(row8 == br, compact(_gate_row(gt2, br, g)), g8)
    gates = _pad_rows(g8, LANES).T
    out = gates[:, 0:1] * o_cmp + gates[:, 1:2] * o_sel + gates[:, 2:3] * o_win
    o_ref[0] = jnp.concatenate([out[r * tq:r * tq + 8] for r in range(REP)], axis=1)


def _attn_s2(ids, phys, qs8, gates_t, oct, cache_sel_t, sel_new8, cache_win_t, win_new8, bias_tiles, q0, nvalid):
    batch = qs8.shape[0]
    nsel = nvalid * (TOPK - 1)

    def kb_spec(t, k):
        return pl.BlockSpec((1, LANES, PAGE),
                            lambda b, g, ids_r, phys_r, t=t, k=k:
                            (phys_r[((b * NKV + g) * nvalid + t) * TOPK + k], g, 0))

    def bg(shape, col):
        return pl.BlockSpec(shape, lambda b, g, ids_r, phys_r, col=col: (b, 0, col * NKV + g))

    return pl.pallas_call(
        functools.partial(_attn_s2_kernel, q0=q0, nvalid=nvalid),
        grid_spec=pltpu.PrefetchScalarGridSpec(
            num_scalar_prefetch=2,
            grid=(batch, NKV),
            in_specs=[bg((1, 8, REP * HD), 1), bg((1, 8, REP * HD), 2),
                      pl.BlockSpec((1, LANES, TQ), lambda b, g, ids_r, phys_r: (b, 0, 0)),
                      pl.BlockSpec((1, HD, REP * TQ), lambda b, g, ids_r, phys_r: (b * NKV + g, 0, 0))]
                     + [kb_spec(t, k) for t in range(nvalid) for k in range(TOPK - 1)]
                     + [bg((1, 8, LANES), 0),
                        pl.BlockSpec((1, LANES, WINDOW), lambda b, g, ids_r, phys_r: (b, g, 0)),
                        bg((1, 8, LANES), 0),
                        pl.BlockSpec((N_BIAS_TILES, REP, LANES, TQ), lambda b, g, ids_r, phys_r: (0, g, 0, 0))],
            out_specs=bg((1, 8, REP * HD), 0)),
        out_shape=jax.ShapeDtypeStruct((batch, 8, NKV * REP * HD), F32),
        compiler_params=_cparams("arbitrary", "arbitrary"),
        name="attn_sample",
    )(ids, phys, qs8, qs8, gates_t, oct, *([cache_sel_t] * nsel), sel_new8, cache_win_t, win_new8, bias_tiles)


def _pad_cols(w, n):
    return jnp.pad(w, ((0, 0), (0, n - w.shape[1])))


def _prep(p):
    inner = NH_A * DK_A
    width = NKV * REP * HD
    q = {}
    q['w_in'] = _pad_cols(p['w_in_a'][0], 4 * inner + LANES).astype(BF16)
    q['b_if'] = _pad_cols(p['b_if_a'][0][None, :], LANES)
    q['w_out'] = p['w_out_a'][0].astype(BF16)
    q['w_ff1'] = [p['w_ff1'][l].astype(BF16) for l in range(2)]
    q['w_ff2'] = [p['w_ff2'][l].astype(BF16) for l in range(2)]
    q['w_kv'] = p['w_kv'].astype(BF16)
    q['w_q'] = _pad_cols(p['w_q_b'][0], 3 * width + LANES).astype(BF16)
    q['b_gate'] = _pad_cols(p['b_gate_b'][0][None, :], LANES)
    q['w_o'] = p['w_o_b'][0].astype(BF16)
    q['q_gain'] = (jnp.tile(p['q_norm_b'][0][:, None, :], (1, NKV * REP, 1)) * (HD ** -0.5)).reshape(1, 3 * width)
    seg = np.arange(256) // HD
    q['bd'] = jnp.asarray((seg[:, None] == seg[None, :]).astype(np.float32) / HD, dtype=BF16)
    ones = jnp.ones((HD,), F32)
    q['k_gain'] = jnp.concatenate([jnp.tile(jnp.concatenate([p['k_norm'][br], ones]), NKV) for br in (1, 2)])[None, :]
    q['k_mask'] = jnp.tile(jnp.concatenate([ones, 0.0 * ones]), 2 * NKV)[None, :]
    q['k_gain0'] = jnp.concatenate([p['k_norm'][0], ones])[None, :]
    w1 = p['w_cmp1']
    z = jnp.zeros_like(w1[0])
    per_l = jnp.concatenate([jnp.concatenate([w1[0], z], axis=-1), jnp.concatenate([z, w1[1]], axis=-1)], axis=1)
    q['w_cmp1'] = per_l.reshape(BLOCK // 2, 2 * 2 * HD, 2 * w1.shape[-1]).astype(BF16)
    w2 = p['w_cmp2']
    z2 = jnp.zeros_like(w2[0])
    q['w_cmp2'] = jnp.concatenate([jnp.concatenate([w2[0], z2], axis=1), jnp.concatenate([z2, w2[1]], axis=1)],
                                  axis=0).astype(BF16)
    q['pos_rows'] = jnp.tile(p['cmp_pos'].reshape(BLOCK, 2 * HD), (1, NKV))
    return q


def _feature_major(cache):
    n, t = cache.shape[:2]
    return jnp.transpose(cache, (0, 2, 3, 4, 1)).reshape(n, -1, t)


def _mods(ada_rows, rep):
    k = ada_rows.shape[1] // D_MODEL
    out = []
    for j in range(k):
        a = ada_rows[:, j * D_MODEL:(j + 1) * D_MODEL]
        out.append(jnp.repeat(a, rep, axis=0) if rep > 1 else a)
    return out


def _layer0(x, mods, p, q, mlstm_fn, tm, tmf):
    sh1, sc1, g1, sh2, sc2, g2 = mods
    proj = _proj_plain(x, p['norm_mix'][0][None, :], sh1, sc1, q['w_in'], tm)
    y, states = mlstm_fn(proj)
    x = _out_proj(x, g1, y, q['w_out'], tm)
    x = _ffn(x, p['norm_ffn'][0][None, :], sh2, sc2, g2, q['w_ff1'][0], q['w_ff2'][0], tmf)
    return x, states


def _layer1_tail(x, attn, mods, p, q, tm, tmf):
    _, _, g1, sh2, sc2, g2 = mods
    x = _out_proj(x, g1, attn, q['w_o'], tm)
    return _ffn(x, p['norm_ffn'][1][None, :], sh2, sc2, g2, q['w_ff1'][1], q['w_ff2'][1], tmf)


def kernel(x_prompt, x_sample, cache_cmp, cache_sel, cache_win, state_C, state_n, state_m, page_table,
           c_prompt, c_sample, w_ada, b_ada, norm_mix, norm_ffn, w_ff1, w_ff2, w_in_a, b_if_a,
           head_norm_a, w_out_a, w_ada_kv, b_ada_kv, norm_kv, w_kv, k_norm, cmp_pos, w_cmp1, w_cmp2,
           w_q_b, b_gate_b, q_norm_b, w_o_b, rel_bias):
    p = dict(norm_mix=norm_mix, norm_ffn=norm_ffn, w_ff1=w_ff1, w_ff2=w_ff2, w_in_a=w_in_a, b_if_a=b_if_a,
             w_out_a=w_out_a, w_kv=w_kv, k_norm=k_norm, cmp_pos=cmp_pos, w_cmp1=w_cmp1, w_cmp2=w_cmp2,
             w_q_b=w_q_b, b_gate_b=b_gate_b, q_norm_b=q_norm_b, w_o_b=w_o_b)
    q = _prep(p)
    bp, tp, d = x_prompt.shape
    bs, ts, _ = x_sample.shape
    past = page_table.shape[1] * PAGE
    wbuf = cache_win.shape[1]
    assert bp == 1 and ts < BLOCK and ts <= 8 and wbuf == WINDOW and past % LANES == 0
    assert tp % (SUPER * SUPERS_PER_STEP) == 0
    width = NKV * REP * HD
    kvw = NKV * 2 * HD

    nc = bp + bs
    c_all = jnp.pad(jnp.concatenate([c_prompt, c_sample], axis=0), ((0, -nc % 8), (0, 0)))
    ada = [_ada(c_all, w_ada[l], b_ada[l]) for l in range(2)]
    ada_kv = _ada(c_all, w_ada_kv, b_ada_kv)
    bias_kq, bias_qk = _bias_tiles(rel_bias)
    head_gain = head_norm_a[0]

    xp = x_prompt.reshape(tp, d)
    zc = jnp.zeros((bp, NH_A, DK_A, DK_A), F32)
    zn = jnp.zeros((bp, NH_A, 1, DK_A), F32)
    zm = jnp.zeros((bp, 8, LANES), F32)

    def mlstm_prompt(proj):
        y, c, n, m = _mlstm(proj.reshape(bp, tp, -1), q['b_if'], head_gain, zc, zn, zm,
                            rows=256, valid=256, chunk=256, zero_init=True)
        return y.reshape(tp, -1), (c, n, m)

    xp, (pc, pn, pm) = _layer0(xp, _mods(ada[0][:bp], 1), p, q, mlstm_prompt, 256, 512)
    sh, sc = _mods(ada_kv[:bp], 1)
    p_cmp, _, _, p_cmp_t, p_sel_t, p_win_t, sel_bf, selt_bf, win_bf, wint_bf = _proj_kv(
        xp, norm_kv[None, :], sh, sc, q['w_kv'], q['bd'], q['k_gain'], q['k_mask'], 256)
    kcvc_p = _compress(p_cmp, q['pos_rows'], q['w_cmp1'], q['w_cmp2'], q['k_gain0'],
                       min(CMP_BLOCKS_PER_STEP, tp // BLOCK))
    mods1 = _mods(ada[1][:bp], 1)
    qs_p, gt_p = _proj_q(xp, norm_mix[1][None, :], mods1[0], mods1[1], q['w_q'], q['bd'], q['q_gain'],
                         q['b_gate'], 256)
    attn_p = _attn_prompt(rel_bias, qs_p, gt_p, kcvc_p, sel_bf, selt_bf, win_bf, wint_bf, bias_kq)
    y_prompt = _layer1_tail(xp, attn_p, mods1, p, q, 256, 512).reshape(bp, tp, d)

    ms = bs * ts
    xs = x_sample.reshape(ms, d)
    m0 = jnp.broadcast_to(jnp.pad(state_m[0], ((0, 0), (0, 8 - NH_A)))[:, :, None], (bs, 8, LANES))

    def mlstm_sample(proj):
        proj8 = jnp.pad(proj.reshape(bs, ts, -1), ((0, 0), (0, 8 - ts), (0, 0)))
        y, c, n, m = _mlstm(proj8, q['b_if'], head_gain, state_C[0], state_n[0][:, :, None, :], m0,
                            rows=8, valid=ts, chunk=LANES, zero_init=False)
        return y[:, :ts].reshape(ms, -1), (c, n, m)

    xs, (sc_, sn_, sm_) = _layer0(xs, _mods(ada[0][bp:nc], ts), p, q, mlstm_sample, ms, ms)
    sh, sc = _mods(ada_kv[bp:nc], ts)
    s_cmp, s_sel, s_win = _proj_kv(xs, norm_kv[None, :], sh, sc, q['w_kv'], q['bd'], q['k_gain'],
                                   q['k_mask'], ms)[:3]
    kcvc_s = _compress_paged(_feature_major(cache_cmp), page_table, q['pos_rows'],
                             q['w_cmp1'], q['w_cmp2'], q['k_gain0'], min(CMP_BLOCKS_PER_STEP, past // BLOCK))
    mods1 = _mods(ada[1][bp:nc], ts)
    qs_s, gt_s = _proj_q(xs, norm_mix[1][None, :], mods1[0], mods1[1], q['w_q'], q['bd'], q['q_gain'],
                         q['b_gate'], ms)

    def pad8(a):
        return jnp.pad(a.reshape(bs, ts, -1), ((0, 0), (0, 8 - ts), (0, 0)))

    qs8 = pad8(qs_s)
    oct_s, idx_s = _attn_s1(rel_bias, qs8, kcvc_s, past)
    ids = jnp.transpose(idx_s.reshape(bs, NKV, TOPK, TQ)[:, :, :, :ts], (0, 1, 3, 2))
    idc = jnp.maximum(ids, 0)
    pages = jnp.take_along_axis(page_table, (idc // (PAGE // BLOCK)).reshape(bs, -1), axis=1).reshape(ids.shape)
    gt8 = jnp.pad(jnp.transpose(gt_s.reshape(LANES, bs, ts), (1, 0, 2)), ((0, 0), (0, 0), (0, TQ - ts)))
    attn_s = _attn_s2(ids.reshape(-1), pages.reshape(-1), qs8, gt8, oct_s,
                      _feature_major(cache_sel), pad8(s_sel), _feature_major(cache_win), pad8(s_win),
                      bias_qk, past, ts)
    y_sample = _layer1_tail(xs, attn_s[:, :ts].reshape(ms, width), mods1, p, q, ms, ms).reshape(bs, ts, d)

    rows5 = (NKV, 2, HD)

    def token_major(a_t):
        return jnp.transpose(a_t.reshape(*rows5, a_t.shape[1]), (3, 0, 1, 2))[None]

    s_win_all = jnp.concatenate([cache_win, s_win.reshape(bs, ts, *rows5)], axis=1)
    return (y_prompt, y_sample,
            pc[None], pn.reshape(1, bp, NH_A, DK_A), pm[None, :, :NH_A, 0],
            token_major(p_cmp_t), token_major(p_sel_t), token_major(p_win_t[:, tp - min(WINDOW, tp):]),
            sc_[None], sn_.reshape(1, bs, NH_A, DK_A), sm_[None, :, :NH_A, 0],
            s_cmp.reshape(bs, ts, *rows5), s_sel.reshape(bs, ts, *rows5),
            s_win_all[:, -min(WINDOW, wbuf + ts):])
```

```python
import functools
import math

import numpy as np
import jax
import jax.numpy as jnp
from jax import lax
from jax.experimental import pallas as pl
from jax.experimental.pallas import tpu as pltpu

F32 = jnp.float32
BF16 = jnp.bfloat16
I32 = jnp.int32

D_MODEL = 1024
NH_A = 4
DK_A = 256
NKV = 4
REP = 4
HD = 64
BLOCK = 64
TOPK = 16
WINDOW = 512
N_BUCKETS = 32
REL_MAX_DIST = 2048
PAGE = 128
EPS = 1e-6
NEG = -1e30
LOG2E = math.log2(math.e)

LANES = 128
VMEM_LIMIT = 56 * 1024 * 1024
TQ = 128
TQ_SAMPLE = 32
KT = 256
SUPER_BLOCKS = 16
SUPER = SUPER_BLOCKS * BLOCK
SUPERS_PER_STEP = 2
VT_ROWS = HD + 16
CONST_LANE = HD + SUPER_BLOCKS
N_BIAS_TILES = 14
CMP_BLOCKS_PER_STEP = 64
CMP_BIAS_ROWS = 40


def _bucket_thresholds():
    exact = N_BUCKETS // 2
    d = np.arange(1, 4 * REL_MAX_DIST, dtype=np.float64)
    big = exact + np.floor(np.log(d / exact) / math.log(REL_MAX_DIST / exact) * (N_BUCKETS - exact)).astype(np.int64)
    b = np.where(d < exact, d.astype(np.int64), np.minimum(big, N_BUCKETS - 1))
    return [int(d[np.argmax(b >= k)]) for k in range(1, N_BUCKETS)]


BUCKET_THR = _bucket_thresholds()
assert 128 * (N_BIAS_TILES - 1) - (TQ - 1) >= BUCKET_THR[-1]


def _cparams(*sem):
    return pltpu.CompilerParams(dimension_semantics=sem, vmem_limit_bytes=VMEM_LIMIT)


def _dot(a, b):
    return jnp.dot(a.astype(BF16), b.astype(BF16), preferred_element_type=F32)


def _dot_nt(a, b):
    return lax.dot_general(a.astype(BF16), b.astype(BF16), (((1,), (1,)), ((), ())),
                           preferred_element_type=F32)


def _split3(x):
    hi = x.astype(BF16)
    r1 = x - hi.astype(F32)
    mid = r1.astype(BF16)
    lo = (r1 - mid.astype(F32)).astype(BF16)
    return hi, mid, lo


def _sigmoid(x):
    return 1.0 / (1.0 + jnp.exp(-x))


def _norm_mod(x, gain, shift, scale):
    ms = jnp.mean(x * x, axis=-1, keepdims=True)
    y = x * lax.rsqrt(ms + EPS) * gain
    return y * (1.0 + scale) + shift


def _seg_mean_sq(y, bd):
    parts = []
    for j in range(y.shape[1] // 256):
        sq = y[:, j * 256:(j + 1) * 256]
        sq = sq * sq
        hi = sq.astype(BF16)
        lo = (sq - hi.astype(F32)).astype(BF16)
        parts.append(jnp.dot(hi, bd, preferred_element_type=F32) + jnp.dot(lo, bd, preferred_element_type=F32))
    return parts[0] if len(parts) == 1 else jnp.concatenate(parts, axis=1)


def _row_spec(tm, n, per_row):
    if per_row:
        return pl.BlockSpec((tm, n), lambda i: (i, 0))
    return pl.BlockSpec((1, n), lambda i: (0, 0))


def _const_spec(shape):
    return pl.BlockSpec(shape, lambda i: tuple(0 for _ in shape))


def _ada_kernel(c_ref, w_ref, b_ref, o_ref):
    c = c_ref[...]
    o_ref[...] = _dot(c * _sigmoid(c), w_ref[...]) + b_ref[...]


def _ada(c, w, b):
    m, k = c.shape
    n = w.shape[1]
    tn = 1024
    return pl.pallas_call(
        _ada_kernel,
        grid=(n // tn,),
        in_specs=[pl.BlockSpec((m, k), lambda j: (0, 0)),
                  pl.BlockSpec((k, tn), lambda j: (0, j)),
                  pl.BlockSpec((1, tn), lambda j: (0, j))],
        out_specs=pl.BlockSpec((m, tn), lambda j: (0, j)),
        out_shape=jax.ShapeDtypeStruct((m, n), F32),
        compiler_params=_cparams("arbitrary"),
        name="ada",
    )(c, w, b.reshape(1, n))


def _proj_plain_kernel(x_ref, gain_ref, sh_ref, sc_ref, w_ref, o_ref):
    xn = _norm_mod(x_ref[...], gain_ref[...], sh_ref[...], sc_ref[...])
    o_ref[...] = jnp.dot(xn.astype(BF16), w_ref[...], preferred_element_type=F32)


def _proj_plain(x, gain, shift, scale, w_bf, tm):
    m, d = x.shape
    n = w_bf.shape[1]
    per_row = shift.shape[0] != 1
    return pl.pallas_call(
        _proj_plain_kernel,
        grid=(m // tm,),
        in_specs=[pl.BlockSpec((tm, d), lambda i: (i, 0)),
                  _const_spec((1, d)),
                  _row_spec(tm, d, per_row), _row_spec(tm, d, per_row),
                  _const_spec((d, n))],
        out_specs=pl.BlockSpec((tm, n), lambda i: (i, 0)),
        out_shape=jax.ShapeDtypeStruct((m, n), F32),
        compiler_params=_cparams("arbitrary"),
        name="proj_mlstm",
    )(x, gain, shift, scale, w_bf)


def _proj_q_kernel(x_ref, gain_ref, sh_ref, sc_ref, w_ref, bd_ref, qg_ref, bg_ref, q_ref, gt_ref):
    xn = _norm_mod(x_ref[...], gain_ref[...], sh_ref[...], sc_ref[...])
    y = jnp.dot(xn.astype(BF16), w_ref[...], preferred_element_type=F32)
    nq = q_ref.shape[1]
    yq = y[:, :nq]
    ms = _seg_mean_sq(yq, bd_ref[...])
    q_ref[...] = yq * lax.rsqrt(ms + EPS) * qg_ref[...]
    gates = _sigmoid(y[:, nq:] + bg_ref[...])
    gt_ref[...] = gates.T


def _proj_q(x, gain, shift, scale, w_bf, bd, qgain_row, bgate_row, tm):
    m, d = x.shape
    n = w_bf.shape[1]
    nq = n - LANES
    per_row = shift.shape[0] != 1
    return pl.pallas_call(
        _proj_q_kernel,
        grid=(m // tm,),
        in_specs=[pl.BlockSpec((tm, d), lambda i: (i, 0)),
                  _const_spec((1, d)),
                  _row_spec(tm, d, per_row), _row_spec(tm, d, per_row),
                  _const_spec((d, n)), _const_spec((256, 256)),
                  _const_spec((1, nq)), _const_spec((1, LANES))],
        out_specs=[pl.BlockSpec((tm, nq), lambda i: (i, 0)),
                   pl.BlockSpec((LANES, tm), lambda i: (0, i))],
        out_shape=[jax.ShapeDtypeStruct((m, nq), F32),
                   jax.ShapeDtypeStruct((LANES, m), F32)],
        compiler_params=_cparams("arbitrary"),
        name="proj_q",
    )(x, gain, shift, scale, w_bf, bd, qgain_row, bgate_row)


def _proj_kv_kernel(x_ref, gain_ref, sh_ref, sc_ref, w_ref, bd_ref, kg_ref, km_ref,
                    cmp_ref, sel_ref, win_ref, cmpt_ref, selt_ref, wint_ref,
                    selb_ref, seltb_ref, winb_ref, wintb_ref):
    xn = _norm_mod(x_ref[...], gain_ref[...], sh_ref[...], sc_ref[...])
    y = jnp.dot(xn.astype(BF16), w_ref[...], preferred_element_type=F32)
    w = cmp_ref.shape[1]
    cmp = y[:, :w]
    ykn = y[:, w:]
    ms = _seg_mean_sq(ykn, bd_ref[...])
    ykn = jnp.where(km_ref[...] > 0.5, ykn * lax.rsqrt(ms + EPS) * kg_ref[...], ykn)
    sel = ykn[:, :w]
    win = ykn[:, w:]
    sel_t = sel.T
    win_t = win.T
    cmp_ref[...] = cmp
    sel_ref[...] = sel
    win_ref[...] = win
    cmpt_ref[...] = cmp.T
    selt_ref[...] = sel_t
    wint_ref[...] = win_t
    tm = sel.shape[0]
    tok = pl.program_id(0) * tm + lax.broadcasted_iota(I32, sel.shape, 0)
    lane = lax.broadcasted_iota(I32, sel.shape, 1) % LANES
    onehot = jnp.where((lane - HD == (tok // BLOCK) % SUPER_BLOCKS) | (lane == CONST_LANE)
                       | (lane == CONST_LANE + 1), 1.0, 0.0)
    selb_ref[...] = jnp.where(lane < HD, sel, onehot).astype(BF16)
    winb_ref[...] = win.astype(BF16)

    def value_rows(x_t):
        ones_blk = jnp.where(lax.broadcasted_iota(I32, (HD, tm), 0) == 0, 1.0, 0.0)
        return jnp.concatenate(
            [piece for gg in range(NKV) for piece in (x_t[gg * LANES + HD:(gg + 1) * LANES], ones_blk)],
            axis=0).astype(BF16)

    seltb_ref[...] = value_rows(sel_t)
    wintb_ref[...] = value_rows(win_t)


def _proj_kv(x, gain, shift, scale, w_bf, bd, kgain_row, kmask_row, tm):
    m, d = x.shape
    n = w_bf.shape[1]
    w = n // 3
    per_row = shift.shape[0] != 1
    row = pl.BlockSpec((tm, w), lambda i: (i, 0))
    col = pl.BlockSpec((w, tm), lambda i: (0, i))
    return pl.pallas_call(
        _proj_kv_kernel,
        grid=(m // tm,),
        in_specs=[pl.BlockSpec((tm, d), lambda i: (i, 0)),
                  _const_spec((1, d)),
                  _row_spec(tm, d, per_row), _row_spec(tm, d, per_row),
                  _const_spec((d, n)), _const_spec((256, 256)),
                  _const_spec((1, 2 * w)), _const_spec((1, 2 * w))],
        out_specs=[row, row, row, col, col, col, row, col, row, col],
        out_shape=[jax.ShapeDtypeStruct((m, w), F32)] * 3 + [jax.ShapeDtypeStruct((w, m), F32)] * 3
                  + [jax.ShapeDtypeStruct((m, w), BF16), jax.ShapeDtypeStruct((w, m), BF16),
                     jax.ShapeDtypeStruct((m, w), BF16), jax.ShapeDtypeStruct((w, m), BF16)],
        compiler_params=_cparams("arbitrary"),
        name="proj_kv",
    )(x, gain, shift, scale, w_bf, bd, kgain_row, kmask_row)


def _out_proj_kernel(x_ref, g_ref, y_ref, w_ref, o_ref):
    o_ref[...] = x_ref[...] + g_ref[...] * jnp.dot(y_ref[...].astype(BF16), w_ref[...],
                                                   preferred_element_type=F32)


def _out_proj(x, g, y, w_bf, tm):
    m, d = x.shape
    k = y.shape[1]
    per_row = g.shape[0] != 1
    return pl.pallas_call(
        _out_proj_kernel,
        grid=(m // tm,),
        in_specs=[pl.BlockSpec((tm, d), lambda i: (i, 0)),
                  _row_spec(tm, d, per_row),
                  pl.BlockSpec((tm, k), lambda i: (i, 0)),
                  _const_spec((k, d))],
        out_specs=pl.BlockSpec((tm, d), lambda i: (i, 0)),
        out_shape=jax.ShapeDtypeStruct((m, d), F32),
        compiler_params=_cparams("arbitrary"),
        name="out_proj",
    )(x, g, y, w_bf)


def _ffn_kernel(x_ref, gain_ref, sh_ref, sc_ref, g_ref, w1_ref, w2_ref, o_ref, *, fc):
    x = x_ref[...]
    xn = _norm_mod(x, gain_ref[...], sh_ref[...], sc_ref[...]).astype(BF16)
    acc = jnp.zeros(x.shape, F32)
    for c in range(w1_ref.shape[1] // fc):
        h = jnp.dot(xn, w1_ref[:, c * fc:(c + 1) * fc], preferred_element_type=F32)
        h = jnp.maximum(h, 0.0)
        acc = acc + jnp.dot((h * h).astype(BF16), w2_ref[c * fc:(c + 1) * fc, :],
                            preferred_element_type=F32)
    o_ref[...] = x + g_ref[...] * acc


def _ffn(x, gain, shift, scale, g, w1_bf, w2_bf, tm):
    m, d = x.shape
    f = w1_bf.shape[1]
    per_row = shift.shape[0] != 1
    return pl.pallas_call(
        functools.partial(_ffn_kernel, fc=1024),
        grid=(m // tm,),
        in_specs=[pl.BlockSpec((tm, d), lambda i: (i, 0)),
                  _const_spec((1, d)),
                  _row_spec(tm, d, per_row), _row_spec(tm, d, per_row), _row_spec(tm, d, per_row),
                  _const_spec((d, f)), _const_spec((f, d))],
        out_specs=pl.BlockSpec((tm, d), lambda i: (i, 0)),
        out_shape=jax.ShapeDtypeStruct((m, d), F32),
        compiler_params=_cparams("arbitrary"),
        name="ffn",
    )(x, gain, shift, scale, g, w1_bf, w2_bf)


def _mlstm_kernel(q_ref, k_ref, v_ref, o_ref, gt_ref, bif_ref, hg_ref, c0_ref, n0_ref, m0_ref,
                  y_ref, cout_ref, nout_ref, mout_ref, c_scr, n_scr, m_scr, *, rows, valid, chunk, zero_init):
    ci = pl.program_id(1)
    nci = pl.num_programs(1)

    @pl.when(ci == 0)
    def _():
        if zero_init:
            c_scr[...] = jnp.zeros(c_scr.shape, F32)
            n_scr[...] = jnp.zeros(n_scr.shape, F32)
            m_scr[...] = jnp.zeros(m_scr.shape, F32)
        else:
            c_scr[...] = c0_ref[0]
            n_scr[...] = n0_ref[0]
            m_scr[...] = m0_ref[0]

    L = chunk

    def padded(ref_val, fill):
        if rows == L:
            return ref_val
        pad = jnp.full((L - rows, ref_val.shape[1]), fill, ref_val.dtype)
        return jnp.concatenate([ref_val, pad], axis=0)

    q = padded(q_ref[0], 0.0)
    k = padded(k_ref[0], 0.0)
    v = padded(v_ref[0], 0.0)
    og = padded(o_ref[0], 0.0)
    g = padded(gt_ref[0], 0.0) + bif_ref[...]
    lf = -(jnp.maximum(-g, 0.0) + jnp.log1p(jnp.exp(-jnp.abs(g))))
    li = g
    if valid != L:
        is_real = lax.broadcasted_iota(I32, (L, LANES), 0) < valid
        lf = jnp.where(is_real, lf, 0.0)
        li = jnp.where(is_real, li, NEG)
    r_io = lax.broadcasted_iota(I32, (L, L), 0)
    c_io = lax.broadcasted_iota(I32, (L, L), 1)
    causal = c_io <= r_io
    tril = jnp.where(causal, 1.0, 0.0).astype(BF16)
    hi, mid, lo = _split3(lf)
    b = (jnp.dot(tril, hi, preferred_element_type=F32) + jnp.dot(tril, mid, preferred_element_type=F32)
         + jnp.dot(tril, lo, preferred_element_type=F32))
    lane = lax.broadcasted_iota(I32, (L, LANES), 1)
    mixed_t = jnp.where(lane < NH_A, li, b).T
    for h in range(NH_A):
        sl = slice(h * DK_A, (h + 1) * DK_A)
        qh = q[:, sl] * (DK_A ** -0.5)
        kh = k[:, sl]
        vh = v[:, sl]
        b_col = b[:, NH_A + h:NH_A + h + 1]
        li_col = li[:, h:h + 1]
        b_row = mixed_t[NH_A + h:NH_A + h + 1, :]
        li_row = mixed_t[h:h + 1, :]
        m_prev = m_scr[h:h + 1, 0:1]
        c_prev = c_scr[h]
        n_prev = n_scr[h]
        dmat = jnp.where(causal, b_col - b_row + li_row, NEG)
        m_inter = b_col + m_prev
        m_t = jnp.maximum(m_inter, jnp.max(dmat, axis=-1, keepdims=True))
        s = _dot_nt(qh, kh) * jnp.exp(dmat - m_t)
        a_inter = jnp.exp(m_inter - m_t)
        num = _dot(s, vh) + a_inter * _dot(qh, c_prev)
        den = jnp.sum(s, axis=-1, keepdims=True) + a_inter * jnp.sum(qh * n_prev, axis=-1, keepdims=True)
        hh = num / jnp.maximum(jnp.abs(den), jnp.exp(-m_t))
        b_last = b_col[L - 1:L, :]
        g_col = b_last - b_col + li_col
        m_new = jnp.maximum(b_last + m_prev, jnp.max(g_col, axis=0, keepdims=True))
        w_col = jnp.exp(g_col - m_new)
        decay = jnp.exp(b_last + m_prev - m_new)
        kw = kh * w_col
        c_scr[h] = decay * c_prev + _dot(kw.T, vh)
        n_scr[h] = decay * n_prev + jnp.sum(kw, axis=0, keepdims=True)
        m_scr[h:h + 1, :] = jnp.broadcast_to(m_new, (1, LANES))
        hn = hh * lax.rsqrt(jnp.mean(hh * hh, axis=-1, keepdims=True) + EPS) * hg_ref[h:h + 1, :]
        yh = hn * _sigmoid(og[:, sl])
        y_ref[0, :, sl] = yh[:rows]

    @pl.when(ci == nci - 1)
    def _():
        cout_ref[0] = c_scr[...]
        nout_ref[0] = n_scr[...]
        mout_ref[0] = m_scr[...]


def _mlstm(proj, b_if_row, head_gain, c0, n0, m0, *, rows, valid, chunk, zero_init):
    batch, t, _ = proj.shape
    nch = t // rows
    inner = NH_A * DK_A

    def colblk(j, width):
        return pl.BlockSpec((1, rows, width), lambda b, c, j=j: (b, c, j))

    state_c = pl.BlockSpec((1, NH_A, DK_A, DK_A), lambda b, c: (b, 0, 0, 0))
    state_n = pl.BlockSpec((1, NH_A, 1, DK_A), lambda b, c: (b, 0, 0, 0))
    state_m = pl.BlockSpec((1, 8, LANES), lambda b, c: (b, 0, 0))
    return pl.pallas_call(
        functools.partial(_mlstm_kernel, rows=rows, valid=valid, chunk=chunk, zero_init=zero_init),
        grid=(batch, nch),
        in_specs=[colblk(0, inner), colblk(1, inner), colblk(2, inner), colblk(3, inner),
                  colblk(4 * inner // LANES, LANES),
                  pl.BlockSpec((1, LANES), lambda b, c: (0, 0)),
                  pl.BlockSpec((NH_A, DK_A), lambda b, c: (0, 0)),
                  state_c, state_n, state_m],
        out_specs=[pl.BlockSpec((1, rows, inner), lambda b, c: (b, c, 0)), state_c, state_n, state_m],
        out_shape=[jax.ShapeDtypeStruct((batch, t, inner), F32),
                   jax.ShapeDtypeStruct((batch, NH_A, DK_A, DK_A), F32),
                   jax.ShapeDtypeStruct((batch, NH_A, 1, DK_A), F32),
                   jax.ShapeDtypeStruct((batch, 8, LANES), F32)],
        scratch_shapes=[pltpu.VMEM((NH_A, DK_A, DK_A), F32), pltpu.VMEM((NH_A, 1, DK_A), F32),
                        pltpu.VMEM((8, LANES), F32)],
        compiler_params=_cparams("arbitrary", "arbitrary"),
        name="mlstm",
    )(proj, proj, proj, proj, proj, b_if_row, head_gain, c0, n0, m0)


def _compress_body(r_refs, pos_ref, w1_ref, w2_ref, kg_ref, o_ref, nb):
    half = BLOCK // 2
    acc = None
    for lp in range(half):
        pieces = []
        for g in range(NKV):
            xa = r_refs[g][pl.ds(lp, nb, stride=BLOCK), :]
            xb = r_refs[g][pl.ds(lp + half, nb, stride=BLOCK), :]
            pieces.append(jnp.concatenate([xa, xb], axis=1))
        pa = pos_ref[lp:lp + 1, 0:LANES]
        pb = pos_ref[lp + half:lp + half + 1, 0:LANES]
        pieces.append(jnp.broadcast_to(jnp.concatenate([pa, pb], axis=1), (8, 2 * LANES)))
        x = jnp.concatenate(pieces, axis=0).astype(BF16)
        d = jnp.dot(x, w1_ref[lp], preferred_element_type=F32)
        acc = d if acc is None else acc + d
    pre = acc[:NKV * nb] + acc[NKV * nb:NKV * nb + 1]
    hid = pre * _sigmoid(pre)
    out = jnp.dot(hid.astype(BF16), w2_ref[...], preferred_element_type=F32)
    is_k = lax.broadcasted_iota(I32, out.shape, 1) < HD
    ms = jnp.sum(jnp.where(is_k, out * out, 0.0), axis=-1, keepdims=True) * (1.0 / HD)
    out = jnp.where(is_k, out * lax.rsqrt(ms + EPS) * kg_ref[...], out)
    for g in range(NKV):
        o_ref[:, g * LANES:(g + 1) * LANES] = out[g * nb:(g + 1) * nb]


def _compress_kernel(r0, r1, r2, r3, pos_ref, w1_ref, w2_ref, kg_ref, o_ref, *, nb):
    _compress_body((r0, r1, r2, r3), pos_ref, w1_ref, w2_ref, kg_ref, o_ref, nb)


def _compress_paged_kernel(pt_ref, *refs, nb):
    npages = nb * BLOCK // PAGE
    pages = refs[:npages]
    pos_ref, w1_ref, w2_ref, kg_ref, o_ref = refs[npages:npages + 5]
    r_scrs = refs[npages + 5:]
    for p in range(npages):
        for g in range(NKV):
            r_scrs[g][p * PAGE:(p + 1) * PAGE, :] = pages[p][0, g * LANES:(g + 1) * LANES, :].T
    _compress_body(r_scrs, pos_ref, w1_ref, w2_ref, kg_ref, o_ref, nb)


def _compress_paged(cache_t, page_table, pos_rows, w1p, w2p, kgain_row, nb):
    batch, npg = page_table.shape
    w = cache_t.shape[1]
    npages = nb * BLOCK // PAGE
    steps = npg // npages

    def page_spec(p):
        return pl.BlockSpec((1, w, PAGE), lambda b, j, pt, p=p: (pt[b, j * npages + p], 0, 0))

    def const(shape):
        return pl.BlockSpec(shape, lambda b, j, pt: tuple(0 for _ in shape))

    return pl.pallas_call(
        functools.partial(_compress_paged_kernel, nb=nb),
        grid_spec=pltpu.PrefetchScalarGridSpec(
            num_scalar_prefetch=1,
            grid=(batch, steps),
            in_specs=[page_spec(p) for p in range(npages)]
                     + [const((BLOCK, w)), const(w1p.shape), const(w2p.shape), const((1, LANES))],
            out_specs=pl.BlockSpec((nb, w), lambda b, j, pt: (b * steps + j, 0)),
            scratch_shapes=[pltpu.VMEM((nb * BLOCK, LANES), F32) for _ in range(NKV)]),
        out_shape=jax.ShapeDtypeStruct((batch * npg * PAGE // BLOCK, w), F32),
        compiler_params=_cparams("arbitrary", "arbitrary"),
        name="compress_paged",
    )(page_table, *([cache_t] * npages), pos_rows, w1p, w2p, kgain_row)


def _compress(rows, pos_rows, w1p, w2p, kgain_row, nb):
    t, w = rows.shape
    nblk = t // BLOCK
    return pl.pallas_call(
        functools.partial(_compress_kernel, nb=nb),
        grid=(nblk // nb,),
        in_specs=[pl.BlockSpec((nb * BLOCK, LANES), lambda i, g=g: (i, g)) for g in range(NKV)]
                 + [_const_spec((BLOCK, w)), _const_spec(w1p.shape), _const_spec(w2p.shape),
                    _const_spec((1, LANES))],
        out_specs=pl.BlockSpec((nb, w), lambda i: (i, 0)),
        out_shape=jax.ShapeDtypeStruct((nblk, w), F32),
        compiler_params=_cparams("arbitrary"),
        name="compress",
    )(rows, rows, rows, rows, pos_rows, w1p, w2p, kgain_row)


def _bias_from_dist(dist, tab_ref, h):
    out = jnp.full(dist.shape, tab_ref[0, h], F32)
    for kk in range(1, N_BUCKETS):
        out = jnp.where(dist >= BUCKET_THR[kk - 1], tab_ref[kk, h], out)
    return out


def _bias_tiles_kernel(tab_ref, kq_ref, kqd_ref, qk_ref):
    a = pl.program_id(0)
    row = lax.broadcasted_iota(I32, (LANES, TQ), 0)
    col = lax.broadcasted_iota(I32, (LANES, TQ), 1)

    def head(h, _):
        kq = _bias_from_dist(a * LANES + col - row, tab_ref, h)
        kq_ref[0, h] = kq * LOG2E
        kqd_ref[0, h] = (kq - tab_ref[N_BUCKETS - 1, h]) * LOG2E
        qk_ref[0, h] = _bias_from_dist(a * LANES + row - col, tab_ref, h) * LOG2E
        return 0

    lax.fori_loop(0, kq_ref.shape[1], head, 0)


def _bias_tiles(rel_bias):
    nh = rel_bias.shape[1]
    spec = pl.BlockSpec((1, nh, LANES, TQ), lambda a: (a, 0, 0, 0))
    shape = jax.ShapeDtypeStruct((N_BIAS_TILES, nh, LANES, TQ), F32)
    return pl.pallas_call(
        _bias_tiles_kernel,
        grid=(N_BIAS_TILES,),
        in_specs=[pl.BlockSpec(memory_space=pltpu.SMEM)],
        out_specs=[spec, spec, spec],
        out_shape=[shape, shape, shape],
        compiler_params=_cparams("arbitrary"),
        name="bias_tiles",
    )(rel_bias)


def _stack_heads(q):
    z = jnp.zeros((q.shape[0], HD), q.dtype)
    return jnp.concatenate([jnp.concatenate([q[:, r * HD:(r + 1) * HD], z], axis=1) for r in range(REP)],
                           axis=0)


def _tile_heads(x):
    return jnp.concatenate([x] * REP, axis=1)


def _pad_rows(x, n):
    if x.shape[0] == n:
        return x
    return jnp.concatenate([x, jnp.zeros((n - x.shape[0],) + x.shape[1:], x.dtype)], axis=0)


def _bias4(dist, tab_ref, g):
    biases = [jnp.full(dist.shape, tab_ref[0, g * REP + r], F32) for r in range(REP)]
    for kk in range(1, N_BUCKETS):
        reached = dist >= BUCKET_THR[kk - 1]
        biases = [jnp.where(reached, tab_ref[kk, g * REP + r], biases[r]) for r in range(REP)]
    return biases


def _cmp_branch(qc, kcvc_ref, tab_ref, g, pos, pos0, x_scr):
    tq = qc.shape[0]
    nb = kcvc_ref.shape[0]
    kcvc = kcvc_ref[...]
    q4 = _stack_heads(qc)
    lt = _dot_nt(kcvc, q4)
    n_io = lax.broadcasted_iota(I32, (nb, tq), 0)
    dist = pos - (n_io * BLOCK + (BLOCK - 1))
    vis = dist >= 0
    if nb <= CMP_BIAS_ROWS:
        biased = [lt[:, r * tq:(r + 1) * tq] + b for r, b in enumerate(_bias4(dist, tab_ref, g))]
    else:
        far = [tab_ref[N_BUCKETS - 1, g * REP + r] for r in range(REP)]
        x_scr[...] = jnp.concatenate([lt[:, r * tq:(r + 1) * tq] + far[r] for r in range(REP)], axis=1)
        first = (pos0 - (BUCKET_THR[-1] + BLOCK - 1)) // BLOCK + 1
        w0 = pl.multiple_of(jnp.clip(first // 8 * 8, 0, nb - CMP_BIAS_ROWS), 8)
        w_io = lax.broadcasted_iota(I32, (CMP_BIAS_ROWS, tq), 0) + w0
        near = _bias4(pos - (w_io * BLOCK + (BLOCK - 1)), tab_ref, g)
        lt_w = _dot_nt(kcvc_ref[pl.ds(w0, CMP_BIAS_ROWS), :], q4)
        x_scr[pl.ds(w0, CMP_BIAS_ROWS), :] = jnp.concatenate(
            [lt_w[:, r * tq:(r + 1) * tq] + near[r] for r in range(REP)], axis=1)
        biased = [x_scr[:, r * tq:(r + 1) * tq] for r in range(REP)]
    probs = []
    for r in range(REP):
        x = jnp.where(vis, biased[r], NEG)
        e = jnp.exp(x - jnp.max(x, axis=0, keepdims=True))
        p = e / jnp.sum(e, axis=0, keepdims=True)
        probs.append(jnp.where(vis, p, 0.0))
    score = probs[0] + probs[1] + probs[2] + probs[3]
    oc_t = _dot(kcvc.T[HD:, :], jnp.concatenate(probs, axis=1))
    cand = n_io < pos // BLOCK
    return oc_t, jnp.where(cand, score, -jnp.inf)


def _topk_rows(score, n_top, pick_fn):
    nb, tq = score.shape
    n_f = lax.broadcasted_iota(I32, (nb, tq), 0).astype(F32)
    s = score
    for it in range(n_top):
        mx = jnp.max(s, axis=0, keepdims=True)
        idx = jnp.min(jnp.where(s == mx, n_f, float(nb)), axis=0, keepdims=True)
        ok = mx > -jnp.inf
        hit = n_f == idx
        pick_fn(it, idx, ok, hit)
        s = jnp.where(hit, -jnp.inf, s)


def _flash_init(nl):
    return jnp.full((1, nl), NEG, F32), jnp.zeros((VT_ROWS, nl), F32)


def _online_step(carry, tiles):
    m_c, acc_c = carry
    m_new = m_c
    for x2, _ in tiles:
        m_new = jnp.maximum(m_new, jnp.max(x2, axis=0, keepdims=True))
    acc = jnp.exp2(m_c - m_new) * acc_c
    for x2, vt in tiles:
        acc = acc + jnp.dot(vt, jnp.exp2(x2 - m_new).astype(BF16), preferred_element_type=F32)
    return m_new, acc


def _softmax_part(x2, vt):
    m = jnp.max(x2, axis=0, keepdims=True)
    return m, jnp.dot(vt, jnp.exp2(x2 - m).astype(BF16), preferred_element_type=F32)


def _merge_parts(a, b):
    m = jnp.maximum(a[0], b[0])
    return m, jnp.exp2(a[0] - m) * a[1] + jnp.exp2(b[0] - m) * b[1]


def _normalise(acc):
    return acc[:HD] / acc[HD:HD + 1]


def _untranspose_heads(o_t, tq):
    halves = []
    for p in range(REP // 2):
        pair = jnp.concatenate([o_t[:, (2 * p) * tq:(2 * p + 1) * tq],
                                o_t[:, (2 * p + 1) * tq:(2 * p + 2) * tq]], axis=0)
        halves.append(pair.T)
    return jnp.concatenate(halves, axis=1)


def _gate_row(gt_ref, br, g):
    return jnp.concatenate([gt_ref[pl.ds(br * NKV * REP + g * REP + r, 1), :] for r in range(REP)], axis=1)


def _attn_prompt_kernel(tab_ref, qc_ref, qs_ref, qw_ref, gt_ref, kcvc_ref, ksel_ref, vselt_ref,
                        kw0, kw1, kw2, kw3, kw4, vw0, vw1, vw2, vw3, vw4, bt_ref, btd_ref, o_ref,
                        sel_scr, xc_scr):
    g = pl.program_id(0)
    i = pl.program_id(1)
    tq = TQ
    nl = REP * tq
    t0 = i * tq
    pos = t0 + lax.broadcasted_iota(I32, (1, tq), 1)

    oc_t, score = _cmp_branch(qc_ref[...], kcvc_ref, tab_ref, g, pos, t0, xc_scr)
    nb = score.shape[0]
    sel_scr[...] = jnp.full((nb, tq), NEG, F32)

    def pick(it, idx, ok, hit):
        sel_scr[...] = jnp.where(hit, jnp.where(ok, 0.0, sel_scr[...]), sel_scr[...])

    _topk_rows(score, TOPK - 1, pick)

    def bias_tile(a):
        return jnp.concatenate([bt_ref[a, r] for r in range(REP)], axis=1)

    qs_t = (_stack_heads(qs_ref[...]) * LOG2E).T[:HD].astype(BF16)
    far_bias = jnp.concatenate([jnp.full((1, tq), tab_ref[N_BUCKETS - 1, g * REP + r] * LOG2E, F32)
                                for r in range(REP)], axis=1)
    far_hi = far_bias.astype(BF16).astype(F32)
    row16 = lax.broadcasted_iota(I32, (SUPER_BLOCKS, nl), 0)
    const_rows = jnp.where(row16 == 0, far_hi, jnp.where(row16 == 1, far_bias - far_hi, 0.0)).astype(BF16)
    pad_rows = jnp.zeros((LANES - CONST_LANE - SUPER_BLOCKS, nl), BF16)

    def make_body(near):
        def sel_body(step, carry):
            tiles = []
            for u in range(SUPERS_PER_STEP):
                jj = step * SUPERS_PER_STEP + u
                mrows = _tile_heads(
                    sel_scr[pl.ds(pl.multiple_of(jj * SUPER_BLOCKS, SUPER_BLOCKS), SUPER_BLOCKS), :])
                q_aug = jnp.concatenate([qs_t, mrows.astype(BF16), const_rows, pad_rows], axis=0)
                for c in range(SUPER // KT):
                    s0 = pl.multiple_of(jj * SUPER + c * KT, KT)
                    x = jnp.dot(ksel_ref[pl.ds(s0, KT), :], q_aug, preferred_element_type=F32)
                    if near:
                        tile0 = jj * (SUPER // LANES) + c * (KT // LANES)
                        x = x + jnp.concatenate([bias_delta(jnp.clip(i - tile0 - a, 0, N_BIAS_TILES - 1))
                                                 for a in range(KT // LANES)], axis=0)
                    tiles.append((x, jnp.max(x, axis=0, keepdims=True), s0))
            parts = [(m, jnp.dot(vselt_ref[0:VT_ROWS, pl.ds(s0, KT)], jnp.exp2(x - m).astype(BF16),
                                 preferred_element_type=F32)) for x, m, s0 in tiles]
            for part in parts:
                carry = _merge_parts(carry, part)
            return carry
        return sel_body

    def bias_delta(a):
        return jnp.concatenate([btd_ref[a, r] for r in range(REP)], axis=1)

    keys_per_step = SUPER * SUPERS_PER_STEP
    n_steps = (2 * i) // (SUPER_BLOCKS * SUPERS_PER_STEP) + 1
    n_far = jnp.maximum(t0 - (N_BIAS_TILES - 1) * LANES + LANES, 0) // keys_per_step
    carry = lax.fori_loop(0, n_far, make_body(False), _flash_init(nl))
    carry = lax.fori_loop(n_far, n_steps, make_body(True), carry)

    sj = lax.broadcasted_iota(I32, (LANES, nl), 0)
    ti4 = _tile_heads(lax.broadcasted_iota(I32, (1, tq), 1))
    t0a = pl.multiple_of(t0, LANES)
    q_diag = jnp.concatenate([qs_t, jnp.zeros((SUPER_BLOCKS, nl), BF16), const_rows, pad_rows], axis=0)
    qw_bf = (_stack_heads(qw_ref[...]) * LOG2E).astype(BF16)
    kw_refs = (kw0, kw1, kw2, kw3, kw4)
    vw_refs = (vw0, vw1, vw2, vw3, vw4)
    x_diag = jnp.dot(ksel_ref[pl.ds(t0a, LANES), :], q_diag, preferred_element_type=F32)
    x_win = [_dot_nt(kw_refs[j][...], qw_bf) for j in range(5)]
    x = jnp.where((sj // BLOCK == ti4 // BLOCK) & (sj <= ti4), x_diag + bias_delta(0), NEG)
    _, accs = _online_step(carry, [(x, vselt_ref[0:VT_ROWS, pl.ds(t0a, LANES)])])

    tiles = []
    for j in range(5):
        dist = LANES * (4 - j) + ti4 - sj
        ok = jnp.where((dist >= 0) & (dist <= WINDOW), 1.0, 0.0) * jnp.where(i - 4 + j >= 0, 1.0, 0.0)
        tiles.append((jnp.where(ok > 0.5, x_win[j] + bias_tile(4 - j), NEG), vw_refs[j][0:VT_ROWS, :]))
    _, accw = _online_step(_flash_init(nl), tiles)

    out_t = (_gate_row(gt_ref, 0, g) * oc_t + _gate_row(gt_ref, 1, g) * _normalise(accs)
             + _gate_row(gt_ref, 2, g) * _normalise(accw))
    o_ref[...] = _untranspose_heads(out_t, tq)


def _attn_prompt(rel_bias, qs, gates_t, kcvc, sel_bf, selt_bf, win_bf, wint_bf, bias_tiles, bias_delta_tiles):
    t = qs.shape[0]
    nb = kcvc.shape[0]
    width = NKV * REP * HD

    def qspec(br):
        return pl.BlockSpec((TQ, REP * HD), lambda g, i, br=br: (i, br * NKV + g))

    def kw_spec(j):
        return pl.BlockSpec((LANES, LANES), lambda g, i, j=j: (jnp.maximum(i - 4 + j, 0), g))

    def vw_spec(j):
        return pl.BlockSpec((LANES, LANES), lambda g, i, j=j: (g, jnp.maximum(i - 4 + j, 0)))

    return pl.pallas_call(
        _attn_prompt_kernel,
        grid=(NKV, t // TQ),
        in_specs=[pl.BlockSpec(memory_space=pltpu.SMEM),
                  qspec(0), qspec(1), qspec(2),
                  pl.BlockSpec((LANES, TQ), lambda g, i: (0, i)),
                  pl.BlockSpec((nb, LANES), lambda g, i: (0, g)),
                  pl.BlockSpec((t, LANES), lambda g, i: (0, g)),
                  pl.BlockSpec((LANES, t), lambda g, i: (g, 0))]
                 + [kw_spec(j) for j in range(5)] + [vw_spec(j) for j in range(5)]
                 + [pl.BlockSpec((N_BIAS_TILES, REP, LANES, TQ), lambda g, i: (0, g, 0, 0))] * 2,
        out_specs=pl.BlockSpec((TQ, REP * HD), lambda g, i: (i, g)),
        out_shape=jax.ShapeDtypeStruct((t, width), F32),
        scratch_shapes=[pltpu.VMEM((nb, TQ), F32), pltpu.VMEM((nb, REP * TQ), F32)],
        compiler_params=_cparams("arbitrary", "arbitrary"),
        name="attn_prompt",
    )(rel_bias, qs, qs, qs, gates_t, kcvc, sel_bf, selt_bf,
      win_bf, win_bf, win_bf, win_bf, win_bf, wint_bf, wint_bf, wint_bf, wint_bf, wint_bf, bias_tiles,
      bias_delta_tiles)


def _attn_s1_kernel(tab_ref, qc_ref, kcvc_ref, oct_ref, idx_ref, xc_scr, *, q0):
    g = pl.program_id(1)
    tq = TQ
    pos = q0 + lax.broadcasted_iota(I32, (1, tq), 1)
    oc_t, score = _cmp_branch(_pad_rows(qc_ref[0], tq), kcvc_ref, tab_ref, g, pos, q0, xc_scr)
    oct_ref[0] = oc_t

    def pick(it, idx, ok, hit):
        idx_ref[0, it:it + 1, :] = jnp.where(ok, idx, -1.0).astype(I32)

    _topk_rows(score, TOPK - 1, pick)
    idx_ref[0, TOPK - 1:TOPK, :] = jnp.full((1, tq), -1, I32)


def _attn_s1(rel_bias, qs8, kcvc, q0):
    batch = qs8.shape[0]
    nb = kcvc.shape[0] // batch
    return pl.pallas_call(
        functools.partial(_attn_s1_kernel, q0=q0),
        grid=(batch, NKV),
        in_specs=[pl.BlockSpec(memory_space=pltpu.SMEM),
                  pl.BlockSpec((1, 8, REP * HD), lambda b, g: (b, 0, g)),
                  pl.BlockSpec((nb, LANES), lambda b, g: (b, g))],
        out_specs=[pl.BlockSpec((1, HD, REP * TQ), lambda b, g: (b * NKV + g, 0, 0)),
                   pl.BlockSpec((1, TOPK, TQ), lambda b, g: (b * NKV + g, 0, 0))],
        out_shape=[jax.ShapeDtypeStruct((batch * NKV, HD, REP * TQ), F32),
                   jax.ShapeDtypeStruct((batch * NKV, TOPK, TQ), I32)],
        scratch_shapes=[pltpu.VMEM((nb, REP * TQ), F32)],
        compiler_params=_cparams("arbitrary", "arbitrary"),
        name="attn_sample_select",
    )(rel_bias, qs8, kcvc)


def _attn_s2_kernel(ids_ref, phys_ref, qs_ref, qw_ref, gt_ref, oct_ref, *refs, q0, nvalid):
    nsel = nvalid * (TOPK - 1)
    kb = refs[:nsel]
    snew_ref, cwin_ref, wnew_ref, bt_ref, o_ref = refs[nsel:]
    b = pl.program_id(0)
    g = pl.program_id(1)
    tq = TQ_SAMPLE
    nr = REP * tq
    tr = lax.broadcasted_iota(I32, (nr, LANES), 0) % tq
    sj = lax.broadcasted_iota(I32, (nr, LANES), 1)
    base = (b * NKV + g) * nvalid * TOPK

    def compact(x):
        return jnp.concatenate([x[:, r * TQ:r * TQ + tq] for r in range(REP)], axis=1)

    def bias_rows(a):
        return jnp.concatenate([bt_ref[a, r][:tq, :] for r in range(REP)], axis=0)

    def attend(q_bf, cached, new_ref):
        xs, vts = [], []
        for tile_t, a, ok in cached:
            xs.append(jnp.where(ok, _dot(q_bf, tile_t) + bias_rows(a), NEG))
            vts.append(tile_t.astype(BF16))
        blk = _pad_rows(new_ref[0], LANES).astype(BF16)
        xs.append(jnp.where((sj <= tr) & (sj < nvalid), _dot_nt(q_bf, blk) + bias_rows(0), NEG))
        x = jnp.concatenate(xs, axis=1)
        p = jnp.exp2(x - jnp.max(x, axis=1, keepdims=True))
        l = jnp.sum(p, axis=1, keepdims=True)
        p = p.astype(BF16)
        nk = x.shape[1] - LANES
        acc = _dot_nt(p[:, :nk], jnp.concatenate(vts, axis=1)) + _dot(p[:, nk:], blk)
        return acc[:, HD:] / l

    cached = []
    for t in range(nvalid):
        for k in range(TOPK - 1):
            n = ids_ref[base + t * TOPK + k]
            nc = jnp.maximum(n, 0)
            a = jnp.clip(q0 // LANES - nc // 2, 0, N_BIAS_TILES - 1)
            lo = jnp.where(n >= 0, (nc % 2) * BLOCK, LANES)
            cached.append((kb[t * (TOPK - 1) + k][0], a, (sj >= lo) & (sj < lo + BLOCK) & (tr == t)))
    o_sel = attend((_stack_heads(_pad_rows(qs_ref[0], tq)) * LOG2E).astype(BF16), cached, snew_ref)

    cached = []
    for j in range(WINDOW // LANES):
        dist = (WINDOW - LANES * j) + tr - sj
        cached.append((cwin_ref[0, :, j * LANES:(j + 1) * LANES], WINDOW // LANES - j,
                       (dist >= 0) & (dist <= WINDOW)))
    o_win = attend((_stack_heads(_pad_rows(qw_ref[0], tq)) * LOG2E).astype(BF16), cached, wnew_ref)

    gt2 = gt_ref.at[0]
    o_cmp = _pad_rows(compact(oct_ref[0]), LANES).T[:, :HD]
    row8 = lax.broadcasted_iota(I32, (8, nr), 0)
    g8 = jnp.zeros((8, nr), F32)
    for br in range(3):
        g8 = jnp.where(row8 == br, compact(_gate_row(gt2, br, g)), g8)
    gates = _pad_rows(g8, LANES).T
    out = gates[:, 0:1] * o_cmp + gates[:, 1:2] * o_sel + gates[:, 2:3] * o_win
    o_ref[0] = jnp.concatenate([out[r * tq:r * tq + 8] for r in range(REP)], axis=1)


def _attn_s2(ids, phys, qs8, gates_t, oct, cache_sel_t, sel_new8, cache_win_t, win_new8, bias_tiles, q0, nvalid):
    batch = qs8.shape[0]
    nsel = nvalid * (TOPK - 1)

    def kb_spec(t, k):
        return pl.BlockSpec((1, LANES, PAGE),
                            lambda b, g, ids_r, phys_r, t=t, k=k:
                            (phys_r[((b * NKV + g) * nvalid + t) * TOPK + k], g, 0))

    def bg(shape, col):
        return pl.BlockSpec(shape, lambda b, g, ids_r, phys_r, col=col: (b, 0, col * NKV + g))

    return pl.pallas_call(
        functools.partial(_attn_s2_kernel, q0=q0, nvalid=nvalid),
        grid_spec=pltpu.PrefetchScalarGridSpec(
            num_scalar_prefetch=2,
            grid=(batch, NKV),
            in_specs=[bg((1, 8, REP * HD), 1), bg((1, 8, REP * HD), 2),
                      pl.BlockSpec((1, LANES, TQ), lambda b, g, ids_r, phys_r: (b, 0, 0)),
                      pl.BlockSpec((1, HD, REP * TQ), lambda b, g, ids_r, phys_r: (b * NKV + g, 0, 0))]
                     + [kb_spec(t, k) for t in range(nvalid) for k in range(TOPK - 1)]
                     + [bg((1, 8, LANES), 0),
                        pl.BlockSpec((1, LANES, WINDOW), lambda b, g, ids_r, phys_r: (b, g, 0)),
                        bg((1, 8, LANES), 0),
                        pl.BlockSpec((N_BIAS_TILES, REP, LANES, TQ), lambda b, g, ids_r, phys_r: (0, g, 0, 0))],
            out_specs=bg((1, 8, REP * HD), 0)),
        out_shape=jax.ShapeDtypeStruct((batch, 8, NKV * REP * HD), F32),
        compiler_params=_cparams("arbitrary", "arbitrary"),
        name="attn_sample",
    )(ids, phys, qs8, qs8, gates_t, oct, *([cache_sel_t] * nsel), sel_new8, cache_win_t, win_new8, bias_tiles)


def _pad_cols(w, n):
    return jnp.pad(w, ((0, 0), (0, n - w.shape[1])))


def _prep(p):
    inner = NH_A * DK_A
    width = NKV * REP * HD
    q = {}
    q['w_in'] = _pad_cols(p['w_in_a'][0], 4 * inner + LANES).astype(BF16)
    q['b_if'] = _pad_cols(p['b_if_a'][0][None, :], LANES)
    q['w_out'] = p['w_out_a'][0].astype(BF16)
    q['w_ff1'] = [p['w_ff1'][l].astype(BF16) for l in range(2)]
    q['w_ff2'] = [p['w_ff2'][l].astype(BF16) for l in range(2)]
    q['w_kv'] = p['w_kv'].astype(BF16)
    q['w_q'] = _pad_cols(p['w_q_b'][0], 3 * width + LANES).astype(BF16)
    q['b_gate'] = _pad_cols(p['b_gate_b'][0][None, :], LANES)
    q['w_o'] = p['w_o_b'][0].astype(BF16)
    q['q_gain'] = (jnp.tile(p['q_norm_b'][0][:, None, :], (1, NKV * REP, 1)) * (HD ** -0.5)).reshape(1, 3 * width)
    seg = np.arange(256) // HD
    q['bd'] = jnp.asarray((seg[:, None] == seg[None, :]).astype(np.float32) / HD, dtype=BF16)
    ones = jnp.ones((HD,), F32)
    q['k_gain'] = jnp.concatenate([jnp.tile(jnp.concatenate([p['k_norm'][br], ones]), NKV) for br in (1, 2)])[None, :]
    q['k_mask'] = jnp.tile(jnp.concatenate([ones, 0.0 * ones]), 2 * NKV)[None, :]
    q['k_gain0'] = jnp.concatenate([p['k_norm'][0], ones])[None, :]
    w1 = p['w_cmp1']
    z = jnp.zeros_like(w1[0])
    per_l = jnp.concatenate([jnp.concatenate([w1[0], z], axis=-1), jnp.concatenate([z, w1[1]], axis=-1)], axis=1)
    q['w_cmp1'] = jnp.concatenate([per_l[:BLOCK // 2], per_l[BLOCK // 2:]], axis=1).astype(BF16)
    w2 = p['w_cmp2']
    z2 = jnp.zeros_like(w2[0])
    q['w_cmp2'] = jnp.concatenate([jnp.concatenate([w2[0], z2], axis=1), jnp.concatenate([z2, w2[1]], axis=1)],
                                  axis=0).astype(BF16)
    q['pos_rows'] = jnp.tile(p['cmp_pos'].reshape(BLOCK, 2 * HD), (1, NKV))
    return q


def _feature_major(cache):
    n, t = cache.shape[:2]
    return jnp.transpose(cache, (0, 2, 3, 4, 1)).reshape(n, -1, t)


def _mods(ada_rows, rep):
    k = ada_rows.shape[1] // D_MODEL
    out = []
    for j in range(k):
        a = ada_rows[:, j * D_MODEL:(j + 1) * D_MODEL]
        out.append(jnp.repeat(a, rep, axis=0) if rep > 1 else a)
    return out


def _layer0(x, mods, p, q, mlstm_fn, tm, tmf):
    sh1, sc1, g1, sh2, sc2, g2 = mods
    proj = _proj_plain(x, p['norm_mix'][0][None, :], sh1, sc1, q['w_in'], tm)
    y, states = mlstm_fn(proj)
    x = _out_proj(x, g1, y, q['w_out'], tm)
    x = _ffn(x, p['norm_ffn'][0][None, :], sh2, sc2, g2, q['w_ff1'][0], q['w_ff2'][0], tmf)
    return x, states


def _layer1_tail(x, attn, mods, p, q, tm, tmf):
    _, _, g1, sh2, sc2, g2 = mods
    x = _out_proj(x, g1, attn, q['w_o'], tm)
    return _ffn(x, p['norm_ffn'][1][None, :], sh2, sc2, g2, q['w_ff1'][1], q['w_ff2'][1], tmf)


def kernel(x_prompt, x_sample, cache_cmp, cache_sel, cache_win, state_C, state_n, state_m, page_table,
           c_prompt, c_sample, w_ada, b_ada, norm_mix, norm_ffn, w_ff1, w_ff2, w_in_a, b_if_a,
           head_norm_a, w_out_a, w_ada_kv, b_ada_kv, norm_kv, w_kv, k_norm, cmp_pos, w_cmp1, w_cmp2,
           w_q_b, b_gate_b, q_norm_b, w_o_b, rel_bias):
    p = dict(norm_mix=norm_mix, norm_ffn=norm_ffn, w_ff1=w_ff1, w_ff2=w_ff2, w_in_a=w_in_a, b_if_a=b_if_a,
             w_out_a=w_out_a, w_kv=w_kv, k_norm=k_norm, cmp_pos=cmp_pos, w_cmp1=w_cmp1, w_cmp2=w_cmp2,
             w_q_b=w_q_b, b_gate_b=b_gate_b, q_norm_b=q_norm_b, w_o_b=w_o_b)
    q = _prep(p)
    bp, tp, d = x_prompt.shape
    bs, ts, _ = x_sample.shape
    past = page_table.shape[1] * PAGE
    wbuf = cache_win.shape[1]
    assert bp == 1 and ts < BLOCK and ts <= 8 and wbuf == WINDOW and past % LANES == 0
    assert tp % (SUPER * SUPERS_PER_STEP) == 0
    width = NKV * REP * HD
    kvw = NKV * 2 * HD

    nc = bp + bs
    c_all = jnp.pad(jnp.concatenate([c_prompt, c_sample], axis=0), ((0, -nc % 8), (0, 0)))
    ada = [_ada(c_all, w_ada[l], b_ada[l]) for l in range(2)]
    ada_kv = _ada(c_all, w_ada_kv, b_ada_kv)
    bias_kq, bias_kq_delta, bias_qk = _bias_tiles(rel_bias)
    head_gain = head_norm_a[0]

    xp = x_prompt.reshape(tp, d)
    zc = jnp.zeros((bp, NH_A, DK_A, DK_A), F32)
    zn = jnp.zeros((bp, NH_A, 1, DK_A), F32)
    zm = jnp.zeros((bp, 8, LANES), F32)

    def mlstm_prompt(proj):
        y, c, n, m = _mlstm(proj.reshape(bp, tp, -1), q['b_if'], head_gain, zc, zn, zm,
                            rows=256, valid=256, chunk=256, zero_init=True)
        return y.reshape(tp, -1), (c, n, m)

    xp, (pc, pn, pm) = _layer0(xp, _mods(ada[0][:bp], 1), p, q, mlstm_prompt, 256, 512)
    sh, sc = _mods(ada_kv[:bp], 1)
    p_cmp, _, _, p_cmp_t, p_sel_t, p_win_t, sel_bf, selt_bf, win_bf, wint_bf = _proj_kv(
        xp, norm_kv[None, :], sh, sc, q['w_kv'], q['bd'], q['k_gain'], q['k_mask'], 256)
    kcvc_p = _compress(p_cmp, q['pos_rows'], q['w_cmp1'], q['w_cmp2'], q['k_gain0'],
                       min(CMP_BLOCKS_PER_STEP, tp // BLOCK))
    mods1 = _mods(ada[1][:bp], 1)
    qs_p, gt_p = _proj_q(xp, norm_mix[1][None, :], mods1[0], mods1[1], q['w_q'], q['bd'], q['q_gain'],
                         q['b_gate'], 256)
    attn_p = _attn_prompt(rel_bias, qs_p, gt_p, kcvc_p, sel_bf, selt_bf, win_bf, wint_bf, bias_kq, bias_kq_delta)
    y_prompt = _layer1_tail(xp, attn_p, mods1, p, q, 256, 512).reshape(bp, tp, d)

    ms = bs * ts
    xs = x_sample.reshape(ms, d)
    m0 = jnp.broadcast_to(jnp.pad(state_m[0], ((0, 0), (0, 8 - NH_A)))[:, :, None], (bs, 8, LANES))

    def mlstm_sample(proj):
        proj8 = jnp.pad(proj.reshape(bs, ts, -1), ((0, 0), (0, 8 - ts), (0, 0)))
        y, c, n, m = _mlstm(proj8, q['b_if'], head_gain, state_C[0], state_n[0][:, :, None, :], m0,
                            rows=8, valid=ts, chunk=LANES, zero_init=False)
        return y[:, :ts].reshape(ms, -1), (c, n, m)

    xs, (sc_, sn_, sm_) = _layer0(xs, _mods(ada[0][bp:nc], ts), p, q, mlstm_sample, ms, ms)
    sh, sc = _mods(ada_kv[bp:nc], ts)
    s_cmp, s_sel, s_win = _proj_kv(xs, norm_kv[None, :], sh, sc, q['w_kv'], q['bd'], q['k_gain'],
                                   q['k_mask'], ms)[:3]
    kcvc_s = _compress_paged(_feature_major(cache_cmp), page_table, q['pos_rows'],
                             q['w_cmp1'], q['w_cmp2'], q['k_gain0'], min(CMP_BLOCKS_PER_STEP, past // BLOCK))
    mods1 = _mods(ada[1][bp:nc], ts)
    qs_s, gt_s = _proj_q(xs, norm_mix[1][None, :], mods1[0], mods1[1], q['w_q'], q['bd'], q['q_gain'],
                         q['b_gate'], ms)

    def pad8(a):
        return jnp.pad(a.reshape(bs, ts, -1), ((0, 0), (0, 8 - ts), (0, 0)))

    qs8 = pad8(qs_s)
    oct_s, idx_s = _attn_s1(rel_bias, qs8, kcvc_s, past)
    ids = jnp.transpose(idx_s.reshape(bs, NKV, TOPK, TQ)[:, :, :, :ts], (0, 1, 3, 2))
    idc = jnp.maximum(ids, 0)
    pages = jnp.take_along_axis(page_table, (idc // (PAGE // BLOCK)).reshape(bs, -1), axis=1).reshape(ids.shape)
    gt8 = jnp.pad(jnp.transpose(gt_s.reshape(LANES, bs, ts), (1, 0, 2)), ((0, 0), (0, 0), (0, TQ - ts)))
    attn_s = _attn_s2(ids.reshape(-1), pages.reshape(-1), qs8, gt8, oct_s,
                      _feature_major(cache_sel), pad8(s_sel), _feature_major(cache_win), pad8(s_win),
                      bias_qk, past, ts)
    y_sample = _layer1_tail(xs, attn_s[:, :ts].reshape(ms, width), mods1, p, q, ms, ms).reshape(bs, ts, d)

    rows5 = (NKV, 2, HD)

    def token_major(a_t):
        return jnp.transpose(a_t.reshape(*rows5, a_t.shape[1]), (3, 0, 1, 2))[None]

    s_win_all = jnp.concatenate([cache_win, s_win.reshape(bs, ts, *rows5)], axis=1)
    return (y_prompt, y_sample,
            pc[None], pn.reshape(1, bp, NH_A, DK_A), pm[None, :, :NH_A, 0],
            token_major(p_cmp_t), token_major(p_sel_t), token_major(p_win_t[:, tp - min(WINDOW, tp):]),
            sc_[None], sn_.reshape(1, bs, NH_A, DK_A), sm_[None, :, :NH_A, 0],
            s_cmp.reshape(bs, ts, *rows5), s_sel.reshape(bs, ts, *rows5),
            s_win_all[:, -min(WINDOW, wbuf + ts):])
```

```python
import functools
import math

import numpy as np
import jax
import jax.numpy as jnp
from jax import lax
from jax.experimental import pallas as pl
from jax.experimental.pallas import tpu as pltpu

F32 = jnp.float32
BF16 = jnp.bfloat16
I32 = jnp.int32

D_MODEL = 1024
NH_A = 4
DK_A = 256
NKV = 4
REP = 4
HD = 64
BLOCK = 64
TOPK = 16
WINDOW = 512
N_BUCKETS = 32
REL_MAX_DIST = 2048
PAGE = 128
EPS = 1e-6
NEG = -1e30
LOG2E = math.log2(math.e)

LANES = 128
VMEM_LIMIT = 56 * 1024 * 1024
TQ = 128
TQ_SAMPLE = 32
KT = 256
SUPER_BLOCKS = 16
SUPER = SUPER_BLOCKS * BLOCK
SUPERS_PER_STEP = 2
VT_ROWS = HD + 16
CONST_LANE = HD + SUPER_BLOCKS
N_BIAS_TILES = 14
CMP_BLOCKS_PER_STEP = 64
CMP_BIAS_ROWS = 40


def _bucket_thresholds():
    exact = N_BUCKETS // 2
    d = np.arange(1, 4 * REL_MAX_DIST, dtype=np.float64)
    big = exact + np.floor(np.log(d / exact) / math.log(REL_MAX_DIST / exact) * (N_BUCKETS - exact)).astype(np.int64)
    b = np.where(d < exact, d.astype(np.int64), np.minimum(big, N_BUCKETS - 1))
    return [int(d[np.argmax(b >= k)]) for k in range(1, N_BUCKETS)]


BUCKET_THR = _bucket_thresholds()
assert 128 * (N_BIAS_TILES - 1) - (TQ - 1) >= BUCKET_THR[-1]


def _cparams(*sem):
    return pltpu.CompilerParams(dimension_semantics=sem, vmem_limit_bytes=VMEM_LIMIT)


def _dot(a, b):
    return jnp.dot(a.astype(BF16), b.astype(BF16), preferred_element_type=F32)


def _dot_nt(a, b):
    return lax.dot_general(a.astype(BF16), b.astype(BF16), (((1,), (1,)), ((), ())),
                           preferred_element_type=F32)


def _split3(x):
    hi = x.astype(BF16)
    r1 = x - hi.astype(F32)
    mid = r1.astype(BF16)
    lo = (r1 - mid.astype(F32)).astype(BF16)
    return hi, mid, lo


def _sigmoid(x):
    return 1.0 / (1.0 + jnp.exp(-x))


def _norm_mod(x, gain, shift, scale):
    ms = jnp.mean(x * x, axis=-1, keepdims=True)
    y = x * lax.rsqrt(ms + EPS) * gain
    return y * (1.0 + scale) + shift


def _seg_mean_sq(y, bd):
    parts = []
    for j in range(y.shape[1] // 256):
        sq = y[:, j * 256:(j + 1) * 256]
        sq = sq * sq
        hi = sq.astype(BF16)
        lo = (sq - hi.astype(F32)).astype(BF16)
        parts.append(jnp.dot(hi, bd, preferred_element_type=F32) + jnp.dot(lo, bd, preferred_element_type=F32))
    return parts[0] if len(parts) == 1 else jnp.concatenate(parts, axis=1)


def _row_spec(tm, n, per_row):
    if per_row:
        return pl.BlockSpec((tm, n), lambda i: (i, 0))
    return pl.BlockSpec((1, n), lambda i: (0, 0))


def _const_spec(shape):
    return pl.BlockSpec(shape, lambda i: tuple(0 for _ in shape))


def _ada_kernel(c_ref, w_ref, b_ref, o_ref):
    c = c_ref[...]
    o_ref[...] = _dot(c * _sigmoid(c), w_ref[...]) + b_ref[...]


def _ada(c, w, b):
    m, k = c.shape
    n = w.shape[1]
    tn = 1024
    return pl.pallas_call(
        _ada_kernel,
        grid=(n // tn,),
        in_specs=[pl.BlockSpec((m, k), lambda j: (0, 0)),
                  pl.BlockSpec((k, tn), lambda j: (0, j)),
                  pl.BlockSpec((1, tn), lambda j: (0, j))],
        out_specs=pl.BlockSpec((m, tn), lambda j: (0, j)),
        out_shape=jax.ShapeDtypeStruct((m, n), F32),
        compiler_params=_cparams("arbitrary"),
        name="ada",
    )(c, w, b.reshape(1, n))


def _proj_plain_kernel(x_ref, gain_ref, sh_ref, sc_ref, w_ref, o_ref):
    xn = _norm_mod(x_ref[...], gain_ref[...], sh_ref[...], sc_ref[...])
    o_ref[...] = jnp.dot(xn.astype(BF16), w_ref[...], preferred_element_type=F32)


def _proj_plain(x, gain, shift, scale, w_bf, tm):
    m, d = x.shape
    n = w_bf.shape[1]
    per_row = shift.shape[0] != 1
    return pl.pallas_call(
        _proj_plain_kernel,
        grid=(m // tm,),
        in_specs=[pl.BlockSpec((tm, d), lambda i: (i, 0)),
                  _const_spec((1, d)),
                  _row_spec(tm, d, per_row), _row_spec(tm, d, per_row),
                  _const_spec((d, n))],
        out_specs=pl.BlockSpec((tm, n), lambda i: (i, 0)),
        out_shape=jax.ShapeDtypeStruct((m, n), F32),
        compiler_params=_cparams("arbitrary"),
        name="proj_mlstm",
    )(x, gain, shift, scale, w_bf)


def _proj_q_kernel(x_ref, gain_ref, sh_ref, sc_ref, w_ref, bd_ref, qg_ref, bg_ref, q_ref, gt_ref):
    xn = _norm_mod(x_ref[...], gain_ref[...], sh_ref[...], sc_ref[...])
    y = jnp.dot(xn.astype(BF16), w_ref[...], preferred_element_type=F32)
    nq = q_ref.shape[1]
    yq = y[:, :nq]
    ms = _seg_mean_sq(yq, bd_ref[...])
    q_ref[...] = yq * lax.rsqrt(ms + EPS) * qg_ref[...]
    gates = _sigmoid(y[:, nq:] + bg_ref[...])
    gt_ref[...] = gates.T


def _proj_q(x, gain, shift, scale, w_bf, bd, qgain_row, bgate_row, tm):
    m, d = x.shape
    n = w_bf.shape[1]
    nq = n - LANES
    per_row = shift.shape[0] != 1
    return pl.pallas_call(
        _proj_q_kernel,
        grid=(m // tm,),
        in_specs=[pl.BlockSpec((tm, d), lambda i: (i, 0)),
                  _const_spec((1, d)),
                  _row_spec(tm, d, per_row), _row_spec(tm, d, per_row),
                  _const_spec((d, n)), _const_spec((256, 256)),
                  _const_spec((1, nq)), _const_spec((1, LANES))],
        out_specs=[pl.BlockSpec((tm, nq), lambda i: (i, 0)),
                   pl.BlockSpec((LANES, tm), lambda i: (0, i))],
        out_shape=[jax.ShapeDtypeStruct((m, nq), F32),
                   jax.ShapeDtypeStruct((LANES, m), F32)],
        compiler_params=_cparams("arbitrary"),
        name="proj_q",
    )(x, gain, shift, scale, w_bf, bd, qgain_row, bgate_row)


def _proj_kv_kernel(x_ref, gain_ref, sh_ref, sc_ref, w_ref, bd_ref, kg_ref, km_ref,
                    cmp_ref, sel_ref, win_ref, cmpt_ref, selt_ref, wint_ref,
                    selb_ref, seltb_ref, winb_ref, wintb_ref):
    xn = _norm_mod(x_ref[...], gain_ref[...], sh_ref[...], sc_ref[...])
    y = jnp.dot(xn.astype(BF16), w_ref[...], preferred_element_type=F32)
    w = cmp_ref.shape[1]
    cmp = y[:, :w]
    ykn = y[:, w:]
    ms = _seg_mean_sq(ykn, bd_ref[...])
    ykn = jnp.where(km_ref[...] > 0.5, ykn * lax.rsqrt(ms + EPS) * kg_ref[...], ykn)
    sel = ykn[:, :w]
    win = ykn[:, w:]
    sel_t = sel.T
    win_t = win.T
    cmp_ref[...] = cmp
    sel_ref[...] = sel
    win_ref[...] = win
    cmpt_ref[...] = cmp.T
    selt_ref[...] = sel_t
    wint_ref[...] = win_t
    tm = sel.shape[0]
    tok = pl.program_id(0) * tm + lax.broadcasted_iota(I32, sel.shape, 0)
    lane = lax.broadcasted_iota(I32, sel.shape, 1) % LANES
    onehot = jnp.where((lane - HD == (tok // BLOCK) % SUPER_BLOCKS) | (lane == CONST_LANE)
                       | (lane == CONST_LANE + 1), 1.0, 0.0)
    selb_ref[...] = jnp.where(lane < HD, sel, onehot).astype(BF16)
    winb_ref[...] = win.astype(BF16)

    def value_rows(x_t):
        ones_blk = jnp.where(lax.broadcasted_iota(I32, (HD, tm), 0) == 0, 1.0, 0.0)
        return jnp.concatenate(
            [piece for gg in range(NKV) for piece in (x_t[gg * LANES + HD:(gg + 1) * LANES], ones_blk)],
            axis=0).astype(BF16)

    seltb_ref[...] = value_rows(sel_t)
    wintb_ref[...] = value_rows(win_t)


def _proj_kv(x, gain, shift, scale, w_bf, bd, kgain_row, kmask_row, tm):
    m, d = x.shape
    n = w_bf.shape[1]
    w = n // 3
    per_row = shift.shape[0] != 1
    row = pl.BlockSpec((tm, w), lambda i: (i, 0))
    col = pl.BlockSpec((w, tm), lambda i: (0, i))
    return pl.pallas_call(
        _proj_kv_kernel,
        grid=(m // tm,),
        in_specs=[pl.BlockSpec((tm, d), lambda i: (i, 0)),
                  _const_spec((1, d)),
                  _row_spec(tm, d, per_row), _row_spec(tm, d, per_row),
                  _const_spec((d, n)), _const_spec((256, 256)),
                  _const_spec((1, 2 * w)), _const_spec((1, 2 * w))],
        out_specs=[row, row, row, col, col, col, row, col, row, col],
        out_shape=[jax.ShapeDtypeStruct((m, w), F32)] * 3 + [jax.ShapeDtypeStruct((w, m), F32)] * 3
                  + [jax.ShapeDtypeStruct((m, w), BF16), jax.ShapeDtypeStruct((w, m), BF16),
                     jax.ShapeDtypeStruct((m, w), BF16), jax.ShapeDtypeStruct((w, m), BF16)],
        compiler_params=_cparams("arbitrary"),
        name="proj_kv",
    )(x, gain, shift, scale, w_bf, bd, kgain_row, kmask_row)


def _out_proj_kernel(x_ref, g_ref, y_ref, w_ref, o_ref):
    o_ref[...] = x_ref[...] + g_ref[...] * jnp.dot(y_ref[...].astype(BF16), w_ref[...],
                                                   preferred_element_type=F32)


def _out_proj(x, g, y, w_bf, tm):
    m, d = x.shape
    k = y.shape[1]
    per_row = g.shape[0] != 1
    return pl.pallas_call(
        _out_proj_kernel,
        grid=(m // tm,),
        in_specs=[pl.BlockSpec((tm, d), lambda i: (i, 0)),
                  _row_spec(tm, d, per_row),
                  pl.BlockSpec((tm, k), lambda i: (i, 0)),
                  _const_spec((k, d))],
        out_specs=pl.BlockSpec((tm, d), lambda i: (i, 0)),
        out_shape=jax.ShapeDtypeStruct((m, d), F32),
        compiler_params=_cparams("arbitrary"),
        name="out_proj",
    )(x, g, y, w_bf)


def _ffn_kernel(x_ref, gain_ref, sh_ref, sc_ref, g_ref, w1_ref, w2_ref, o_ref, *, fc):
    x = x_ref[...]
    xn = _norm_mod(x, gain_ref[...], sh_ref[...], sc_ref[...]).astype(BF16)
    acc = jnp.zeros(x.shape, F32)
    for c in range(w1_ref.shape[1] // fc):
        h = jnp.dot(xn, w1_ref[:, c * fc:(c + 1) * fc], preferred_element_type=F32)
        h = jnp.maximum(h, 0.0)
        acc = acc + jnp.dot((h * h).astype(BF16), w2_ref[c * fc:(c + 1) * fc, :],
                            preferred_element_type=F32)
    o_ref[...] = x + g_ref[...] * acc


def _ffn(x, gain, shift, scale, g, w1_bf, w2_bf, tm):
    m, d = x.shape
    f = w1_bf.shape[1]
    per_row = shift.shape[0] != 1
    return pl.pallas_call(
        functools.partial(_ffn_kernel, fc=1024),
        grid=(m // tm,),
        in_specs=[pl.BlockSpec((tm, d), lambda i: (i, 0)),
                  _const_spec((1, d)),
                  _row_spec(tm, d, per_row), _row_spec(tm, d, per_row), _row_spec(tm, d, per_row),
                  _const_spec((d, f)), _const_spec((f, d))],
        out_specs=pl.BlockSpec((tm, d), lambda i: (i, 0)),
        out_shape=jax.ShapeDtypeStruct((m, d), F32),
        compiler_params=_cparams("arbitrary"),
        name="ffn",
    )(x, gain, shift, scale, g, w1_bf, w2_bf)


def _mlstm_kernel(q_ref, k_ref, v_ref, o_ref, gt_ref, bif_ref, hg_ref, c0_ref, n0_ref, m0_ref,
                  y_ref, cout_ref, nout_ref, mout_ref, c_scr, n_scr, m_scr, *, rows, valid, chunk, zero_init):
    ci = pl.program_id(1)
    nci = pl.num_programs(1)

    @pl.when(ci == 0)
    def _():
        if zero_init:
            c_scr[...] = jnp.zeros(c_scr.shape, F32)
            n_scr[...] = jnp.zeros(n_scr.shape, F32)
            m_scr[...] = jnp.zeros(m_scr.shape, F32)
        else:
            c_scr[...] = c0_ref[0]
            n_scr[...] = n0_ref[0]
            m_scr[...] = m0_ref[0]

    L = chunk

    def padded(ref_val, fill):
        if rows == L:
            return ref_val
        pad = jnp.full((L - rows, ref_val.shape[1]), fill, ref_val.dtype)
        return jnp.concatenate([ref_val, pad], axis=0)

    q = padded(q_ref[0], 0.0)
    k = padded(k_ref[0], 0.0)
    v = padded(v_ref[0], 0.0)
    og = padded(o_ref[0], 0.0)
    g = padded(gt_ref[0], 0.0) + bif_ref[...]
    lf = -(jnp.maximum(-g, 0.0) + jnp.log1p(jnp.exp(-jnp.abs(g))))
    li = g
    if valid != L:
        is_real = lax.broadcasted_iota(I32, (L, LANES), 0) < valid
        lf = jnp.where(is_real, lf, 0.0)
        li = jnp.where(is_real, li, NEG)
    r_io = lax.broadcasted_iota(I32, (L, L), 0)
    c_io = lax.broadcasted_iota(I32, (L, L), 1)
    causal = c_io <= r_io
    tril = jnp.where(causal, 1.0, 0.0).astype(BF16)
    hi, mid, lo = _split3(lf)
    b = (jnp.dot(tril, hi, preferred_element_type=F32) + jnp.dot(tril, mid, preferred_element_type=F32)
         + jnp.dot(tril, lo, preferred_element_type=F32))
    lane = lax.broadcasted_iota(I32, (L, LANES), 1)
    mixed_t = jnp.where(lane < NH_A, li, b).T
    for h in range(NH_A):
        sl = slice(h * DK_A, (h + 1) * DK_A)
        qh = q[:, sl] * (DK_A ** -0.5)
        kh = k[:, sl]
        vh = v[:, sl]
        b_col = b[:, NH_A + h:NH_A + h + 1]
        li_col = li[:, h:h + 1]
        b_row = mixed_t[NH_A + h:NH_A + h + 1, :]
        li_row = mixed_t[h:h + 1, :]
        m_prev = m_scr[h:h + 1, 0:1]
        c_prev = c_scr[h]
        n_prev = n_scr[h]
        dmat = jnp.where(causal, b_col - b_row + li_row, NEG)
        m_inter = b_col + m_prev
        m_t = jnp.maximum(m_inter, jnp.max(dmat, axis=-1, keepdims=True))
        s = _dot_nt(qh, kh) * jnp.exp(dmat - m_t)
        a_inter = jnp.exp(m_inter - m_t)
        num = _dot(s, vh) + a_inter * _dot(qh, c_prev)
        den = jnp.sum(s, axis=-1, keepdims=True) + a_inter * jnp.sum(qh * n_prev, axis=-1, keepdims=True)
        hh = num / jnp.maximum(jnp.abs(den), jnp.exp(-m_t))
        b_last = b_col[L - 1:L, :]
        g_col = b_last - b_col + li_col
        m_new = jnp.maximum(b_last + m_prev, jnp.max(g_col, axis=0, keepdims=True))
        w_col = jnp.exp(g_col - m_new)
        decay = jnp.exp(b_last + m_prev - m_new)
        kw = kh * w_col
        c_scr[h] = decay * c_prev + _dot(kw.T, vh)
        n_scr[h] = decay * n_prev + jnp.sum(kw, axis=0, keepdims=True)
        m_scr[h:h + 1, :] = jnp.broadcast_to(m_new, (1, LANES))
        hn = hh * lax.rsqrt(jnp.mean(hh * hh, axis=-1, keepdims=True) + EPS) * hg_ref[h:h + 1, :]
        yh = hn * _sigmoid(og[:, sl])
        y_ref[0, :, sl] = yh[:rows]

    @pl.when(ci == nci - 1)
    def _():
        cout_ref[0] = c_scr[...]
        nout_ref[0] = n_scr[...]
        mout_ref[0] = m_scr[...]


def _mlstm(proj, b_if_row, head_gain, c0, n0, m0, *, rows, valid, chunk, zero_init):
    batch, t, _ = proj.shape
    nch = t // rows
    inner = NH_A * DK_A

    def colblk(j, width):
        return pl.BlockSpec((1, rows, width), lambda b, c, j=j: (b, c, j))

    state_c = pl.BlockSpec((1, NH_A, DK_A, DK_A), lambda b, c: (b, 0, 0, 0))
    state_n = pl.BlockSpec((1, NH_A, 1, DK_A), lambda b, c: (b, 0, 0, 0))
    state_m = pl.BlockSpec((1, 8, LANES), lambda b, c: (b, 0, 0))
    return pl.pallas_call(
        functools.partial(_mlstm_kernel, rows=rows, valid=valid, chunk=chunk, zero_init=zero_init),
        grid=(batch, nch),
        in_specs=[colblk(0, inner), colblk(1, inner), colblk(2, inner), colblk(3, inner),
                  colblk(4 * inner // LANES, LANES),
                  pl.BlockSpec((1, LANES), lambda b, c: (0, 0)),
                  pl.BlockSpec((NH_A, DK_A), lambda b, c: (0, 0)),
                  state_c, state_n, state_m],
        out_specs=[pl.BlockSpec((1, rows, inner), lambda b, c: (b, c, 0)), state_c, state_n, state_m],
        out_shape=[jax.ShapeDtypeStruct((batch, t, inner), F32),
                   jax.ShapeDtypeStruct((batch, NH_A, DK_A, DK_A), F32),
                   jax.ShapeDtypeStruct((batch, NH_A, 1, DK_A), F32),
                   jax.ShapeDtypeStruct((batch, 8, LANES), F32)],
        scratch_shapes=[pltpu.VMEM((NH_A, DK_A, DK_A), F32), pltpu.VMEM((NH_A, 1, DK_A), F32),
                        pltpu.VMEM((8, LANES), F32)],
        compiler_params=_cparams("arbitrary", "arbitrary"),
        name="mlstm",
    )(proj, proj, proj, proj, proj, b_if_row, head_gain, c0, n0, m0)


def _compress_body(r_refs, pos_ref, w1_ref, w2_ref, kg_ref, o_ref, nb):
    acc = None
    for lp in range(BLOCK // 2):
        pieces = []
        for g in range(NKV):
            xa = r_refs[g][pl.ds(2 * lp, nb, stride=BLOCK), :]
            xb = r_refs[g][pl.ds(2 * lp + 1, nb, stride=BLOCK), :]
            pieces.append(jnp.concatenate([xa, xb], axis=1))
        pa = pos_ref[2 * lp:2 * lp + 1, 0:LANES]
        pb = pos_ref[2 * lp + 1:2 * lp + 2, 0:LANES]
        pieces.append(jnp.broadcast_to(jnp.concatenate([pa, pb], axis=1), (8, 2 * LANES)))
        x = jnp.concatenate(pieces, axis=0).astype(BF16)
        d = jnp.dot(x, w1_ref[lp], preferred_element_type=F32)
        acc = d if acc is None else acc + d
    pre = acc[:NKV * nb] + acc[NKV * nb:NKV * nb + 1]
    hid = pre * _sigmoid(pre)
    out = jnp.dot(hid.astype(BF16), w2_ref[...], preferred_element_type=F32)
    is_k = lax.broadcasted_iota(I32, out.shape, 1) < HD
    ms = jnp.sum(jnp.where(is_k, out * out, 0.0), axis=-1, keepdims=True) * (1.0 / HD)
    out = jnp.where(is_k, out * lax.rsqrt(ms + EPS) * kg_ref[...], out)
    for g in range(NKV):
        o_ref[:, g * LANES:(g + 1) * LANES] = out[g * nb:(g + 1) * nb]


def _compress_kernel(r0, r1, r2, r3, pos_ref, w1_ref, w2_ref, kg_ref, o_ref, *, nb):
    _compress_body((r0, r1, r2, r3), pos_ref, w1_ref, w2_ref, kg_ref, o_ref, nb)


def _compress_paged_kernel(pt_ref, *refs, nb):
    npages = nb * BLOCK // PAGE
    pages = refs[:npages]
    pos_ref, w1_ref, w2_ref, kg_ref, o_ref = refs[npages:npages + 5]
    r_scrs = refs[npages + 5:]
    for p in range(npages):
        for g in range(NKV):
            r_scrs[g][p * PAGE:(p + 1) * PAGE, :] = pages[p][0, g * LANES:(g + 1) * LANES, :].T
    _compress_body(r_scrs, pos_ref, w1_ref, w2_ref, kg_ref, o_ref, nb)


def _compress_paged(cache_t, page_table, pos_rows, w1p, w2p, kgain_row, nb):
    batch, npg = page_table.shape
    w = cache_t.shape[1]
    npages = nb * BLOCK // PAGE
    steps = npg // npages

    def page_spec(p):
        return pl.BlockSpec((1, w, PAGE), lambda b, j, pt, p=p: (pt[b, j * npages + p], 0, 0))

    def const(shape):
        return pl.BlockSpec(shape, lambda b, j, pt: tuple(0 for _ in shape))

    return pl.pallas_call(
        functools.partial(_compress_paged_kernel, nb=nb),
        grid_spec=pltpu.PrefetchScalarGridSpec(
            num_scalar_prefetch=1,
            grid=(batch, steps),
            in_specs=[page_spec(p) for p in range(npages)]
                     + [const((BLOCK, w)), const(w1p.shape), const(w2p.shape), const((1, LANES))],
            out_specs=pl.BlockSpec((nb, w), lambda b, j, pt: (b * steps + j, 0)),
            scratch_shapes=[pltpu.VMEM((nb * BLOCK, LANES), F32) for _ in range(NKV)]),
        out_shape=jax.ShapeDtypeStruct((batch * npg * PAGE // BLOCK, w), F32),
        compiler_params=_cparams("arbitrary", "arbitrary"),
        name="compress_paged",
    )(page_table, *([cache_t] * npages), pos_rows, w1p, w2p, kgain_row)


def _compress(rows, pos_rows, w1p, w2p, kgain_row, nb):
    t, w = rows.shape
    nblk = t // BLOCK
    return pl.pallas_call(
        functools.partial(_compress_kernel, nb=nb),
        grid=(nblk // nb,),
        in_specs=[pl.BlockSpec((nb * BLOCK, LANES), lambda i, g=g: (i, g)) for g in range(NKV)]
                 + [_const_spec((BLOCK, w)), _const_spec(w1p.shape), _const_spec(w2p.shape),
                    _const_spec((1, LANES))],
        out_specs=pl.BlockSpec((nb, w), lambda i: (i, 0)),
        out_shape=jax.ShapeDtypeStruct((nblk, w), F32),
        compiler_params=_cparams("arbitrary"),
        name="compress",
    )(rows, rows, rows, rows, pos_rows, w1p, w2p, kgain_row)


def _bias_from_dist(dist, tab_ref, h):
    out = jnp.full(dist.shape, tab_ref[0, h], F32)
    for kk in range(1, N_BUCKETS):
        out = jnp.where(dist >= BUCKET_THR[kk - 1], tab_ref[kk, h], out)
    return out


def _bias_tiles_kernel(tab_ref, kq_ref, kqd_ref, qk_ref):
    a = pl.program_id(0)
    row = lax.broadcasted_iota(I32, (LANES, TQ), 0)
    col = lax.broadcasted_iota(I32, (LANES, TQ), 1)

    def head(h, _):
        kq = _bias_from_dist(a * LANES + col - row, tab_ref, h)
        kq_ref[0, h] = kq * LOG2E
        kqd_ref[0, h] = (kq - tab_ref[N_BUCKETS - 1, h]) * LOG2E
        qk_ref[0, h] = _bias_from_dist(a * LANES + row - col, tab_ref, h) * LOG2E
        return 0

    lax.fori_loop(0, kq_ref.shape[1], head, 0)


def _bias_tiles(rel_bias):
    nh = rel_bias.shape[1]
    spec = pl.BlockSpec((1, nh, LANES, TQ), lambda a: (a, 0, 0, 0))
    shape = jax.ShapeDtypeStruct((N_BIAS_TILES, nh, LANES, TQ), F32)
    return pl.pallas_call(
        _bias_tiles_kernel,
        grid=(N_BIAS_TILES,),
        in_specs=[pl.BlockSpec(memory_space=pltpu.SMEM)],
        out_specs=[spec, spec, spec],
        out_shape=[shape, shape, shape],
        compiler_params=_cparams("arbitrary"),
        name="bias_tiles",
    )(rel_bias)


def _stack_heads(q):
    z = jnp.zeros((q.shape[0], HD), q.dtype)
    return jnp.concatenate([jnp.concatenate([q[:, r * HD:(r + 1) * HD], z], axis=1) for r in range(REP)],
                           axis=0)


def _tile_heads(x):
    return jnp.concatenate([x] * REP, axis=1)


def _pad_rows(x, n):
    if x.shape[0] == n:
        return x
    return jnp.concatenate([x, jnp.zeros((n - x.shape[0],) + x.shape[1:], x.dtype)], axis=0)


def _bias4(dist, tab_ref, g):
    biases = [jnp.full(dist.shape, tab_ref[0, g * REP + r], F32) for r in range(REP)]
    for kk in range(1, N_BUCKETS):
        reached = dist >= BUCKET_THR[kk - 1]
        biases = [jnp.where(reached, tab_ref[kk, g * REP + r], biases[r]) for r in range(REP)]
    return biases


def _cmp_branch(qc, kcvc_ref, tab_ref, g, pos, pos0, x_scr):
    tq = qc.shape[0]
    nb = kcvc_ref.shape[0]
    kcvc = kcvc_ref[...]
    q4 = _stack_heads(qc)
    lt = _dot_nt(kcvc, q4)
    n_io = lax.broadcasted_iota(I32, (nb, tq), 0)
    dist = pos - (n_io * BLOCK + (BLOCK - 1))
    vis = dist >= 0
    if nb <= CMP_BIAS_ROWS:
        biased = [lt[:, r * tq:(r + 1) * tq] + b for r, b in enumerate(_bias4(dist, tab_ref, g))]
    else:
        far = [tab_ref[N_BUCKETS - 1, g * REP + r] for r in range(REP)]
        x_scr[...] = jnp.concatenate([lt[:, r * tq:(r + 1) * tq] + far[r] for r in range(REP)], axis=1)
        first = (pos0 - (BUCKET_THR[-1] + BLOCK - 1)) // BLOCK + 1
        w0 = pl.multiple_of(jnp.clip(first // 8 * 8, 0, nb - CMP_BIAS_ROWS), 8)
        w_io = lax.broadcasted_iota(I32, (CMP_BIAS_ROWS, tq), 0) + w0
        near = _bias4(pos - (w_io * BLOCK + (BLOCK - 1)), tab_ref, g)
        lt_w = _dot_nt(kcvc_ref[pl.ds(w0, CMP_BIAS_ROWS), :], q4)
        x_scr[pl.ds(w0, CMP_BIAS_ROWS), :] = jnp.concatenate(
            [lt_w[:, r * tq:(r + 1) * tq] + near[r] for r in range(REP)], axis=1)
        biased = [x_scr[:, r * tq:(r + 1) * tq] for r in range(REP)]
    probs = []
    for r in range(REP):
        x = jnp.where(vis, biased[r], NEG)
        e = jnp.exp(x - jnp.max(x, axis=0, keepdims=True))
        p = e * (1.0 / jnp.sum(e, axis=0, keepdims=True))
        probs.append(jnp.where(vis, p, 0.0))
    score = probs[0] + probs[1] + probs[2] + probs[3]
    oc_t = _dot(kcvc.T[HD:, :], jnp.concatenate(probs, axis=1))
    cand = n_io < pos // BLOCK
    return oc_t, jnp.where(cand, score, -jnp.inf)


def _topk_rows(score, n_top, pick_fn=None):
    nb, tq = score.shape
    n_f = lax.broadcasted_iota(I32, (nb, tq), 0).astype(F32)
    s = score
    for it in range(n_top):
        mx = jnp.max(s, axis=0, keepdims=True)
        idx = jnp.min(jnp.where(s == mx, n_f, float(nb)), axis=0, keepdims=True)
        if pick_fn is not None:
            pick_fn(it, idx, mx > -jnp.inf)
        s = jnp.where(n_f == idx, -jnp.inf, s)
    return (s == -jnp.inf) & (score > -jnp.inf)


def _flash_init(nl):
    return jnp.full((1, nl), NEG, F32), jnp.zeros((VT_ROWS, nl), F32)


def _online_step(carry, tiles):
    m_c, acc_c = carry
    m_new = m_c
    for x2, _ in tiles:
        m_new = jnp.maximum(m_new, jnp.max(x2, axis=0, keepdims=True))
    acc = jnp.exp2(m_c - m_new) * acc_c
    for x2, vt in tiles:
        acc = acc + jnp.dot(vt, jnp.exp2(x2 - m_new).astype(BF16), preferred_element_type=F32)
    return m_new, acc


def _softmax_part(x2, vt):
    m = jnp.max(x2, axis=0, keepdims=True)
    return m, jnp.dot(vt, jnp.exp2(x2 - m).astype(BF16), preferred_element_type=F32)


def _merge_parts(a, b):
    m = jnp.maximum(a[0], b[0])
    return m, jnp.exp2(a[0] - m) * a[1] + jnp.exp2(b[0] - m) * b[1]


def _normalise(acc):
    return acc[:HD] / acc[HD:HD + 1]


def _untranspose_heads(o_t, tq):
    halves = []
    for p in range(REP // 2):
        pair = jnp.concatenate([o_t[:, (2 * p) * tq:(2 * p + 1) * tq],
                                o_t[:, (2 * p + 1) * tq:(2 * p + 2) * tq]], axis=0)
        halves.append(pair.T)
    return jnp.concatenate(halves, axis=1)


def _gate_row(gt_ref, br, g):
    return jnp.concatenate([gt_ref[pl.ds(br * NKV * REP + g * REP + r, 1), :] for r in range(REP)], axis=1)


def _attn_prompt_kernel(tab_ref, qc_ref, qs_ref, qw_ref, gt_ref, kcvc_ref, ksel_ref, vselt_ref,
                        kw0, kw1, kw2, kw3, kw4, vw0, vw1, vw2, vw3, vw4, bt_ref, btd_ref, o_ref,
                        sel_scr, xc_scr):
    g = pl.program_id(0)
    i = pl.program_id(1)
    tq = TQ
    nl = REP * tq
    t0 = i * tq
    pos = t0 + lax.broadcasted_iota(I32, (1, tq), 1)

    oc_t, score = _cmp_branch(qc_ref[...], kcvc_ref, tab_ref, g, pos, t0, xc_scr)
    nb = score.shape[0]
    sel_scr[...] = jnp.where(_topk_rows(score, TOPK - 1), 0.0, NEG)

    def bias_tile(a):
        return jnp.concatenate([bt_ref[a, r] for r in range(REP)], axis=1)

    qs_t = (_stack_heads(qs_ref[...]) * LOG2E).T[:HD].astype(BF16)
    far_bias = jnp.concatenate([jnp.full((1, tq), tab_ref[N_BUCKETS - 1, g * REP + r] * LOG2E, F32)
                                for r in range(REP)], axis=1)
    far_hi = far_bias.astype(BF16).astype(F32)
    row16 = lax.broadcasted_iota(I32, (SUPER_BLOCKS, nl), 0)
    const_rows = jnp.where(row16 == 0, far_hi, jnp.where(row16 == 1, far_bias - far_hi, 0.0)).astype(BF16)
    pad_rows = jnp.zeros((LANES - CONST_LANE - SUPER_BLOCKS, nl), BF16)

    def make_body(near):
        def sel_body(step, carry):
            tiles = []
            for u in range(SUPERS_PER_STEP):
                jj = step * SUPERS_PER_STEP + u
                mrows = _tile_heads(
                    sel_scr[pl.ds(pl.multiple_of(jj * SUPER_BLOCKS, SUPER_BLOCKS), SUPER_BLOCKS), :])
                q_aug = jnp.concatenate([qs_t, mrows.astype(BF16), const_rows, pad_rows], axis=0)
                for c in range(SUPER // KT):
                    s0 = pl.multiple_of(jj * SUPER + c * KT, KT)
                    x = jnp.dot(ksel_ref[pl.ds(s0, KT), :], q_aug, preferred_element_type=F32)
                    if near:
                        tile0 = jj * (SUPER // LANES) + c * (KT // LANES)
                        x = x + jnp.concatenate([bias_delta(jnp.clip(i - tile0 - a, 0, N_BIAS_TILES - 1))
                                                 for a in range(KT // LANES)], axis=0)
                    tiles.append((x, jnp.max(x, axis=0, keepdims=True), s0))
            parts = [(m, jnp.dot(vselt_ref[0:VT_ROWS, pl.ds(s0, KT)], jnp.exp2(x - m).astype(BF16),
                                 preferred_element_type=F32)) for x, m, s0 in tiles]
            for part in parts:
                carry = _merge_parts(carry, part)
            return carry
        return sel_body

    def bias_delta(a):
        return jnp.concatenate([btd_ref[a, r] for r in range(REP)], axis=1)

    keys_per_step = SUPER * SUPERS_PER_STEP
    n_steps = (2 * i) // (SUPER_BLOCKS * SUPERS_PER_STEP) + 1
    n_far = jnp.maximum(t0 - (N_BIAS_TILES - 1) * LANES + LANES, 0) // keys_per_step
    carry = lax.fori_loop(0, n_far, make_body(False), _flash_init(nl))
    carry = lax.fori_loop(n_far, n_steps, make_body(True), carry)

    sj = lax.broadcasted_iota(I32, (LANES, nl), 0)
    ti4 = _tile_heads(lax.broadcasted_iota(I32, (1, tq), 1))
    t0a = pl.multiple_of(t0, LANES)
    q_diag = jnp.concatenate([qs_t, jnp.zeros((SUPER_BLOCKS, nl), BF16), const_rows, pad_rows], axis=0)
    qw_bf = (_stack_heads(qw_ref[...]) * LOG2E).astype(BF16)
    kw_refs = (kw0, kw1, kw2, kw3, kw4)
    vw_refs = (vw0, vw1, vw2, vw3, vw4)
    x_diag = jnp.dot(ksel_ref[pl.ds(t0a, LANES), :], q_diag, preferred_element_type=F32)
    x_win = [_dot_nt(kw_refs[j][...], qw_bf) for j in range(5)]
    x = jnp.where((sj // BLOCK == ti4 // BLOCK) & (sj <= ti4), x_diag + bias_delta(0), NEG)
    _, accs = _online_step(carry, [(x, vselt_ref[0:VT_ROWS, pl.ds(t0a, LANES)])])

    tiles = []
    for j in range(5):
        dist = LANES * (4 - j) + ti4 - sj
        ok = jnp.where((dist >= 0) & (dist <= WINDOW), 1.0, 0.0) * jnp.where(i - 4 + j >= 0, 1.0, 0.0)
        tiles.append((jnp.where(ok > 0.5, x_win[j] + bias_tile(4 - j), NEG), vw_refs[j][0:VT_ROWS, :]))
    _, accw = _online_step(_flash_init(nl), tiles)

    out_t = (_gate_row(gt_ref, 0, g) * oc_t + _gate_row(gt_ref, 1, g) * _normalise(accs)
             + _gate_row(gt_ref, 2, g) * _normalise(accw))
    o_ref[...] = _untranspose_heads(out_t, tq)


def _attn_prompt(rel_bias, qs, gates_t, kcvc, sel_bf, selt_bf, win_bf, wint_bf, bias_tiles, bias_delta_tiles):
    t = qs.shape[0]
    nb = kcvc.shape[0]
    width = NKV * REP * HD

    def qspec(br):
        return pl.BlockSpec((TQ, REP * HD), lambda g, i, br=br: (i, br * NKV + g))

    def kw_spec(j):
        return pl.BlockSpec((LANES, LANES), lambda g, i, j=j: (jnp.maximum(i - 4 + j, 0), g))

    def vw_spec(j):
        return pl.BlockSpec((LANES, LANES), lambda g, i, j=j: (g, jnp.maximum(i - 4 + j, 0)))

    return pl.pallas_call(
        _attn_prompt_kernel,
        grid=(NKV, t // TQ),
        in_specs=[pl.BlockSpec(memory_space=pltpu.SMEM),
                  qspec(0), qspec(1), qspec(2),
                  pl.BlockSpec((LANES, TQ), lambda g, i: (0, i)),
                  pl.BlockSpec((nb, LANES), lambda g, i: (0, g)),
                  pl.BlockSpec((t, LANES), lambda g, i: (0, g)),
                  pl.BlockSpec((LANES, t), lambda g, i: (g, 0))]
                 + [kw_spec(j) for j in range(5)] + [vw_spec(j) for j in range(5)]
                 + [pl.BlockSpec((N_BIAS_TILES, REP, LANES, TQ), lambda g, i: (0, g, 0, 0))] * 2,
        out_specs=pl.BlockSpec((TQ, REP * HD), lambda g, i: (i, g)),
        out_shape=jax.ShapeDtypeStruct((t, width), F32),
        scratch_shapes=[pltpu.VMEM((nb, TQ), F32), pltpu.VMEM((nb, REP * TQ), F32)],
        compiler_params=_cparams("arbitrary", "arbitrary"),
        name="attn_prompt",
    )(rel_bias, qs, qs, qs, gates_t, kcvc, sel_bf, selt_bf,
      win_bf, win_bf, win_bf, win_bf, win_bf, wint_bf, wint_bf, wint_bf, wint_bf, wint_bf, bias_tiles,
      bias_delta_tiles)


def _attn_s1_kernel(tab_ref, qc_ref, kcvc_ref, oct_ref, idx_ref, xc_scr, *, q0):
    g = pl.program_id(1)
    tq = TQ
    pos = q0 + lax.broadcasted_iota(I32, (1, tq), 1)
    oc_t, score = _cmp_branch(_pad_rows(qc_ref[0], tq), kcvc_ref, tab_ref, g, pos, q0, xc_scr)
    oct_ref[0] = oc_t

    def pick(it, idx, ok):
        idx_ref[0, it:it + 1, :] = jnp.where(ok, idx, -1.0).astype(I32)

    _topk_rows(score, TOPK - 1, pick)
    idx_ref[0, TOPK - 1:TOPK, :] = jnp.full((1, tq), -1, I32)


def _attn_s1(rel_bias, qs8, kcvc, q0):
    batch = qs8.shape[0]
    nb = kcvc.shape[0] // batch
    return pl.pallas_call(
        functools.partial(_attn_s1_kernel, q0=q0),
        grid=(batch, NKV),
        in_specs=[pl.BlockSpec(memory_space=pltpu.SMEM),
                  pl.BlockSpec((1, 8, REP * HD), lambda b, g: (b, 0, g)),
                  pl.BlockSpec((nb, LANES), lambda b, g: (b, g))],
        out_specs=[pl.BlockSpec((1, HD, REP * TQ), lambda b, g: (b * NKV + g, 0, 0)),
                   pl.BlockSpec((1, TOPK, TQ), lambda b, g: (b * NKV + g, 0, 0))],
        out_shape=[jax.ShapeDtypeStruct((batch * NKV, HD, REP * TQ), F32),
                   jax.ShapeDtypeStruct((batch * NKV, TOPK, TQ), I32)],
        scratch_shapes=[pltpu.VMEM((nb, REP * TQ), F32)],
        compiler_params=_cparams("arbitrary", "arbitrary"),
        name="attn_sample_select",
    )(rel_bias, qs8, kcvc)


def _attn_s2_kernel(ids_ref, phys_ref, qs_ref, qw_ref, gt_ref, oct_ref, *refs, q0, nvalid):
    nsel = nvalid * (TOPK - 1)
    kb = refs[:nsel]
    snew_ref, cwin_ref, wnew_ref, bt_ref, o_ref = refs[nsel:]
    b = pl.program_id(0)
    g = pl.program_id(1)
    tq = TQ_SAMPLE
    nr = REP * tq
    tr = lax.broadcasted_iota(I32, (nr, LANES), 0) % tq
    sj = lax.broadcasted_iota(I32, (nr, LANES), 1)
    base = (b * NKV + g) * nvalid * TOPK

    def compact(x):
        return jnp.concatenate([x[:, r * TQ:r * TQ + tq] for r in range(REP)], axis=1)

    def bias_rows(a):
        return jnp.concatenate([bt_ref[a, r][:tq, :] for r in range(REP)], axis=0)

    def attend(q_bf, cached, new_ref):
        xs, vts = [], []
        for tile_t, a, ok in cached:
            xs.append(jnp.where(ok, _dot(q_bf, tile_t) + bias_rows(a), NEG))
            vts.append(tile_t.astype(BF16))
        blk = _pad_rows(new_ref[0], LANES).astype(BF16)
        xs.append(jnp.where((sj <= tr) & (sj < nvalid), _dot_nt(q_bf, blk) + bias_rows(0), NEG))
        x = jnp.concatenate(xs, axis=1)
        p = jnp.exp2(x - jnp.max(x, axis=1, keepdims=True))
        l = jnp.sum(p, axis=1, keepdims=True)
        p = p.astype(BF16)
        nk = x.shape[1] - LANES
        acc = _dot_nt(p[:, :nk], jnp.concatenate(vts, axis=1)) + _dot(p[:, nk:], blk)
        return acc[:, HD:] / l

    cached = []
    for t in range(nvalid):
        for k in range(TOPK - 1):
            n = ids_ref[base + t * TOPK + k]
            nc = jnp.maximum(n, 0)
            a = jnp.clip(q0 // LANES - nc // 2, 0, N_BIAS_TILES - 1)
            lo = jnp.where(n >= 0, (nc % 2) * BLOCK, LANES)
            cached.append((kb[t * (TOPK - 1) + k][0], a, (sj >= lo) & (sj < lo + BLOCK) & (tr == t)))
    o_sel = attend((_stack_heads(_pad_rows(qs_ref[0], tq)) * LOG2E).astype(BF16), cached, snew_ref)

    cached = []
    for j in range(WINDOW // LANES):
        dist = (WINDOW - LANES * j) + tr - sj
        cached.append((cwin_ref[0, :, j * LANES:(j + 1) * LANES], WINDOW // LANES - j,
                       (dist >= 0) & (dist <= WINDOW)))
    o_win = attend((_stack_heads(_pad_rows(qw_ref[0], tq)) * LOG2E).astype(BF16), cached, wnew_ref)

    gt2 = gt_ref.at[0]
    o_cmp = _pad_rows(compact(oct_ref[0]), LANES).T[:, :HD]
    row8 = lax.broadcasted_iota(I32, (8, nr), 0)
    g8 = jnp.zeros((8, nr), F32)
    for br in range(3):
        g8 = jnp.where(row8 == br, compact(_gate_row(gt2, br, g)), g8)
    gates = _pad_rows(g8, LANES).T
    out = gates[:, 0:1] * o_cmp + gates[:, 1:2] * o_sel + gates[:, 2:3] * o_win
    o_ref[0] = jnp.concatenate([out[r * tq:r * tq + 8] for r in range(REP)], axis=1)


def _attn_s2(ids, phys, qs8, gates_t, oct, cache_sel_t, sel_new8, cache_win_t, win_new8, bias_tiles, q0, nvalid):
    batch = qs8.shape[0]
    nsel = nvalid * (TOPK - 1)

    def kb_spec(t, k):
        return pl.BlockSpec((1, LANES, PAGE),
                            lambda b, g, ids_r, phys_r, t=t, k=k:
                            (phys_r[((b * NKV + g) * nvalid + t) * TOPK + k], g, 0))

    def bg(shape, col):
        return pl.BlockSpec(shape, lambda b, g, ids_r, phys_r, col=col: (b, 0, col * NKV + g))

    return pl.pallas_call(
        functools.partial(_attn_s2_kernel, q0=q0, nvalid=nvalid),
        grid_spec=pltpu.PrefetchScalarGridSpec(
            num_scalar_prefetch=2,
            grid=(batch, NKV),
            in_specs=[bg((1, 8, REP * HD), 1), bg((1, 8, REP * HD), 2),
                      pl.BlockSpec((1, LANES, TQ), lambda b, g, ids_r, phys_r: (b, 0, 0)),
                      pl.BlockSpec((1, HD, REP * TQ), lambda b, g, ids_r, phys_r: (b * NKV + g, 0, 0))]
                     + [kb_spec(t, k) for t in range(nvalid) for k in range(TOPK - 1)]
                     + [bg((1, 8, LANES), 0),
                        pl.BlockSpec((1, LANES, WINDOW), lambda b, g, ids_r, phys_r: (b, g, 0)),
                        bg((1, 8, LANES), 0),
                        pl.BlockSpec((N_BIAS_TILES, REP, LANES, TQ), lambda b, g, ids_r, phys_r: (0, g, 0, 0))],
            out_specs=bg((1, 8, REP * HD), 0)),
        out_shape=jax.ShapeDtypeStruct((batch, 8, NKV * REP * HD), F32),
        compiler_params=_cparams("arbitrary", "arbitrary"),
        name="attn_sample",
    )(ids, phys, qs8, qs8, gates_t, oct, *([cache_sel_t] * nsel), sel_new8, cache_win_t, win_new8, bias_tiles)


def _pad_cols(w, n):
    return jnp.pad(w, ((0, 0), (0, n - w.shape[1])))


def _prep(p):
    inner = NH_A * DK_A
    width = NKV * REP * HD
    q = {}
    q['w_in'] = _pad_cols(p['w_in_a'][0], 4 * inner + LANES).astype(BF16)
    q['b_if'] = _pad_cols(p['b_if_a'][0][None, :], LANES)
    q['w_out'] = p['w_out_a'][0].astype(BF16)
    q['w_ff1'] = [p['w_ff1'][l].astype(BF16) for l in range(2)]
    q['w_ff2'] = [p['w_ff2'][l].astype(BF16) for l in range(2)]
    q['w_kv'] = p['w_kv'].astype(BF16)
    q['w_q'] = _pad_cols(p['w_q_b'][0], 3 * width + LANES).astype(BF16)
    q['b_gate'] = _pad_cols(p['b_gate_b'][0][None, :], LANES)
    q['w_o'] = p['w_o_b'][0].astype(BF16)
    q['q_gain'] = (jnp.tile(p['q_norm_b'][0][:, None, :], (1, NKV * REP, 1)) * (HD ** -0.5)).reshape(1, 3 * width)
    seg = np.arange(256) // HD
    q['bd'] = jnp.asarray((seg[:, None] == seg[None, :]).astype(np.float32) / HD, dtype=BF16)
    ones = jnp.ones((HD,), F32)
    q['k_gain'] = jnp.concatenate([jnp.tile(jnp.concatenate([p['k_norm'][br], ones]), NKV) for br in (1, 2)])[None, :]
    q['k_mask'] = jnp.tile(jnp.concatenate([ones, 0.0 * ones]), 2 * NKV)[None, :]
    q['k_gain0'] = jnp.concatenate([p['k_norm'][0], ones])[None, :]
    w1 = p['w_cmp1']
    z = jnp.zeros_like(w1[0])
    per_l = jnp.concatenate([jnp.concatenate([w1[0], z], axis=-1), jnp.concatenate([z, w1[1]], axis=-1)], axis=1)
    q['w_cmp1'] = per_l.reshape(BLOCK // 2, 2 * 2 * HD, 2 * w1.shape[-1]).astype(BF16)
    w2 = p['w_cmp2']
    z2 = jnp.zeros_like(w2[0])
    q['w_cmp2'] = jnp.concatenate([jnp.concatenate([w2[0], z2], axis=1), jnp.concatenate([z2, w2[1]], axis=1)],
                                  axis=0).astype(BF16)
    q['pos_rows'] = jnp.tile(p['cmp_pos'].reshape(BLOCK, 2 * HD), (1, NKV))
    return q


def _feature_major(cache):
    n, t = cache.shape[:2]
    return jnp.transpose(cache, (0, 2, 3, 4, 1)).reshape(n, -1, t)


def _mods(ada_rows, rep):
    k = ada_rows.shape[1] // D_MODEL
    out = []
    for j in range(k):
        a = ada_rows[:, j * D_MODEL:(j + 1) * D_MODEL]
        out.append(jnp.repeat(a, rep, axis=0) if rep > 1 else a)
    return out


def _layer0(x, mods, p, q, mlstm_fn, tm, tmf):
    sh1, sc1, g1, sh2, sc2, g2 = mods
    proj = _proj_plain(x, p['norm_mix'][0][None, :], sh1, sc1, q['w_in'], tm)
    y, states = mlstm_fn(proj)
    x = _out_proj(x, g1, y, q['w_out'], tm)
    x = _ffn(x, p['norm_ffn'][0][None, :], sh2, sc2, g2, q['w_ff1'][0], q['w_ff2'][0], tmf)
    return x, states


def _layer1_tail(x, attn, mods, p, q, tm, tmf):
    _, _, g1, sh2, sc2, g2 = mods
    x = _out_proj(x, g1, attn, q['w_o'], tm)
    return _ffn(x, p['norm_ffn'][1][None, :], sh2, sc2, g2, q['w_ff1'][1], q['w_ff2'][1], tmf)


def kernel(x_prompt, x_sample, cache_cmp, cache_sel, cache_win, state_C, state_n, state_m, page_table,
           c_prompt, c_sample, w_ada, b_ada, norm_mix, norm_ffn, w_ff1, w_ff2, w_in_a, b_if_a,
           head_norm_a, w_out_a, w_ada_kv, b_ada_kv, norm_kv, w_kv, k_norm, cmp_pos, w_cmp1, w_cmp2,
           w_q_b, b_gate_b, q_norm_b, w_o_b, rel_bias):
    p = dict(norm_mix=norm_mix, norm_ffn=norm_ffn, w_ff1=w_ff1, w_ff2=w_ff2, w_in_a=w_in_a, b_if_a=b_if_a,
             w_out_a=w_out_a, w_kv=w_kv, k_norm=k_norm, cmp_pos=cmp_pos, w_cmp1=w_cmp1, w_cmp2=w_cmp2,
             w_q_b=w_q_b, b_gate_b=b_gate_b, q_norm_b=q_norm_b, w_o_b=w_o_b)
    q = _prep(p)
    bp, tp, d = x_prompt.shape
    bs, ts, _ = x_sample.shape
    past = page_table.shape[1] * PAGE
    wbuf = cache_win.shape[1]
    assert bp == 1 and ts < BLOCK and ts <= 8 and wbuf == WINDOW and past % LANES == 0
    assert tp % (SUPER * SUPERS_PER_STEP) == 0
    width = NKV * REP * HD
    kvw = NKV * 2 * HD

    nc = bp + bs
    c_all = jnp.pad(jnp.concatenate([c_prompt, c_sample], axis=0), ((0, -nc % 8), (0, 0)))
    ada = [_ada(c_all, w_ada[l], b_ada[l]) for l in range(2)]
    ada_kv = _ada(c_all, w_ada_kv, b_ada_kv)
    bias_kq, bias_kq_delta, bias_qk = _bias_tiles(rel_bias)
    head_gain = head_norm_a[0]

    xp = x_prompt.reshape(tp, d)
    zc = jnp.zeros((bp, NH_A, DK_A, DK_A), F32)
    zn = jnp.zeros((bp, NH_A, 1, DK_A), F32)
    zm = jnp.zeros((bp, 8, LANES), F32)

    def mlstm_prompt(proj):
        y, c, n, m = _mlstm(proj.reshape(bp, tp, -1), q['b_if'], head_gain, zc, zn, zm,
                            rows=256, valid=256, chunk=256, zero_init=True)
        return y.reshape(tp, -1), (c, n, m)

    xp, (pc, pn, pm) = _layer0(xp, _mods(ada[0][:bp], 1), p, q, mlstm_prompt, 256, 512)
    sh, sc = _mods(ada_kv[:bp], 1)
    p_cmp, _, _, p_cmp_t, p_sel_t, p_win_t, sel_bf, selt_bf, win_bf, wint_bf = _proj_kv(
        xp, norm_kv[None, :], sh, sc, q['w_kv'], q['bd'], q['k_gain'], q['k_mask'], 256)
    kcvc_p = _compress(p_cmp, q['pos_rows'], q['w_cmp1'], q['w_cmp2'], q['k_gain0'],
                       min(CMP_BLOCKS_PER_STEP, tp // BLOCK))
    mods1 = _mods(ada[1][:bp], 1)
    qs_p, gt_p = _proj_q(xp, norm_mix[1][None, :], mods1[0], mods1[1], q['w_q'], q['bd'], q['q_gain'],
                         q['b_gate'], 256)
    attn_p = _attn_prompt(rel_bias, qs_p, gt_p, kcvc_p, sel_bf, selt_bf, win_bf, wint_bf, bias_kq, bias_kq_delta)
    y_prompt = _layer1_tail(xp, attn_p, mods1, p, q, 256, 512).reshape(bp, tp, d)

    ms = bs * ts
    xs = x_sample.reshape(ms, d)
    m0 = jnp.broadcast_to(jnp.pad(state_m[0], ((0, 0), (0, 8 - NH_A)))[:, :, None], (bs, 8, LANES))

    def mlstm_sample(proj):
        proj8 = jnp.pad(proj.reshape(bs, ts, -1), ((0, 0), (0, 8 - ts), (0, 0)))
        y, c, n, m = _mlstm(proj8, q['b_if'], head_gain, state_C[0], state_n[0][:, :, None, :], m0,
                            rows=8, valid=ts, chunk=LANES, zero_init=False)
        return y[:, :ts].reshape(ms, -1), (c, n, m)

    xs, (sc_, sn_, sm_) = _layer0(xs, _mods(ada[0][bp:nc], ts), p, q, mlstm_sample, ms, ms)
    sh, sc = _mods(ada_kv[bp:nc], ts)
    s_cmp, s_sel, s_win = _proj_kv(xs, norm_kv[None, :], sh, sc, q['w_kv'], q['bd'], q['k_gain'],
                                   q['k_mask'], ms)[:3]
    kcvc_s = _compress_paged(_feature_major(cache_cmp), page_table, q['pos_rows'],
                             q['w_cmp1'], q['w_cmp2'], q['k_gain0'], min(CMP_BLOCKS_PER_STEP, past // BLOCK))
    mods1 = _mods(ada[1][bp:nc], ts)
    qs_s, gt_s = _proj_q(xs, norm_mix[1][None, :], mods1[0], mods1[1], q['w_q'], q['bd'], q['q_gain'],
                         q['b_gate'], ms)

    def pad8(a):
        return jnp.pad(a.reshape(bs, ts, -1), ((0, 0), (0, 8 - ts), (0, 0)))

    qs8 = pad8(qs_s)
    oct_s, idx_s = _attn_s1(rel_bias, qs8, kcvc_s, past)
    ids = jnp.transpose(idx_s.reshape(bs, NKV, TOPK, TQ)[:, :, :, :ts], (0, 1, 3, 2))
    idc = jnp.maximum(ids, 0)
    pages = jnp.take_along_axis(page_table, (idc // (PAGE // BLOCK)).reshape(bs, -1), axis=1).reshape(ids.shape)
    gt8 = jnp.pad(jnp.transpose(gt_s.reshape(LANES, bs, ts), (1, 0, 2)), ((0, 0), (0, 0), (0, TQ - ts)))
    attn_s = _attn_s2(ids.reshape(-1), pages.reshape(-1), qs8, gt8, oct_s,
                      _feature_major(cache_sel), pad8(s_sel), _feature_major(cache_win), pad8(s_win),
                      bias_qk, past, ts)
    y_sample = _layer1_tail(xs, attn_s[:, :ts].reshape(ms, width), mods1, p, q, ms, ms).reshape(bs, ts, d)

    rows5 = (NKV, 2, HD)

    def token_major(a_t):
        return jnp.transpose(a_t.reshape(*rows5, a_t.shape[1]), (3, 0, 1, 2))[None]

    s_win_all = jnp.concatenate([cache_win, s_win.reshape(bs, ts, *rows5)], axis=1)
    return (y_prompt, y_sample,
            pc[None], pn.reshape(1, bp, NH_A, DK_A), pm[None, :, :NH_A, 0],
            token_major(p_cmp_t), token_major(p_sel_t), token_major(p_win_t[:, tp - min(WINDOW, tp):]),
            sc_[None], sn_.reshape(1, bs, NH_A, DK_A), sm_[None, :, :NH_A, 0],
            s_cmp.reshape(bs, ts, *rows5), s_sel.reshape(bs, ts, *rows5),
            s_win_all[:, -min(WINDOW, wbuf + ts):])
```

```python
import functools
import math

import numpy as np
import jax
import jax.numpy as jnp
from jax import lax
from jax.experimental import pallas as pl
from jax.experimental.pallas import tpu as pltpu

F32 = jnp.float32
BF16 = jnp.bfloat16
I32 = jnp.int32

D_MODEL = 1024
NH_A = 4
DK_A = 256
NKV = 4
REP = 4
HD = 64
BLOCK = 64
TOPK = 16
WINDOW = 512
N_BUCKETS = 32
REL_MAX_DIST = 2048
PAGE = 128
EPS = 1e-6
NEG = -1e30
LOG2E = math.log2(math.e)

LANES = 128
VMEM_LIMIT = 56 * 1024 * 1024
TQ = 128
TQ_SAMPLE = 32
KT = 256
SUPER_BLOCKS = 16
SUPER = SUPER_BLOCKS * BLOCK
SUPERS_PER_STEP = 2
VT_ROWS = HD + 16
CONST_LANE = HD + SUPER_BLOCKS
N_BIAS_TILES = 14
CMP_BLOCKS_PER_STEP = 64
CMP_BIAS_ROWS = 40
CMP_ROW_CLASSES = (64, 128)


def _bucket_thresholds():
    exact = N_BUCKETS // 2
    d = np.arange(1, 4 * REL_MAX_DIST, dtype=np.float64)
    big = exact + np.floor(np.log(d / exact) / math.log(REL_MAX_DIST / exact) * (N_BUCKETS - exact)).astype(np.int64)
    b = np.where(d < exact, d.astype(np.int64), np.minimum(big, N_BUCKETS - 1))
    return [int(d[np.argmax(b >= k)]) for k in range(1, N_BUCKETS)]


BUCKET_THR = _bucket_thresholds()
assert 128 * (N_BIAS_TILES - 1) - (TQ - 1) >= BUCKET_THR[-1]


def _cparams(*sem):
    return pltpu.CompilerParams(dimension_semantics=sem, vmem_limit_bytes=VMEM_LIMIT)


def _dot(a, b):
    return jnp.dot(a.astype(BF16), b.astype(BF16), preferred_element_type=F32)


def _dot_nt(a, b):
    return lax.dot_general(a.astype(BF16), b.astype(BF16), (((1,), (1,)), ((), ())),
                           preferred_element_type=F32)


def _split3(x):
    hi = x.astype(BF16)
    r1 = x - hi.astype(F32)
    mid = r1.astype(BF16)
    lo = (r1 - mid.astype(F32)).astype(BF16)
    return hi, mid, lo


def _sigmoid(x):
    return 1.0 / (1.0 + jnp.exp(-x))


def _norm_mod(x, gain, shift, scale):
    ms = jnp.mean(x * x, axis=-1, keepdims=True)
    y = x * lax.rsqrt(ms + EPS) * gain
    return y * (1.0 + scale) + shift


def _seg_mean_sq(y, bd):
    parts = []
    for j in range(y.shape[1] // 256):
        sq = y[:, j * 256:(j + 1) * 256]
        sq = sq * sq
        hi = sq.astype(BF16)
        lo = (sq - hi.astype(F32)).astype(BF16)
        parts.append(jnp.dot(hi, bd, preferred_element_type=F32) + jnp.dot(lo, bd, preferred_element_type=F32))
    return parts[0] if len(parts) == 1 else jnp.concatenate(parts, axis=1)


def _row_spec(tm, n, per_row):
    if per_row:
        return pl.BlockSpec((tm, n), lambda i: (i, 0))
    return pl.BlockSpec((1, n), lambda i: (0, 0))


def _const_spec(shape):
    return pl.BlockSpec(shape, lambda i: tuple(0 for _ in shape))


def _ada_kernel(c_ref, w_ref, b_ref, o_ref):
    c = c_ref[...]
    o_ref[...] = _dot(c * _sigmoid(c), w_ref[...]) + b_ref[...]


def _ada(c, w, b):
    m, k = c.shape
    n = w.shape[1]
    tn = 1024
    return pl.pallas_call(
        _ada_kernel,
        grid=(n // tn,),
        in_specs=[pl.BlockSpec((m, k), lambda j: (0, 0)),
                  pl.BlockSpec((k, tn), lambda j: (0, j)),
                  pl.BlockSpec((1, tn), lambda j: (0, j))],
        out_specs=pl.BlockSpec((m, tn), lambda j: (0, j)),
        out_shape=jax.ShapeDtypeStruct((m, n), F32),
        compiler_params=_cparams("arbitrary"),
        name="ada",
    )(c, w, b.reshape(1, n))


def _proj_plain_kernel(x_ref, gain_ref, sh_ref, sc_ref, w_ref, o_ref):
    xn = _norm_mod(x_ref[...], gain_ref[...], sh_ref[...], sc_ref[...])
    o_ref[...] = jnp.dot(xn.astype(BF16), w_ref[...], preferred_element_type=F32)


def _proj_plain(x, gain, shift, scale, w_bf, tm):
    m, d = x.shape
    n = w_bf.shape[1]
    per_row = shift.shape[0] != 1
    return pl.pallas_call(
        _proj_plain_kernel,
        grid=(m // tm,),
        in_specs=[pl.BlockSpec((tm, d), lambda i: (i, 0)),
                  _const_spec((1, d)),
                  _row_spec(tm, d, per_row), _row_spec(tm, d, per_row),
                  _const_spec((d, n))],
        out_specs=pl.BlockSpec((tm, n), lambda i: (i, 0)),
        out_shape=jax.ShapeDtypeStruct((m, n), F32),
        compiler_params=_cparams("arbitrary"),
        name="proj_mlstm",
    )(x, gain, shift, scale, w_bf)


def _proj_q_kernel(x_ref, gain_ref, sh_ref, sc_ref, w_ref, bd_ref, qg_ref, bg_ref, q_ref, gt_ref):
    xn = _norm_mod(x_ref[...], gain_ref[...], sh_ref[...], sc_ref[...])
    y = jnp.dot(xn.astype(BF16), w_ref[...], preferred_element_type=F32)
    nq = q_ref.shape[1]
    yq = y[:, :nq]
    ms = _seg_mean_sq(yq, bd_ref[...])
    q_ref[...] = yq * lax.rsqrt(ms + EPS) * qg_ref[...]
    gates = _sigmoid(y[:, nq:] + bg_ref[...])
    gt_ref[...] = gates.T


def _proj_q(x, gain, shift, scale, w_bf, bd, qgain_row, bgate_row, tm):
    m, d = x.shape
    n = w_bf.shape[1]
    nq = n - LANES
    per_row = shift.shape[0] != 1
    return pl.pallas_call(
        _proj_q_kernel,
        grid=(m // tm,),
        in_specs=[pl.BlockSpec((tm, d), lambda i: (i, 0)),
                  _const_spec((1, d)),
                  _row_spec(tm, d, per_row), _row_spec(tm, d, per_row),
                  _const_spec((d, n)), _const_spec((256, 256)),
                  _const_spec((1, nq)), _const_spec((1, LANES))],
        out_specs=[pl.BlockSpec((tm, nq), lambda i: (i, 0)),
                   pl.BlockSpec((LANES, tm), lambda i: (0, i))],
        out_shape=[jax.ShapeDtypeStruct((m, nq), F32),
                   jax.ShapeDtypeStruct((LANES, m), F32)],
        compiler_params=_cparams("arbitrary"),
        name="proj_q",
    )(x, gain, shift, scale, w_bf, bd, qgain_row, bgate_row)


def _proj_kv_kernel(x_ref, gain_ref, sh_ref, sc_ref, w_ref, bd_ref, kg_ref, km_ref,
                    cmp_ref, sel_ref, win_ref, cmpt_ref, selt_ref, wint_ref,
                    selb_ref, seltb_ref, winb_ref, wintb_ref):
    xn = _norm_mod(x_ref[...], gain_ref[...], sh_ref[...], sc_ref[...])
    y = jnp.dot(xn.astype(BF16), w_ref[...], preferred_element_type=F32)
    w = cmp_ref.shape[1]
    cmp = y[:, :w]
    ykn = y[:, w:]
    ms = _seg_mean_sq(ykn, bd_ref[...])
    ykn = jnp.where(km_ref[...] > 0.5, ykn * lax.rsqrt(ms + EPS) * kg_ref[...], ykn)
    sel = ykn[:, :w]
    win = ykn[:, w:]
    sel_t = sel.T
    win_t = win.T
    cmp_ref[...] = cmp
    sel_ref[...] = sel
    win_ref[...] = win
    cmpt_ref[...] = cmp.T
    selt_ref[...] = sel_t
    wint_ref[...] = win_t
    tm = sel.shape[0]
    tok = pl.program_id(0) * tm + lax.broadcasted_iota(I32, sel.shape, 0)
    lane = lax.broadcasted_iota(I32, sel.shape, 1) % LANES
    onehot = jnp.where((lane - HD == (tok // BLOCK) % SUPER_BLOCKS) | (lane == CONST_LANE)
                       | (lane == CONST_LANE + 1), 1.0, 0.0)
    selb_ref[...] = jnp.where(lane < HD, sel, onehot).astype(BF16)
    winb_ref[...] = win.astype(BF16)

    def value_rows(x_t):
        ones_blk = jnp.where(lax.broadcasted_iota(I32, (HD, tm), 0) == 0, 1.0, 0.0)
        return jnp.concatenate(
            [piece for gg in range(NKV) for piece in (x_t[gg * LANES + HD:(gg + 1) * LANES], ones_blk)],
            axis=0).astype(BF16)

    seltb_ref[...] = value_rows(sel_t)
    wintb_ref[...] = value_rows(win_t)


def _proj_kv(x, gain, shift, scale, w_bf, bd, kgain_row, kmask_row, tm):
    m, d = x.shape
    n = w_bf.shape[1]
    w = n // 3
    per_row = shift.shape[0] != 1
    row = pl.BlockSpec((tm, w), lambda i: (i, 0))
    col = pl.BlockSpec((w, tm), lambda i: (0, i))
    return pl.pallas_call(
        _proj_kv_kernel,
        grid=(m // tm,),
        in_specs=[pl.BlockSpec((tm, d), lambda i: (i, 0)),
                  _const_spec((1, d)),
                  _row_spec(tm, d, per_row), _row_spec(tm, d, per_row),
                  _const_spec((d, n)), _const_spec((256, 256)),
                  _const_spec((1, 2 * w)), _const_spec((1, 2 * w))],
        out_specs=[row, row, row, col, col, col, row, col, row, col],
        out_shape=[jax.ShapeDtypeStruct((m, w), F32)] * 3 + [jax.ShapeDtypeStruct((w, m), F32)] * 3
                  + [jax.ShapeDtypeStruct((m, w), BF16), jax.ShapeDtypeStruct((w, m), BF16),
                     jax.ShapeDtypeStruct((m, w), BF16), jax.ShapeDtypeStruct((w, m), BF16)],
        compiler_params=_cparams("arbitrary"),
        name="proj_kv",
    )(x, gain, shift, scale, w_bf, bd, kgain_row, kmask_row)


def _out_proj_kernel(x_ref, g_ref, y_ref, w_ref, o_ref):
    o_ref[...] = x_ref[...] + g_ref[...] * jnp.dot(y_ref[...].astype(BF16), w_ref[...],
                                                   preferred_element_type=F32)


def _out_proj(x, g, y, w_bf, tm):
    m, d = x.shape
    k = y.shape[1]
    per_row = g.shape[0] != 1
    return pl.pallas_call(
        _out_proj_kernel,
        grid=(m // tm,),
        in_specs=[pl.BlockSpec((tm, d), lambda i: (i, 0)),
                  _row_spec(tm, d, per_row),
                  pl.BlockSpec((tm, k), lambda i: (i, 0)),
                  _const_spec((k, d))],
        out_specs=pl.BlockSpec((tm, d), lambda i: (i, 0)),
        out_shape=jax.ShapeDtypeStruct((m, d), F32),
        compiler_params=_cparams("arbitrary"),
        name="out_proj",
    )(x, g, y, w_bf)


def _ffn_kernel(x_ref, gain_ref, sh_ref, sc_ref, g_ref, w1_ref, w2_ref, o_ref, *, fc):
    x = x_ref[...]
    xn = _norm_mod(x, gain_ref[...], sh_ref[...], sc_ref[...]).astype(BF16)
    acc = jnp.zeros(x.shape, F32)
    for c in range(w1_ref.shape[1] // fc):
        h = jnp.dot(xn, w1_ref[:, c * fc:(c + 1) * fc], preferred_element_type=F32)
        h = jnp.maximum(h, 0.0)
        acc = acc + jnp.dot((h * h).astype(BF16), w2_ref[c * fc:(c + 1) * fc, :],
                            preferred_element_type=F32)
    o_ref[...] = x + g_ref[...] * acc


def _ffn(x, gain, shift, scale, g, w1_bf, w2_bf, tm):
    m, d = x.shape
    f = w1_bf.shape[1]
    per_row = shift.shape[0] != 1
    return pl.pallas_call(
        functools.partial(_ffn_kernel, fc=1024),
        grid=(m // tm,),
        in_specs=[pl.BlockSpec((tm, d), lambda i: (i, 0)),
                  _const_spec((1, d)),
                  _row_spec(tm, d, per_row), _row_spec(tm, d, per_row), _row_spec(tm, d, per_row),
                  _const_spec((d, f)), _const_spec((f, d))],
        out_specs=pl.BlockSpec((tm, d), lambda i: (i, 0)),
        out_shape=jax.ShapeDtypeStruct((m, d), F32),
        compiler_params=_cparams("arbitrary"),
        name="ffn",
    )(x, gain, shift, scale, g, w1_bf, w2_bf)


def _mlstm_kernel(q_ref, k_ref, v_ref, o_ref, gt_ref, bif_ref, hg_ref, c0_ref, n0_ref, m0_ref,
                  y_ref, cout_ref, nout_ref, mout_ref, c_scr, n_scr, m_scr, *, rows, valid, chunk, zero_init):
    ci = pl.program_id(1)
    nci = pl.num_programs(1)

    @pl.when(ci == 0)
    def _():
        if zero_init:
            c_scr[...] = jnp.zeros(c_scr.shape, F32)
            n_scr[...] = jnp.zeros(n_scr.shape, F32)
            m_scr[...] = jnp.zeros(m_scr.shape, F32)
        else:
            c_scr[...] = c0_ref[0]
            n_scr[...] = n0_ref[0]
            m_scr[...] = m0_ref[0]

    L = chunk

    def padded(ref_val, fill):
        if rows == L:
            return ref_val
        pad = jnp.full((L - rows, ref_val.shape[1]), fill, ref_val.dtype)
        return jnp.concatenate([ref_val, pad], axis=0)

    q = padded(q_ref[0], 0.0)
    k = padded(k_ref[0], 0.0)
    v = padded(v_ref[0], 0.0)
    og = padded(o_ref[0], 0.0)
    g = padded(gt_ref[0], 0.0) + bif_ref[...]
    lf = -(jnp.maximum(-g, 0.0) + jnp.log1p(jnp.exp(-jnp.abs(g))))
    li = g
    if valid != L:
        is_real = lax.broadcasted_iota(I32, (L, LANES), 0) < valid
        lf = jnp.where(is_real, lf, 0.0)
        li = jnp.where(is_real, li, NEG)
    r_io = lax.broadcasted_iota(I32, (L, L), 0)
    c_io = lax.broadcasted_iota(I32, (L, L), 1)
    causal = c_io <= r_io
    tril = jnp.where(causal, 1.0, 0.0).astype(BF16)
    hi, mid, lo = _split3(lf)
    b = (jnp.dot(tril, hi, preferred_element_type=F32) + jnp.dot(tril, mid, preferred_element_type=F32)
         + jnp.dot(tril, lo, preferred_element_type=F32))
    lane = lax.broadcasted_iota(I32, (L, LANES), 1)
    mixed_t = jnp.where(lane < NH_A, li, b).T
    for h in range(NH_A):
        sl = slice(h * DK_A, (h + 1) * DK_A)
        qh = q[:, sl] * (DK_A ** -0.5)
        kh = k[:, sl]
        vh = v[:, sl]
        b_col = b[:, NH_A + h:NH_A + h + 1]
        li_col = li[:, h:h + 1]
        b_row = mixed_t[NH_A + h:NH_A + h + 1, :]
        li_row = mixed_t[h:h + 1, :]
        m_prev = m_scr[h:h + 1, 0:1]
        c_prev = c_scr[h]
        n_prev = n_scr[h]
        dmat = jnp.where(causal, b_col - b_row + li_row, NEG)
        m_inter = b_col + m_prev
        m_t = jnp.maximum(m_inter, jnp.max(dmat, axis=-1, keepdims=True))
        s = _dot_nt(qh, kh) * jnp.exp(dmat - m_t)
        a_inter = jnp.exp(m_inter - m_t)
        num = _dot(s, vh) + a_inter * _dot(qh, c_prev)
        den = jnp.sum(s, axis=-1, keepdims=True) + a_inter * jnp.sum(qh * n_prev, axis=-1, keepdims=True)
        hh = num / jnp.maximum(jnp.abs(den), jnp.exp(-m_t))
        b_last = b_col[L - 1:L, :]
        g_col = b_last - b_col + li_col
        m_new = jnp.maximum(b_last + m_prev, jnp.max(g_col, axis=0, keepdims=True))
        w_col = jnp.exp(g_col - m_new)
        decay = jnp.exp(b_last + m_prev - m_new)
        kw = kh * w_col
        c_scr[h] = decay * c_prev + _dot(kw.T, vh)
        n_scr[h] = decay * n_prev + jnp.sum(kw, axis=0, keepdims=True)
        m_scr[h:h + 1, :] = jnp.broadcast_to(m_new, (1, LANES))
        hn = hh * lax.rsqrt(jnp.mean(hh * hh, axis=-1, keepdims=True) + EPS) * hg_ref[h:h + 1, :]
        yh = hn * _sigmoid(og[:, sl])
        y_ref[0, :, sl] = yh[:rows]

    @pl.when(ci == nci - 1)
    def _():
        cout_ref[0] = c_scr[...]
        nout_ref[0] = n_scr[...]
        mout_ref[0] = m_scr[...]


def _mlstm(proj, b_if_row, head_gain, c0, n0, m0, *, rows, valid, chunk, zero_init):
    batch, t, _ = proj.shape
    nch = t // rows
    inner = NH_A * DK_A

    def colblk(j, width):
        return pl.BlockSpec((1, rows, width), lambda b, c, j=j: (b, c, j))

    state_c = pl.BlockSpec((1, NH_A, DK_A, DK_A), lambda b, c: (b, 0, 0, 0))
    state_n = pl.BlockSpec((1, NH_A, 1, DK_A), lambda b, c: (b, 0, 0, 0))
    state_m = pl.BlockSpec((1, 8, LANES), lambda b, c: (b, 0, 0))
    return pl.pallas_call(
        functools.partial(_mlstm_kernel, rows=rows, valid=valid, chunk=chunk, zero_init=zero_init),
        grid=(batch, nch),
        in_specs=[colblk(0, inner), colblk(1, inner), colblk(2, inner), colblk(3, inner),
                  colblk(4 * inner // LANES, LANES),
                  pl.BlockSpec((1, LANES), lambda b, c: (0, 0)),
                  pl.BlockSpec((NH_A, DK_A), lambda b, c: (0, 0)),
                  state_c, state_n, state_m],
        out_specs=[pl.BlockSpec((1, rows, inner), lambda b, c: (b, c, 0)), state_c, state_n, state_m],
        out_shape=[jax.ShapeDtypeStruct((batch, t, inner), F32),
                   jax.ShapeDtypeStruct((batch, NH_A, DK_A, DK_A), F32),
                   jax.ShapeDtypeStruct((batch, NH_A, 1, DK_A), F32),
                   jax.ShapeDtypeStruct((batch, 8, LANES), F32)],
        scratch_shapes=[pltpu.VMEM((NH_A, DK_A, DK_A), F32), pltpu.VMEM((NH_A, 1, DK_A), F32),
                        pltpu.VMEM((8, LANES), F32)],
        compiler_params=_cparams("arbitrary", "arbitrary"),
        name="mlstm",
    )(proj, proj, proj, proj, proj, b_if_row, head_gain, c0, n0, m0)


def _compress_body(r_refs, pos_ref, w1_ref, w2_ref, kg_ref, o_ref, nb):
    acc = None
    for lp in range(BLOCK // 2):
        pieces = []
        for g in range(NKV):
            xa = r_refs[g][pl.ds(2 * lp, nb, stride=BLOCK), :]
            xb = r_refs[g][pl.ds(2 * lp + 1, nb, stride=BLOCK), :]
            pieces.append(jnp.concatenate([xa, xb], axis=1))
        pa = pos_ref[2 * lp:2 * lp + 1, 0:LANES]
        pb = pos_ref[2 * lp + 1:2 * lp + 2, 0:LANES]
        pieces.append(jnp.broadcast_to(jnp.concatenate([pa, pb], axis=1), (8, 2 * LANES)))
        x = jnp.concatenate(pieces, axis=0).astype(BF16)
        d = jnp.dot(x, w1_ref[lp], preferred_element_type=F32)
        acc = d if acc is None else acc + d
    pre = acc[:NKV * nb] + acc[NKV * nb:NKV * nb + 1]
    hid = pre * _sigmoid(pre)
    out = jnp.dot(hid.astype(BF16), w2_ref[...], preferred_element_type=F32)
    is_k = lax.broadcasted_iota(I32, out.shape, 1) < HD
    ms = jnp.sum(jnp.where(is_k, out * out, 0.0), axis=-1, keepdims=True) * (1.0 / HD)
    out = jnp.where(is_k, out * lax.rsqrt(ms + EPS) * kg_ref[...], out)
    for g in range(NKV):
        o_ref[:, g * LANES:(g + 1) * LANES] = out[g * nb:(g + 1) * nb]


def _compress_kernel(r0, r1, r2, r3, pos_ref, w1_ref, w2_ref, kg_ref, o_ref, *, nb):
    _compress_body((r0, r1, r2, r3), pos_ref, w1_ref, w2_ref, kg_ref, o_ref, nb)


def _compress_paged_kernel(pt_ref, *refs, nb):
    npages = nb * BLOCK // PAGE
    pages = refs[:npages]
    pos_ref, w1_ref, w2_ref, kg_ref, o_ref = refs[npages:npages + 5]
    r_scrs = refs[npages + 5:]
    for p in range(npages):
        for g in range(NKV):
            r_scrs[g][p * PAGE:(p + 1) * PAGE, :] = pages[p][0, g * LANES:(g + 1) * LANES, :].T
    _compress_body(r_scrs, pos_ref, w1_ref, w2_ref, kg_ref, o_ref, nb)


def _compress_paged(cache_t, page_table, pos_rows, w1p, w2p, kgain_row, nb):
    batch, npg = page_table.shape
    w = cache_t.shape[1]
    npages = nb * BLOCK // PAGE
    steps = npg // npages

    def page_spec(p):
        return pl.BlockSpec((1, w, PAGE), lambda b, j, pt, p=p: (pt[b, j * npages + p], 0, 0))

    def const(shape):
        return pl.BlockSpec(shape, lambda b, j, pt: tuple(0 for _ in shape))

    return pl.pallas_call(
        functools.partial(_compress_paged_kernel, nb=nb),
        grid_spec=pltpu.PrefetchScalarGridSpec(
            num_scalar_prefetch=1,
            grid=(batch, steps),
            in_specs=[page_spec(p) for p in range(npages)]
                     + [const((BLOCK, w)), const(w1p.shape), const(w2p.shape), const((1, LANES))],
            out_specs=pl.BlockSpec((nb, w), lambda b, j, pt: (b * steps + j, 0)),
            scratch_shapes=[pltpu.VMEM((nb * BLOCK, LANES), F32) for _ in range(NKV)]),
        out_shape=jax.ShapeDtypeStruct((batch * npg * PAGE // BLOCK, w), F32),
        compiler_params=_cparams("arbitrary", "arbitrary"),
        name="compress_paged",
    )(page_table, *([cache_t] * npages), pos_rows, w1p, w2p, kgain_row)


def _compress(rows, pos_rows, w1p, w2p, kgain_row, nb):
    t, w = rows.shape
    nblk = t // BLOCK
    return pl.pallas_call(
        functools.partial(_compress_kernel, nb=nb),
        grid=(nblk // nb,),
        in_specs=[pl.BlockSpec((nb * BLOCK, LANES), lambda i, g=g: (i, g)) for g in range(NKV)]
                 + [_const_spec((BLOCK, w)), _const_spec(w1p.shape), _const_spec(w2p.shape),
                    _const_spec((1, LANES))],
        out_specs=pl.BlockSpec((nb, w), lambda i: (i, 0)),
        out_shape=jax.ShapeDtypeStruct((nblk, w), F32),
        compiler_params=_cparams("arbitrary"),
        name="compress",
    )(rows, rows, rows, rows, pos_rows, w1p, w2p, kgain_row)


def _bias_from_dist(dist, tab_ref, h):
    out = jnp.full(dist.shape, tab_ref[0, h], F32)
    for kk in range(1, N_BUCKETS):
        out = jnp.where(dist >= BUCKET_THR[kk - 1], tab_ref[kk, h], out)
    return out


def _bias_tiles_kernel(tab_ref, kq_ref, kqd_ref, qk_ref):
    a = pl.program_id(0)
    row = lax.broadcasted_iota(I32, (LANES, TQ), 0)
    col = lax.broadcasted_iota(I32, (LANES, TQ), 1)

    def head(h, _):
        kq = _bias_from_dist(a * LANES + col - row, tab_ref, h)
        kq_ref[0, h] = kq * LOG2E
        kqd_ref[0, h] = (kq - tab_ref[N_BUCKETS - 1, h]) * LOG2E
        qk_ref[0, h] = _bias_from_dist(a * LANES + row - col, tab_ref, h) * LOG2E
        return 0

    lax.fori_loop(0, kq_ref.shape[1], head, 0)


def _bias_tiles(rel_bias):
    nh = rel_bias.shape[1]
    spec = pl.BlockSpec((1, nh, LANES, TQ), lambda a: (a, 0, 0, 0))
    shape = jax.ShapeDtypeStruct((N_BIAS_TILES, nh, LANES, TQ), F32)
    return pl.pallas_call(
        _bias_tiles_kernel,
        grid=(N_BIAS_TILES,),
        in_specs=[pl.BlockSpec(memory_space=pltpu.SMEM)],
        out_specs=[spec, spec, spec],
        out_shape=[shape, shape, shape],
        compiler_params=_cparams("arbitrary"),
        name="bias_tiles",
    )(rel_bias)


def _stack_heads(q):
    z = jnp.zeros((q.shape[0], HD), q.dtype)
    return jnp.concatenate([jnp.concatenate([q[:, r * HD:(r + 1) * HD], z], axis=1) for r in range(REP)],
                           axis=0)


def _tile_heads(x):
    return jnp.concatenate([x] * REP, axis=1)


def _pad_rows(x, n):
    if x.shape[0] == n:
        return x
    return jnp.concatenate([x, jnp.zeros((n - x.shape[0],) + x.shape[1:], x.dtype)], axis=0)


def _bias4(dist, tab_ref, g):
    biases = [jnp.full(dist.shape, tab_ref[0, g * REP + r], F32) for r in range(REP)]
    for kk in range(1, N_BUCKETS):
        reached = dist >= BUCKET_THR[kk - 1]
        biases = [jnp.where(reached, tab_ref[kk, g * REP + r], biases[r]) for r in range(REP)]
    return biases


def _cmp_branch(qc, kcvc_ref, tab_ref, g, pos, pos0, x_scr, nb):
    tq = qc.shape[0]
    kcvc = kcvc_ref[0:nb, :]
    q4 = _stack_heads(qc)
    lt = _dot_nt(kcvc, q4)
    n_io = lax.broadcasted_iota(I32, (nb, tq), 0)
    dist = pos - (n_io * BLOCK + (BLOCK - 1))
    vis = dist >= 0
    if nb <= CMP_BIAS_ROWS:
        biased = [lt[:, r * tq:(r + 1) * tq] + b for r, b in enumerate(_bias4(dist, tab_ref, g))]
    else:
        far = [tab_ref[N_BUCKETS - 1, g * REP + r] for r in range(REP)]
        x_scr[0:nb, :] = jnp.concatenate([lt[:, r * tq:(r + 1) * tq] + far[r] for r in range(REP)], axis=1)
        first = (pos0 - (BUCKET_THR[-1] + BLOCK - 1)) // BLOCK + 1
        w0 = pl.multiple_of(jnp.clip(first // 8 * 8, 0, nb - CMP_BIAS_ROWS), 8)
        w_io = lax.broadcasted_iota(I32, (CMP_BIAS_ROWS, tq), 0) + w0
        near = _bias4(pos - (w_io * BLOCK + (BLOCK - 1)), tab_ref, g)
        lt_w = _dot_nt(kcvc_ref[pl.ds(w0, CMP_BIAS_ROWS), :], q4)
        x_scr[pl.ds(w0, CMP_BIAS_ROWS), :] = jnp.concatenate(
            [lt_w[:, r * tq:(r + 1) * tq] + near[r] for r in range(REP)], axis=1)
        biased = [x_scr[0:nb, r * tq:(r + 1) * tq] for r in range(REP)]
    probs = []
    for r in range(REP):
        x = jnp.where(vis, biased[r], NEG)
        e = jnp.exp(x - jnp.max(x, axis=0, keepdims=True))
        p = e * (1.0 / jnp.sum(e, axis=0, keepdims=True))
        probs.append(jnp.where(vis, p, 0.0))
    score = probs[0] + probs[1] + probs[2] + probs[3]
    oc_t = _dot(kcvc.T[HD:, :], jnp.concatenate(probs, axis=1))
    cand = n_io < pos // BLOCK
    return oc_t, jnp.where(cand, score, -jnp.inf)


def _topk_rows(score, n_top, pick_fn=None):
    nb, tq = score.shape
    n_f = lax.broadcasted_iota(I32, (nb, tq), 0).astype(F32)
    s = score
    for it in range(n_top):
        mx = jnp.max(s, axis=0, keepdims=True)
        idx = jnp.min(jnp.where(s == mx, n_f, float(nb)), axis=0, keepdims=True)
        if pick_fn is not None:
            pick_fn(it, idx, mx > -jnp.inf)
        s = jnp.where(n_f == idx, -jnp.inf, s)
    return (s == -jnp.inf) & (score > -jnp.inf)


def _flash_init(nl):
    return jnp.full((1, nl), NEG, F32), jnp.zeros((VT_ROWS, nl), F32)


def _online_step(carry, tiles):
    m_c, acc_c = carry
    m_new = m_c
    for x2, _ in tiles:
        m_new = jnp.maximum(m_new, jnp.max(x2, axis=0, keepdims=True))
    acc = jnp.exp2(m_c - m_new) * acc_c
    for x2, vt in tiles:
        acc = acc + jnp.dot(vt, jnp.exp2(x2 - m_new).astype(BF16), preferred_element_type=F32)
    return m_new, acc


def _softmax_part(x2, vt):
    m = jnp.max(x2, axis=0, keepdims=True)
    return m, jnp.dot(vt, jnp.exp2(x2 - m).astype(BF16), preferred_element_type=F32)


def _merge_parts(a, b):
    m = jnp.maximum(a[0], b[0])
    return m, jnp.exp2(a[0] - m) * a[1] + jnp.exp2(b[0] - m) * b[1]


def _normalise(acc):
    return acc[:HD] / acc[HD:HD + 1]


def _untranspose_heads(o_t, tq):
    halves = []
    for p in range(REP // 2):
        pair = jnp.concatenate([o_t[:, (2 * p) * tq:(2 * p + 1) * tq],
                                o_t[:, (2 * p + 1) * tq:(2 * p + 2) * tq]], axis=0)
        halves.append(pair.T)
    return jnp.concatenate(halves, axis=1)


def _gate_row(gt_ref, br, g):
    return jnp.concatenate([gt_ref[pl.ds(br * NKV * REP + g * REP + r, 1), :] for r in range(REP)], axis=1)


def _attn_prompt_kernel(tab_ref, qc_ref, qs_ref, qw_ref, gt_ref, kcvc_ref, ksel_ref, vselt_ref,
                        kw0, kw1, kw2, kw3, kw4, vw0, vw1, vw2, vw3, vw4, bt_ref, btd_ref, o_ref,
                        sel_scr, xc_scr, oct_scr):
    g = pl.program_id(0)
    i = pl.program_id(1)
    tq = TQ
    nl = REP * tq
    t0 = i * tq
    pos = t0 + lax.broadcasted_iota(I32, (1, tq), 1)

    nb = kcvc_ref.shape[0]
    sel_scr[...] = jnp.full((nb, tq), NEG, F32)
    classes = [rows for rows in CMP_ROW_CLASSES if rows < nb] + [nb]
    for k, rows in enumerate(classes):
        lo = classes[k - 1] if k else 0

        @pl.when((2 * i + 2 > lo) & ((2 * i + 2 <= rows) | (rows == nb)))
        def _(rows=rows):
            oc, score = _cmp_branch(qc_ref[...], kcvc_ref, tab_ref, g, pos, t0, xc_scr, rows)
            oct_scr[...] = oc
            sel_scr[0:rows, :] = jnp.where(_topk_rows(score, TOPK - 1), 0.0, NEG)

    oc_t = oct_scr[...]

    def bias_tile(a):
        return jnp.concatenate([bt_ref[a, r] for r in range(REP)], axis=1)

    qs_t = (_stack_heads(qs_ref[...]) * LOG2E).T[:HD].astype(BF16)
    far_bias = jnp.concatenate([jnp.full((1, tq), tab_ref[N_BUCKETS - 1, g * REP + r] * LOG2E, F32)
                                for r in range(REP)], axis=1)
    far_hi = far_bias.astype(BF16).astype(F32)
    row16 = lax.broadcasted_iota(I32, (SUPER_BLOCKS, nl), 0)
    const_rows = jnp.where(row16 == 0, far_hi, jnp.where(row16 == 1, far_bias - far_hi, 0.0)).astype(BF16)
    pad_rows = jnp.zeros((LANES - CONST_LANE - SUPER_BLOCKS, nl), BF16)

    def supers_step(supers, near, carry):
        tiles = []
        for jj in supers:
            mrows = _tile_heads(sel_scr[pl.ds(pl.multiple_of(jj * SUPER_BLOCKS, SUPER_BLOCKS), SUPER_BLOCKS), :])
            q_aug = jnp.concatenate([qs_t, mrows.astype(BF16), const_rows, pad_rows], axis=0)
            for c in range(SUPER // KT):
                s0 = pl.multiple_of(jj * SUPER + c * KT, KT)
                x = jnp.dot(ksel_ref[pl.ds(s0, KT), :], q_aug, preferred_element_type=F32)
                if near:
                    tile0 = jj * (SUPER // LANES) + c * (KT // LANES)
                    x = x + jnp.concatenate([bias_delta(jnp.clip(i - tile0 - a, 0, N_BIAS_TILES - 1))
                                             for a in range(KT // LANES)], axis=0)
                tiles.append((x, jnp.max(x, axis=0, keepdims=True), s0))
        parts = [(m, jnp.dot(vselt_ref[0:VT_ROWS, pl.ds(s0, KT)], jnp.exp2(x - m).astype(BF16),
                             preferred_element_type=F32)) for x, m, s0 in tiles]
        for part in parts:
            carry = _merge_parts(carry, part)
        return carry

    def make_body(near):
        return lambda step, carry: supers_step(
            [step * SUPERS_PER_STEP + u for u in range(SUPERS_PER_STEP)], near, carry)

    def bias_delta(a):
        return jnp.concatenate([btd_ref[a, r] for r in range(REP)], axis=1)

    keys_per_step = SUPER * SUPERS_PER_STEP
    n_super = (2 * i) // SUPER_BLOCKS + 1
    n_steps = n_super // SUPERS_PER_STEP
    n_far = jnp.minimum(jnp.maximum(t0 - (N_BIAS_TILES - 1) * LANES + LANES, 0) // keys_per_step, n_steps)
    carry = lax.fori_loop(0, n_far, make_body(False), _flash_init(nl))
    carry = lax.fori_loop(n_far, n_steps, make_body(True), carry)
    carry = lax.cond(n_super % SUPERS_PER_STEP == 1,
                     lambda c: supers_step([n_super - 1], True, c), lambda c: c, carry)

    sj = lax.broadcasted_iota(I32, (LANES, nl), 0)
    ti4 = _tile_heads(lax.broadcasted_iota(I32, (1, tq), 1))
    t0a = pl.multiple_of(t0, LANES)
    q_diag = jnp.concatenate([qs_t, jnp.zeros((SUPER_BLOCKS, nl), BF16), const_rows, pad_rows], axis=0)
    qw_bf = (_stack_heads(qw_ref[...]) * LOG2E).astype(BF16)
    kw_refs = (kw0, kw1, kw2, kw3, kw4)
    vw_refs = (vw0, vw1, vw2, vw3, vw4)
    x_diag = jnp.dot(ksel_ref[pl.ds(t0a, LANES), :], q_diag, preferred_element_type=F32)
    x_win = [_dot_nt(kw_refs[j][...], qw_bf) for j in range(5)]
    x = jnp.where((sj // BLOCK == ti4 // BLOCK) & (sj <= ti4), x_diag + bias_delta(0), NEG)
    _, accs = _online_step(carry, [(x, vselt_ref[0:VT_ROWS, pl.ds(t0a, LANES)])])

    tiles = []
    for j in range(5):
        dist = LANES * (4 - j) + ti4 - sj
        ok = jnp.where((dist >= 0) & (dist <= WINDOW), 1.0, 0.0) * jnp.where(i - 4 + j >= 0, 1.0, 0.0)
        tiles.append((jnp.where(ok > 0.5, x_win[j] + bias_tile(4 - j), NEG), vw_refs[j][0:VT_ROWS, :]))
    _, accw = _online_step(_flash_init(nl), tiles)

    out_t = (_gate_row(gt_ref, 0, g) * oc_t + _gate_row(gt_ref, 1, g) * _normalise(accs)
             + _gate_row(gt_ref, 2, g) * _normalise(accw))
    o_ref[...] = _untranspose_heads(out_t, tq)


def _attn_prompt(rel_bias, qs, gates_t, kcvc, sel_bf, selt_bf, win_bf, wint_bf, bias_tiles, bias_delta_tiles):
    t = qs.shape[0]
    nb = kcvc.shape[0]
    width = NKV * REP * HD

    def qspec(br):
        return pl.BlockSpec((TQ, REP * HD), lambda g, i, br=br: (i, br * NKV + g))

    def kw_spec(j):
        return pl.BlockSpec((LANES, LANES), lambda g, i, j=j: (jnp.maximum(i - 4 + j, 0), g))

    def vw_spec(j):
        return pl.BlockSpec((LANES, LANES), lambda g, i, j=j: (g, jnp.maximum(i - 4 + j, 0)))

    return pl.pallas_call(
        _attn_prompt_kernel,
        grid=(NKV, t // TQ),
        in_specs=[pl.BlockSpec(memory_space=pltpu.SMEM),
                  qspec(0), qspec(1), qspec(2),
                  pl.BlockSpec((LANES, TQ), lambda g, i: (0, i)),
                  pl.BlockSpec((nb, LANES), lambda g, i: (0, g)),
                  pl.BlockSpec((t, LANES), lambda g, i: (0, g)),
                  pl.BlockSpec((LANES, t), lambda g, i: (g, 0))]
                 + [kw_spec(j) for j in range(5)] + [vw_spec(j) for j in range(5)]
                 + [pl.BlockSpec((N_BIAS_TILES, REP, LANES, TQ), lambda g, i: (0, g, 0, 0))] * 2,
        out_specs=pl.BlockSpec((TQ, REP * HD), lambda g, i: (i, g)),
        out_shape=jax.ShapeDtypeStruct((t, width), F32),
        scratch_shapes=[pltpu.VMEM((nb, TQ), F32), pltpu.VMEM((nb, REP * TQ), F32),
                        pltpu.VMEM((HD, REP * TQ), F32)],
        compiler_params=_cparams("arbitrary", "arbitrary"),
        name="attn_prompt",
    )(rel_bias, qs, qs, qs, gates_t, kcvc, sel_bf, selt_bf,
      win_bf, win_bf, win_bf, win_bf, win_bf, wint_bf, wint_bf, wint_bf, wint_bf, wint_bf, bias_tiles,
      bias_delta_tiles)


def _attn_s1_kernel(tab_ref, qc_ref, kcvc_ref, oct_ref, idx_ref, xc_scr, *, q0):
    g = pl.program_id(1)
    tq = TQ
    pos = q0 + lax.broadcasted_iota(I32, (1, tq), 1)
    oc_t, score = _cmp_branch(_pad_rows(qc_ref[0], tq), kcvc_ref, tab_ref, g, pos, q0, xc_scr,
                              kcvc_ref.shape[0])
    oct_ref[0] = oc_t

    def pick(it, idx, ok):
        idx_ref[0, it:it + 1, :] = jnp.where(ok, idx, -1.0).astype(I32)

    _topk_rows(score, TOPK - 1, pick)
    idx_ref[0, TOPK - 1:TOPK, :] = jnp.full((1, tq), -1, I32)


def _attn_s1(rel_bias, qs8, kcvc, q0):
    batch = qs8.shape[0]
    nb = kcvc.shape[0] // batch
    return pl.pallas_call(
        functools.partial(_attn_s1_kernel, q0=q0),
        grid=(batch, NKV),
        in_specs=[pl.BlockSpec(memory_space=pltpu.SMEM),
                  pl.BlockSpec((1, 8, REP * HD), lambda b, g: (b, 0, g)),
                  pl.BlockSpec((nb, LANES), lambda b, g: (b, g))],
        out_specs=[pl.BlockSpec((1, HD, REP * TQ), lambda b, g: (b * NKV + g, 0, 0)),
                   pl.BlockSpec((1, TOPK, TQ), lambda b, g: (b * NKV + g, 0, 0))],
        out_shape=[jax.ShapeDtypeStruct((batch * NKV, HD, REP * TQ), F32),
                   jax.ShapeDtypeStruct((batch * NKV, TOPK, TQ), I32)],
        scratch_shapes=[pltpu.VMEM((nb, REP * TQ), F32)],
        compiler_params=_cparams("arbitrary", "arbitrary"),
        name="attn_sample_select",
    )(rel_bias, qs8, kcvc)


def _attn_s2_kernel(ids_ref, phys_ref, qs_ref, qw_ref, gt_ref, oct_ref, *refs, q0, nvalid):
    nsel = nvalid * (TOPK - 1)
    kb = refs[:nsel]
    snew_ref, cwin_ref, wnew_ref, bt_ref, o_ref = refs[nsel:]
    b = pl.program_id(0)
    g = pl.program_id(1)
    tq = TQ_SAMPLE
    nr = REP * tq
    tr = lax.broadcasted_iota(I32, (nr, LANES), 0) % tq
    sj = lax.broadcasted_iota(I32, (nr, LANES), 1)
    base = (b * NKV + g) * nvalid * TOPK

    def compact(x):
        return jnp.concatenate([x[:, r * TQ:r * TQ + tq] for r in range(REP)], axis=1)

    def bias_rows(a):
        return jnp.concatenate([bt_ref[a, r][:tq, :] for r in range(REP)], axis=0)

    def attend(q_bf, cached, new_ref):
        xs, vts = [], []
        for tile_t, a, ok in cached:
            xs.append(jnp.where(ok, _dot(q_bf, tile_t) + bias_rows(a), NEG))
            vts.append(tile_t.astype(BF16))
        blk = _pad_rows(new_ref[0], LANES).astype(BF16)
        xs.append(jnp.where((sj <= tr) & (sj < nvalid), _dot_nt(q_bf, blk) + bias_rows(0), NEG))
        x = jnp.concatenate(xs, axis=1)
        p = jnp.exp2(x - jnp.max(x, axis=1, keepdims=True))
        l = jnp.sum(p, axis=1, keepdims=True)
        p = p.astype(BF16)
        nk = x.shape[1] - LANES
        acc = _dot_nt(p[:, :nk], jnp.concatenate(vts, axis=1)) + _dot(p[:, nk:], blk)
        return acc[:, HD:] / l

    cached = []
    for t in range(nvalid):
        for k in range(TOPK - 1):
            n = ids_ref[base + t * TOPK + k]
            nc = jnp.maximum(n, 0)
            a = jnp.clip(q0 // LANES - nc // 2, 0, N_BIAS_TILES - 1)
            lo = jnp.where(n >= 0, (nc % 2) * BLOCK, LANES)
            cached.append((kb[t * (TOPK - 1) + k][0], a, (sj >= lo) & (sj < lo + BLOCK) & (tr == t)))
    o_sel = attend((_stack_heads(_pad_rows(qs_ref[0], tq)) * LOG2E).astype(BF16), cached, snew_ref)

    cached = []
    for j in range(WINDOW // LANES):
        dist = (WINDOW - LANES * j) + tr - sj
        cached.append((cwin_ref[0, :, j * LANES:(j + 1) * LANES], WINDOW // LANES - j,
                       (dist >= 0) & (dist <= WINDOW)))
    o_win = attend((_stack_heads(_pad_rows(qw_ref[0], tq)) * LOG2E).astype(BF16), cached, wnew_ref)

    gt2 = gt_ref.at[0]
    o_cmp = _pad_rows(compact(oct_ref[0]), LANES).T[:, :HD]
    row8 = lax.broadcasted_iota(I32, (8, nr), 0)
    g8 = jnp.zeros((8, nr), F32)
    for br in range(3):
        g8 = jnp.where(row8 == br, compact(_gate_row(gt2, br, g)), g8)
    gates = _pad_rows(g8, LANES).T
    out = gates[:, 0:1] * o_cmp + gates[:, 1:2] * o_sel + gates[:, 2:3] * o_win
    o_ref[0] = jnp.concatenate([out[r * tq:r * tq + 8] for r in range(REP)], axis=1)


def _attn_s2(ids, phys, qs8, gates_t, oct, cache_sel_t, sel_new8, cache_win_t, win_new8, bias_tiles, q0, nvalid):
    batch = qs8.shape[0]
    nsel = nvalid * (TOPK - 1)

    def kb_spec(t, k):
        return pl.BlockSpec((1, LANES, PAGE),
                            lambda b, g, ids_r, phys_r, t=t, k=k:
                            (phys_r[((b * NKV + g) * nvalid + t) * TOPK + k], g, 0))

    def bg(shape, col):
        return pl.BlockSpec(shape, lambda b, g, ids_r, phys_r, col=col: (b, 0, col * NKV + g))

    return pl.pallas_call(
        functools.partial(_attn_s2_kernel, q0=q0, nvalid=nvalid),
        grid_spec=pltpu.PrefetchScalarGridSpec(
            num_scalar_prefetch=2,
            grid=(batch, NKV),
            in_specs=[bg((1, 8, REP * HD), 1), bg((1, 8, REP * HD), 2),
                      pl.BlockSpec((1, LANES, TQ), lambda b, g, ids_r, phys_r: (b, 0, 0)),
                      pl.BlockSpec((1, HD, REP * TQ), lambda b, g, ids_r, phys_r: (b * NKV + g, 0, 0))]
                     + [kb_spec(t, k) for t in range(nvalid) for k in range(TOPK - 1)]
                     + [bg((1, 8, LANES), 0),
                        pl.BlockSpec((1, LANES, WINDOW), lambda b, g, ids_r, phys_r: (b, g, 0)),
                        bg((1, 8, LANES), 0),
                        pl.BlockSpec((N_BIAS_TILES, REP, LANES, TQ), lambda b, g, ids_r, phys_r: (0, g, 0, 0))],
            out_specs=bg((1, 8, REP * HD), 0)),
        out_shape=jax.ShapeDtypeStruct((batch, 8, NKV * REP * HD), F32),
        compiler_params=_cparams("arbitrary", "arbitrary"),
        name="attn_sample",
    )(ids, phys, qs8, qs8, gates_t, oct, *([cache_sel_t] * nsel), sel_new8, cache_win_t, win_new8, bias_tiles)


def _pad_cols(w, n):
    return jnp.pad(w, ((0, 0), (0, n - w.shape[1])))


def _prep(p):
    inner = NH_A * DK_A
    width = NKV * REP * HD
    q = {}
    q['w_in'] = _pad_cols(p['w_in_a'][0], 4 * inner + LANES).astype(BF16)
    q['b_if'] = _pad_cols(p['b_if_a'][0][None, :], LANES)
    q['w_out'] = p['w_out_a'][0].astype(BF16)
    q['w_ff1'] = [p['w_ff1'][l].astype(BF16) for l in range(2)]
    q['w_ff2'] = [p['w_ff2'][l].astype(BF16) for l in range(2)]
    q['w_kv'] = p['w_kv'].astype(BF16)
    q['w_q'] = _pad_cols(p['w_q_b'][0], 3 * width + LANES).astype(BF16)
    q['b_gate'] = _pad_cols(p['b_gate_b'][0][None, :], LANES)
    q['w_o'] = p['w_o_b'][0].astype(BF16)
    q['q_gain'] = (jnp.tile(p['q_norm_b'][0][:, None, :], (1, NKV * REP, 1)) * (HD ** -0.5)).reshape(1, 3 * width)
    seg = np.arange(256) // HD
    q['bd'] = jnp.asarray((seg[:, None] == seg[None, :]).astype(np.float32) / HD, dtype=BF16)
    ones = jnp.ones((HD,), F32)
    q['k_gain'] = jnp.concatenate([jnp.tile(jnp.concatenate([p['k_norm'][br], ones]), NKV) for br in (1, 2)])[None, :]
    q['k_mask'] = jnp.tile(jnp.concatenate([ones, 0.0 * ones]), 2 * NKV)[None, :]
    q['k_gain0'] = jnp.concatenate([p['k_norm'][0], ones])[None, :]
    w1 = p['w_cmp1']
    z = jnp.zeros_like(w1[0])
    per_l = jnp.concatenate([jnp.concatenate([w1[0], z], axis=-1), jnp.concatenate([z, w1[1]], axis=-1)], axis=1)
    q['w_cmp1'] = per_l.reshape(BLOCK // 2, 2 * 2 * HD, 2 * w1.shape[-1]).astype(BF16)
    w2 = p['w_cmp2']
    z2 = jnp.zeros_like(w2[0])
    q['w_cmp2'] = jnp.concatenate([jnp.concatenate([w2[0], z2], axis=1), jnp.concatenate([z2, w2[1]], axis=1)],
                                  axis=0).astype(BF16)
    q['pos_rows'] = jnp.tile(p['cmp_pos'].reshape(BLOCK, 2 * HD), (1, NKV))
    return q


def _feature_major(cache):
    n, t = cache.shape[:2]
    return jnp.transpose(cache, (0, 2, 3, 4, 1)).reshape(n, -1, t)


def _mods(ada_rows, rep):
    k = ada_rows.shape[1] // D_MODEL
    out = []
    for j in range(k):
        a = ada_rows[:, j * D_MODEL:(j + 1) * D_MODEL]
        out.append(jnp.repeat(a, rep, axis=0) if rep > 1 else a)
    return out


def _layer0(x, mods, p, q, mlstm_fn, tm, tmf):
    sh1, sc1, g1, sh2, sc2, g2 = mods
    proj = _proj_plain(x, p['norm_mix'][0][None, :], sh1, sc1, q['w_in'], tm)
    y, states = mlstm_fn(proj)
    x = _out_proj(x, g1, y, q['w_out'], tm)
    x = _ffn(x, p['norm_ffn'][0][None, :], sh2, sc2, g2, q['w_ff1'][0], q['w_ff2'][0], tmf)
    return x, states


def _layer1_tail(x, attn, mods, p, q, tm, tmf):
    _, _, g1, sh2, sc2, g2 = mods
    x = _out_proj(x, g1, attn, q['w_o'], tm)
    return _ffn(x, p['norm_ffn'][1][None, :], sh2, sc2, g2, q['w_ff1'][1], q['w_ff2'][1], tmf)


def kernel(x_prompt, x_sample, cache_cmp, cache_sel, cache_win, state_C, state_n, state_m, page_table,
           c_prompt, c_sample, w_ada, b_ada, norm_mix, norm_ffn, w_ff1, w_ff2, w_in_a, b_if_a,
           head_norm_a, w_out_a, w_ada_kv, b_ada_kv, norm_kv, w_kv, k_norm, cmp_pos, w_cmp1, w_cmp2,
           w_q_b, b_gate_b, q_norm_b, w_o_b, rel_bias):
    p = dict(norm_mix=norm_mix, norm_ffn=norm_ffn, w_ff1=w_ff1, w_ff2=w_ff2, w_in_a=w_in_a, b_if_a=b_if_a,
             w_out_a=w_out_a, w_kv=w_kv, k_norm=k_norm, cmp_pos=cmp_pos, w_cmp1=w_cmp1, w_cmp2=w_cmp2,
             w_q_b=w_q_b, b_gate_b=b_gate_b, q_norm_b=q_norm_b, w_o_b=w_o_b)
    q = _prep(p)
    bp, tp, d = x_prompt.shape
    bs, ts, _ = x_sample.shape
    past = page_table.shape[1] * PAGE
    wbuf = cache_win.shape[1]
    assert bp == 1 and ts < BLOCK and ts <= 8 and wbuf == WINDOW and past % LANES == 0
    assert tp % (SUPER * SUPERS_PER_STEP) == 0
    width = NKV * REP * HD
    kvw = NKV * 2 * HD

    nc = bp + bs
    c_all = jnp.pad(jnp.concatenate([c_prompt, c_sample], axis=0), ((0, -nc % 8), (0, 0)))
    ada = [_ada(c_all, w_ada[l], b_ada[l]) for l in range(2)]
    ada_kv = _ada(c_all, w_ada_kv, b_ada_kv)
    bias_kq, bias_kq_delta, bias_qk = _bias_tiles(rel_bias)
    head_gain = head_norm_a[0]

    xp = x_prompt.reshape(tp, d)
    zc = jnp.zeros((bp, NH_A, DK_A, DK_A), F32)
    zn = jnp.zeros((bp, NH_A, 1, DK_A), F32)
    zm = jnp.zeros((bp, 8, LANES), F32)

    def mlstm_prompt(proj):
        y, c, n, m = _mlstm(proj.reshape(bp, tp, -1), q['b_if'], head_gain, zc, zn, zm,
                            rows=256, valid=256, chunk=256, zero_init=True)
        return y.reshape(tp, -1), (c, n, m)

    xp, (pc, pn, pm) = _layer0(xp, _mods(ada[0][:bp], 1), p, q, mlstm_prompt, 256, 512)
    sh, sc = _mods(ada_kv[:bp], 1)
    p_cmp, _, _, p_cmp_t, p_sel_t, p_win_t, sel_bf, selt_bf, win_bf, wint_bf = _proj_kv(
        xp, norm_kv[None, :], sh, sc, q['w_kv'], q['bd'], q['k_gain'], q['k_mask'], 256)
    kcvc_p = _compress(p_cmp, q['pos_rows'], q['w_cmp1'], q['w_cmp2'], q['k_gain0'],
                       min(CMP_BLOCKS_PER_STEP, tp // BLOCK))
    mods1 = _mods(ada[1][:bp], 1)
    qs_p, gt_p = _proj_q(xp, norm_mix[1][None, :], mods1[0], mods1[1], q['w_q'], q['bd'], q['q_gain'],
                         q['b_gate'], 256)
    attn_p = _attn_prompt(rel_bias, qs_p, gt_p, kcvc_p, sel_bf, selt_bf, win_bf, wint_bf, bias_kq, bias_kq_delta)
    y_prompt = _layer1_tail(xp, attn_p, mods1, p, q, 256, 512).reshape(bp, tp, d)

    ms = bs * ts
    xs = x_sample.reshape(ms, d)
    m0 = jnp.broadcast_to(jnp.pad(state_m[0], ((0, 0), (0, 8 - NH_A)))[:, :, None], (bs, 8, LANES))

    def mlstm_sample(proj):
        proj8 = jnp.pad(proj.reshape(bs, ts, -1), ((0, 0), (0, 8 - ts), (0, 0)))
        y, c, n, m = _mlstm(proj8, q['b_if'], head_gain, state_C[0], state_n[0][:, :, None, :], m0,
                            rows=8, valid=ts, chunk=LANES, zero_init=False)
        return y[:, :ts].reshape(ms, -1), (c, n, m)

    xs, (sc_, sn_, sm_) = _layer0(xs, _mods(ada[0][bp:nc], ts), p, q, mlstm_sample, ms, ms)
    sh, sc = _mods(ada_kv[bp:nc], ts)
    s_cmp, s_sel, s_win = _proj_kv(xs, norm_kv[None, :], sh, sc, q['w_kv'], q['bd'], q['k_gain'],
                                   q['k_mask'], ms)[:3]
    kcvc_s = _compress_paged(_feature_major(cache_cmp), page_table, q['pos_rows'],
                             q['w_cmp1'], q['w_cmp2'], q['k_gain0'], min(CMP_BLOCKS_PER_STEP, past // BLOCK))
    mods1 = _mods(ada[1][bp:nc], ts)
    qs_s, gt_s = _proj_q(xs, norm_mix[1][None, :], mods1[0], mods1[1], q['w_q'], q['bd'], q['q_gain'],
                         q['b_gate'], ms)

    def pad8(a):
        return jnp.pad(a.reshape(bs, ts, -1), ((0, 0), (0, 8 - ts), (0, 0)))

    qs8 = pad8(qs_s)
    oct_s, idx_s = _attn_s1(rel_bias, qs8, kcvc_s, past)
    ids = jnp.transpose(idx_s.reshape(bs, NKV, TOPK, TQ)[:, :, :, :ts], (0, 1, 3, 2))
    idc = jnp.maximum(ids, 0)
    pages = jnp.take_along_axis(page_table, (idc // (PAGE // BLOCK)).reshape(bs, -1), axis=1).reshape(ids.shape)
    gt8 = jnp.pad(jnp.transpose(gt_s.reshape(LANES, bs, ts), (1, 0, 2)), ((0, 0), (0, 0), (0, TQ - ts)))
    attn_s = _attn_s2(ids.reshape(-1), pages.reshape(-1), qs8, gt8, oct_s,
                      _feature_major(cache_sel), pad8(s_sel), _feature_major(cache_win), pad8(s_win),
                      bias_qk, past, ts)
    y_sample = _layer1_tail(xs, attn_s[:, :ts].reshape(ms, width), mods1, p, q, ms, ms).reshape(bs, ts, d)

    rows5 = (NKV, 2, HD)

    def token_major(a_t):
        return jnp.transpose(a_t.reshape(*rows5, a_t.shape[1]), (3, 0, 1, 2))[None]

    s_win_all = jnp.concatenate([cache_win, s_win.reshape(bs, ts, *rows5)], axis=1)
    return (y_prompt, y_sample,
            pc[None], pn.reshape(1, bp, NH_A, DK_A), pm[None, :, :NH_A, 0],
            token_major(p_cmp_t), token_major(p_sel_t), token_major(p_win_t[:, tp - min(WINDOW, tp):]),
            sc_[None], sn_.reshape(1, bs, NH_A, DK_A), sm_[None, :, :NH_A, 0],
            s_cmp.reshape(bs, ts, *rows5), s_sel.reshape(bs, ts, *rows5),
            s_win_all[:, -min(WINDOW, wbuf + ts):])
```

```python
import functools
import math

import numpy as np
import jax
import jax.numpy as jnp
from jax import lax
from jax.experimental import pallas as pl
from jax.experimental.pallas import tpu as pltpu

F32 = jnp.float32
BF16 = jnp.bfloat16
I32 = jnp.int32

D_MODEL = 1024
NH_A = 4
DK_A = 256
NKV = 4
REP = 4
HD = 64
BLOCK = 64
TOPK = 16
WINDOW = 512
N_BUCKETS = 32
REL_MAX_DIST = 2048
PAGE = 128
EPS = 1e-6
NEG = -1e30
LOG2E = math.log2(math.e)

LANES = 128
VMEM_LIMIT = 56 * 1024 * 1024
TQ = 128
TQ_SAMPLE = 32
KT = 256
SUPER_BLOCKS = 16
SUPER = SUPER_BLOCKS * BLOCK
SUPERS_PER_STEP = 2
VT_ROWS = HD + 16
CONST_LANE = HD + SUPER_BLOCKS
N_BIAS_TILES = 14
CMP_BLOCKS_PER_STEP = 64
CMP_BIAS_ROWS = 40
CMP_ROW_CLASSES = (64, 128, 192)


def _bucket_thresholds():
    exact = N_BUCKETS // 2
    d = np.arange(1, 4 * REL_MAX_DIST, dtype=np.float64)
    big = exact + np.floor(np.log(d / exact) / math.log(REL_MAX_DIST / exact) * (N_BUCKETS - exact)).astype(np.int64)
    b = np.where(d < exact, d.astype(np.int64), np.minimum(big, N_BUCKETS - 1))
    return [int(d[np.argmax(b >= k)]) for k in range(1, N_BUCKETS)]


BUCKET_THR = _bucket_thresholds()
assert 128 * (N_BIAS_TILES - 1) - (TQ - 1) >= BUCKET_THR[-1]


def _cparams(*sem):
    return pltpu.CompilerParams(dimension_semantics=sem, vmem_limit_bytes=VMEM_LIMIT)


def _dot(a, b):
    return jnp.dot(a.astype(BF16), b.astype(BF16), preferred_element_type=F32)


def _dot_nt(a, b):
    return lax.dot_general(a.astype(BF16), b.astype(BF16), (((1,), (1,)), ((), ())),
                           preferred_element_type=F32)


def _split3(x):
    hi = x.astype(BF16)
    r1 = x - hi.astype(F32)
    mid = r1.astype(BF16)
    lo = (r1 - mid.astype(F32)).astype(BF16)
    return hi, mid, lo


def _sigmoid(x):
    return 1.0 / (1.0 + jnp.exp(-x))


def _norm_mod(x, gain, shift, scale):
    ms = jnp.mean(x * x, axis=-1, keepdims=True)
    y = x * lax.rsqrt(ms + EPS) * gain
    return y * (1.0 + scale) + shift


def _seg_mean_sq(y, bd):
    parts = []
    for j in range(y.shape[1] // 256):
        sq = y[:, j * 256:(j + 1) * 256]
        sq = sq * sq
        hi = sq.astype(BF16)
        lo = (sq - hi.astype(F32)).astype(BF16)
        parts.append(jnp.dot(hi, bd, preferred_element_type=F32) + jnp.dot(lo, bd, preferred_element_type=F32))
    return parts[0] if len(parts) == 1 else jnp.concatenate(parts, axis=1)


def _row_spec(tm, n, per_row):
    if per_row:
        return pl.BlockSpec((tm, n), lambda i: (i, 0))
    return pl.BlockSpec((1, n), lambda i: (0, 0))


def _const_spec(shape):
    return pl.BlockSpec(shape, lambda i: tuple(0 for _ in shape))


def _ada_kernel(c_ref, w_ref, b_ref, o_ref):
    c = c_ref[...]
    o_ref[...] = _dot(c * _sigmoid(c), w_ref[...]) + b_ref[...]


def _ada(c, w, b):
    m, k = c.shape
    n = w.shape[1]
    tn = 1024
    return pl.pallas_call(
        _ada_kernel,
        grid=(n // tn,),
        in_specs=[pl.BlockSpec((m, k), lambda j: (0, 0)),
                  pl.BlockSpec((k, tn), lambda j: (0, j)),
                  pl.BlockSpec((1, tn), lambda j: (0, j))],
        out_specs=pl.BlockSpec((m, tn), lambda j: (0, j)),
        out_shape=jax.ShapeDtypeStruct((m, n), F32),
        compiler_params=_cparams("arbitrary"),
        name="ada",
    )(c, w, b.reshape(1, n))


def _proj_plain_kernel(x_ref, gain_ref, sh_ref, sc_ref, w_ref, o_ref):
    xn = _norm_mod(x_ref[...], gain_ref[...], sh_ref[...], sc_ref[...])
    o_ref[...] = jnp.dot(xn.astype(BF16), w_ref[...], preferred_element_type=F32)


def _proj_plain(x, gain, shift, scale, w_bf, tm):
    m, d = x.shape
    n = w_bf.shape[1]
    per_row = shift.shape[0] != 1
    return pl.pallas_call(
        _proj_plain_kernel,
        grid=(m // tm,),
        in_specs=[pl.BlockSpec((tm, d), lambda i: (i, 0)),
                  _const_spec((1, d)),
                  _row_spec(tm, d, per_row), _row_spec(tm, d, per_row),
                  _const_spec((d, n))],
        out_specs=pl.BlockSpec((tm, n), lambda i: (i, 0)),
        out_shape=jax.ShapeDtypeStruct((m, n), F32),
        compiler_params=_cparams("arbitrary"),
        name="proj_mlstm",
    )(x, gain, shift, scale, w_bf)


def _proj_q_kernel(x_ref, gain_ref, sh_ref, sc_ref, w_ref, bd_ref, qg_ref, bg_ref, q_ref, gt_ref):
    xn = _norm_mod(x_ref[...], gain_ref[...], sh_ref[...], sc_ref[...])
    y = jnp.dot(xn.astype(BF16), w_ref[...], preferred_element_type=F32)
    nq = q_ref.shape[1]
    yq = y[:, :nq]
    ms = _seg_mean_sq(yq, bd_ref[...])
    q_ref[...] = yq * lax.rsqrt(ms + EPS) * qg_ref[...]
    gates = _sigmoid(y[:, nq:] + bg_ref[...])
    gt_ref[...] = gates.T


def _proj_q(x, gain, shift, scale, w_bf, bd, qgain_row, bgate_row, tm):
    m, d = x.shape
    n = w_bf.shape[1]
    nq = n - LANES
    per_row = shift.shape[0] != 1
    return pl.pallas_call(
        _proj_q_kernel,
        grid=(m // tm,),
        in_specs=[pl.BlockSpec((tm, d), lambda i: (i, 0)),
                  _const_spec((1, d)),
                  _row_spec(tm, d, per_row), _row_spec(tm, d, per_row),
                  _const_spec((d, n)), _const_spec((256, 256)),
                  _const_spec((1, nq)), _const_spec((1, LANES))],
        out_specs=[pl.BlockSpec((tm, nq), lambda i: (i, 0)),
                   pl.BlockSpec((LANES, tm), lambda i: (0, i))],
        out_shape=[jax.ShapeDtypeStruct((m, nq), F32),
                   jax.ShapeDtypeStruct((LANES, m), F32)],
        compiler_params=_cparams("arbitrary"),
        name="proj_q",
    )(x, gain, shift, scale, w_bf, bd, qgain_row, bgate_row)


def _proj_kv_kernel(x_ref, gain_ref, sh_ref, sc_ref, w_ref, bd_ref, kg_ref, km_ref,
                    cmp_ref, sel_ref, win_ref, cmpt_ref, selt_ref, wint_ref,
                    selb_ref, seltb_ref, winb_ref, wintb_ref):
    xn = _norm_mod(x_ref[...], gain_ref[...], sh_ref[...], sc_ref[...])
    y = jnp.dot(xn.astype(BF16), w_ref[...], preferred_element_type=F32)
    w = cmp_ref.shape[1]
    cmp = y[:, :w]
    ykn = y[:, w:]
    ms = _seg_mean_sq(ykn, bd_ref[...])
    ykn = jnp.where(km_ref[...] > 0.5, ykn * lax.rsqrt(ms + EPS) * kg_ref[...], ykn)
    sel = ykn[:, :w]
    win = ykn[:, w:]
    sel_t = sel.T
    win_t = win.T
    cmp_ref[...] = cmp
    sel_ref[...] = sel
    win_ref[...] = win
    cmpt_ref[...] = cmp.T
    selt_ref[...] = sel_t
    wint_ref[...] = win_t
    tm = sel.shape[0]
    tok = pl.program_id(0) * tm + lax.broadcasted_iota(I32, sel.shape, 0)
    lane = lax.broadcasted_iota(I32, sel.shape, 1) % LANES
    onehot = jnp.where((lane - HD == (tok // BLOCK) % SUPER_BLOCKS) | (lane == CONST_LANE)
                       | (lane == CONST_LANE + 1), 1.0, 0.0)
    selb_ref[...] = jnp.where(lane < HD, sel, onehot).astype(BF16)
    winb_ref[...] = win.astype(BF16)

    def value_rows(x_t):
        ones_blk = jnp.where(lax.broadcasted_iota(I32, (HD, tm), 0) == 0, 1.0, 0.0)
        return jnp.concatenate(
            [piece for gg in range(NKV) for piece in (x_t[gg * LANES + HD:(gg + 1) * LANES], ones_blk)],
            axis=0).astype(BF16)

    seltb_ref[...] = value_rows(sel_t)
    wintb_ref[...] = value_rows(win_t)


def _proj_kv(x, gain, shift, scale, w_bf, bd, kgain_row, kmask_row, tm):
    m, d = x.shape
    n = w_bf.shape[1]
    w = n // 3
    per_row = shift.shape[0] != 1
    row = pl.BlockSpec((tm, w), lambda i: (i, 0))
    col = pl.BlockSpec((w, tm), lambda i: (0, i))
    return pl.pallas_call(
        _proj_kv_kernel,
        grid=(m // tm,),
        in_specs=[pl.BlockSpec((tm, d), lambda i: (i, 0)),
                  _const_spec((1, d)),
                  _row_spec(tm, d, per_row), _row_spec(tm, d, per_row),
                  _const_spec((d, n)), _const_spec((256, 256)),
                  _const_spec((1, 2 * w)), _const_spec((1, 2 * w))],
        out_specs=[row, row, row, col, col, col, row, col, row, col],
        out_shape=[jax.ShapeDtypeStruct((m, w), F32)] * 3 + [jax.ShapeDtypeStruct((w, m), F32)] * 3
                  + [jax.ShapeDtypeStruct((m, w), BF16), jax.ShapeDtypeStruct((w, m), BF16),
                     jax.ShapeDtypeStruct((m, w), BF16), jax.ShapeDtypeStruct((w, m), BF16)],
        compiler_params=_cparams("arbitrary"),
        name="proj_kv",
    )(x, gain, shift, scale, w_bf, bd, kgain_row, kmask_row)


def _out_proj_kernel(x_ref, g_ref, y_ref, w_ref, o_ref):
    o_ref[...] = x_ref[...] + g_ref[...] * jnp.dot(y_ref[...].astype(BF16), w_ref[...],
                                                   preferred_element_type=F32)


def _out_proj(x, g, y, w_bf, tm):
    m, d = x.shape
    k = y.shape[1]
    per_row = g.shape[0] != 1
    return pl.pallas_call(
        _out_proj_kernel,
        grid=(m // tm,),
        in_specs=[pl.BlockSpec((tm, d), lambda i: (i, 0)),
                  _row_spec(tm, d, per_row),
                  pl.BlockSpec((tm, k), lambda i: (i, 0)),
                  _const_spec((k, d))],
        out_specs=pl.BlockSpec((tm, d), lambda i: (i, 0)),
        out_shape=jax.ShapeDtypeStruct((m, d), F32),
        compiler_params=_cparams("arbitrary"),
        name="out_proj",
    )(x, g, y, w_bf)


def _ffn_kernel(x_ref, gain_ref, sh_ref, sc_ref, g_ref, w1_ref, w2_ref, o_ref, *, fc):
    x = x_ref[...]
    xn = _norm_mod(x, gain_ref[...], sh_ref[...], sc_ref[...]).astype(BF16)
    acc = jnp.zeros(x.shape, F32)
    for c in range(w1_ref.shape[1] // fc):
        h = jnp.dot(xn, w1_ref[:, c * fc:(c + 1) * fc], preferred_element_type=F32)
        h = jnp.maximum(h, 0.0)
        acc = acc + jnp.dot((h * h).astype(BF16), w2_ref[c * fc:(c + 1) * fc, :],
                            preferred_element_type=F32)
    o_ref[...] = x + g_ref[...] * acc


def _ffn(x, gain, shift, scale, g, w1_bf, w2_bf, tm):
    m, d = x.shape
    f = w1_bf.shape[1]
    per_row = shift.shape[0] != 1
    return pl.pallas_call(
        functools.partial(_ffn_kernel, fc=1024),
        grid=(m // tm,),
        in_specs=[pl.BlockSpec((tm, d), lambda i: (i, 0)),
                  _const_spec((1, d)),
                  _row_spec(tm, d, per_row), _row_spec(tm, d, per_row), _row_spec(tm, d, per_row),
                  _const_spec((d, f)), _const_spec((f, d))],
        out_specs=pl.BlockSpec((tm, d), lambda i: (i, 0)),
        out_shape=jax.ShapeDtypeStruct((m, d), F32),
        compiler_params=_cparams("arbitrary"),
        name="ffn",
    )(x, gain, shift, scale, g, w1_bf, w2_bf)


def _mlstm_kernel(q_ref, k_ref, v_ref, o_ref, gt_ref, bif_ref, hg_ref, c0_ref, n0_ref, m0_ref,
                  y_ref, cout_ref, nout_ref, mout_ref, c_scr, n_scr, m_scr, *, rows, valid, chunk, zero_init):
    ci = pl.program_id(1)
    nci = pl.num_programs(1)

    @pl.when(ci == 0)
    def _():
        if zero_init:
            c_scr[...] = jnp.zeros(c_scr.shape, F32)
            n_scr[...] = jnp.zeros(n_scr.shape, F32)
            m_scr[...] = jnp.zeros(m_scr.shape, F32)
        else:
            c_scr[...] = c0_ref[0]
            n_scr[...] = n0_ref[0]
            m_scr[...] = m0_ref[0]

    L = chunk

    def padded(ref_val, fill):
        if rows == L:
            return ref_val
        pad = jnp.full((L - rows, ref_val.shape[1]), fill, ref_val.dtype)
        return jnp.concatenate([ref_val, pad], axis=0)

    q = padded(q_ref[0], 0.0)
    k = padded(k_ref[0], 0.0)
    v = padded(v_ref[0], 0.0)
    og = padded(o_ref[0], 0.0)
    g = padded(gt_ref[0], 0.0) + bif_ref[...]
    lf = -(jnp.maximum(-g, 0.0) + jnp.log1p(jnp.exp(-jnp.abs(g))))
    li = g
    if valid != L:
        is_real = lax.broadcasted_iota(I32, (L, LANES), 0) < valid
        lf = jnp.where(is_real, lf, 0.0)
        li = jnp.where(is_real, li, NEG)
    r_io = lax.broadcasted_iota(I32, (L, L), 0)
    c_io = lax.broadcasted_iota(I32, (L, L), 1)
    causal = c_io <= r_io
    tril = jnp.where(causal, 1.0, 0.0).astype(BF16)
    hi, mid, lo = _split3(lf)
    b = (jnp.dot(tril, hi, preferred_element_type=F32) + jnp.dot(tril, mid, preferred_element_type=F32)
         + jnp.dot(tril, lo, preferred_element_type=F32))
    lane = lax.broadcasted_iota(I32, (L, LANES), 1)
    mixed_t = jnp.where(lane < NH_A, li, b).T
    for h in range(NH_A):
        sl = slice(h * DK_A, (h + 1) * DK_A)
        qh = q[:, sl] * (DK_A ** -0.5)
        kh = k[:, sl]
        vh = v[:, sl]
        b_col = b[:, NH_A + h:NH_A + h + 1]
        li_col = li[:, h:h + 1]
        b_row = mixed_t[NH_A + h:NH_A + h + 1, :]
        li_row = mixed_t[h:h + 1, :]
        m_prev = m_scr[h:h + 1, 0:1]
        c_prev = c_scr[h]
        n_prev = n_scr[h]
        dmat = jnp.where(causal, b_col - b_row + li_row, NEG)
        m_inter = b_col + m_prev
        m_t = jnp.maximum(m_inter, jnp.max(dmat, axis=-1, keepdims=True))
        s = _dot_nt(qh, kh) * jnp.exp(dmat - m_t)
        a_inter = jnp.exp(m_inter - m_t)
        num = _dot(s, vh) + a_inter * _dot(qh, c_prev)
        den = jnp.sum(s, axis=-1, keepdims=True) + a_inter * jnp.sum(qh * n_prev, axis=-1, keepdims=True)
        hh = num / jnp.maximum(jnp.abs(den), jnp.exp(-m_t))
        b_last = b_col[L - 1:L, :]
        g_col = b_last - b_col + li_col
        m_new = jnp.maximum(b_last + m_prev, jnp.max(g_col, axis=0, keepdims=True))
        w_col = jnp.exp(g_col - m_new)
        decay = jnp.exp(b_last + m_prev - m_new)
        kw = kh * w_col
        c_scr[h] = decay * c_prev + _dot(kw.T, vh)
        n_scr[h] = decay * n_prev + jnp.sum(kw, axis=0, keepdims=True)
        m_scr[h:h + 1, :] = jnp.broadcast_to(m_new, (1, LANES))
        hn = hh * lax.rsqrt(jnp.mean(hh * hh, axis=-1, keepdims=True) + EPS) * hg_ref[h:h + 1, :]
        yh = hn * _sigmoid(og[:, sl])
        y_ref[0, :, sl] = yh[:rows]

    @pl.when(ci == nci - 1)
    def _():
        cout_ref[0] = c_scr[...]
        nout_ref[0] = n_scr[...]
        mout_ref[0] = m_scr[...]


def _mlstm(proj, b_if_row, head_gain, c0, n0, m0, *, rows, valid, chunk, zero_init):
    batch, t, _ = proj.shape
    nch = t // rows
    inner = NH_A * DK_A

    def colblk(j, width):
        return pl.BlockSpec((1, rows, width), lambda b, c, j=j: (b, c, j))

    state_c = pl.BlockSpec((1, NH_A, DK_A, DK_A), lambda b, c: (b, 0, 0, 0))
    state_n = pl.BlockSpec((1, NH_A, 1, DK_A), lambda b, c: (b, 0, 0, 0))
    state_m = pl.BlockSpec((1, 8, LANES), lambda b, c: (b, 0, 0))
    return pl.pallas_call(
        functools.partial(_mlstm_kernel, rows=rows, valid=valid, chunk=chunk, zero_init=zero_init),
        grid=(batch, nch),
        in_specs=[colblk(0, inner), colblk(1, inner), colblk(2, inner), colblk(3, inner),
                  colblk(4 * inner // LANES, LANES),
                  pl.BlockSpec((1, LANES), lambda b, c: (0, 0)),
                  pl.BlockSpec((NH_A, DK_A), lambda b, c: (0, 0)),
                  state_c, state_n, state_m],
        out_specs=[pl.BlockSpec((1, rows, inner), lambda b, c: (b, c, 0)), state_c, state_n, state_m],
        out_shape=[jax.ShapeDtypeStruct((batch, t, inner), F32),
                   jax.ShapeDtypeStruct((batch, NH_A, DK_A, DK_A), F32),
                   jax.ShapeDtypeStruct((batch, NH_A, 1, DK_A), F32),
                   jax.ShapeDtypeStruct((batch, 8, LANES), F32)],
        scratch_shapes=[pltpu.VMEM((NH_A, DK_A, DK_A), F32), pltpu.VMEM((NH_A, 1, DK_A), F32),
                        pltpu.VMEM((8, LANES), F32)],
        compiler_params=_cparams("arbitrary", "arbitrary"),
        name="mlstm",
    )(proj, proj, proj, proj, proj, b_if_row, head_gain, c0, n0, m0)


def _compress_body(r_refs, pos_ref, w1_ref, w2_ref, kg_ref, o_ref, nb):
    acc = None
    for lp in range(BLOCK // 2):
        pieces = []
        for g in range(NKV):
            xa = r_refs[g][pl.ds(2 * lp, nb, stride=BLOCK), :]
            xb = r_refs[g][pl.ds(2 * lp + 1, nb, stride=BLOCK), :]
            pieces.append(jnp.concatenate([xa, xb], axis=1))
        pa = pos_ref[2 * lp:2 * lp + 1, 0:LANES]
        pb = pos_ref[2 * lp + 1:2 * lp + 2, 0:LANES]
        pieces.append(jnp.broadcast_to(jnp.concatenate([pa, pb], axis=1), (8, 2 * LANES)))
        x = jnp.concatenate(pieces, axis=0).astype(BF16)
        d = jnp.dot(x, w1_ref[lp], preferred_element_type=F32)
        acc = d if acc is None else acc + d
    pre = acc[:NKV * nb] + acc[NKV * nb:NKV * nb + 1]
    hid = pre * _sigmoid(pre)
    out = jnp.dot(hid.astype(BF16), w2_ref[...], preferred_element_type=F32)
    is_k = lax.broadcasted_iota(I32, out.shape, 1) < HD
    ms = jnp.sum(jnp.where(is_k, out * out, 0.0), axis=-1, keepdims=True) * (1.0 / HD)
    out = jnp.where(is_k, out * lax.rsqrt(ms + EPS) * kg_ref[...], out)
    for g in range(NKV):
        o_ref[:, g * LANES:(g + 1) * LANES] = out[g * nb:(g + 1) * nb]


def _compress_kernel(r0, r1, r2, r3, pos_ref, w1_ref, w2_ref, kg_ref, o_ref, *, nb):
    _compress_body((r0, r1, r2, r3), pos_ref, w1_ref, w2_ref, kg_ref, o_ref, nb)


def _compress_paged_kernel(pt_ref, *refs, nb):
    npages = nb * BLOCK // PAGE
    pages = refs[:npages]
    pos_ref, w1_ref, w2_ref, kg_ref, o_ref = refs[npages:npages + 5]
    r_scrs = refs[npages + 5:]
    for p in range(npages):
        for g in range(NKV):
            r_scrs[g][p * PAGE:(p + 1) * PAGE, :] = pages[p][0, g * LANES:(g + 1) * LANES, :].T
    _compress_body(r_scrs, pos_ref, w1_ref, w2_ref, kg_ref, o_ref, nb)


def _compress_paged(cache_t, page_table, pos_rows, w1p, w2p, kgain_row, nb):
    batch, npg = page_table.shape
    w = cache_t.shape[1]
    npages = nb * BLOCK // PAGE
    steps = npg // npages

    def page_spec(p):
        return pl.BlockSpec((1, w, PAGE), lambda b, j, pt, p=p: (pt[b, j * npages + p], 0, 0))

    def const(shape):
        return pl.BlockSpec(shape, lambda b, j, pt: tuple(0 for _ in shape))

    return pl.pallas_call(
        functools.partial(_compress_paged_kernel, nb=nb),
        grid_spec=pltpu.PrefetchScalarGridSpec(
            num_scalar_prefetch=1,
            grid=(batch, steps),
            in_specs=[page_spec(p) for p in range(npages)]
                     + [const((BLOCK, w)), const(w1p.shape), const(w2p.shape), const((1, LANES))],
            out_specs=pl.BlockSpec((nb, w), lambda b, j, pt: (b * steps + j, 0)),
            scratch_shapes=[pltpu.VMEM((nb * BLOCK, LANES), F32) for _ in range(NKV)]),
        out_shape=jax.ShapeDtypeStruct((batch * npg * PAGE // BLOCK, w), F32),
        compiler_params=_cparams("arbitrary", "arbitrary"),
        name="compress_paged",
    )(page_table, *([cache_t] * npages), pos_rows, w1p, w2p, kgain_row)


def _compress(rows, pos_rows, w1p, w2p, kgain_row, nb):
    t, w = rows.shape
    nblk = t // BLOCK
    return pl.pallas_call(
        functools.partial(_compress_kernel, nb=nb),
        grid=(nblk // nb,),
        in_specs=[pl.BlockSpec((nb * BLOCK, LANES), lambda i, g=g: (i, g)) for g in range(NKV)]
                 + [_const_spec((BLOCK, w)), _const_spec(w1p.shape), _const_spec(w2p.shape),
                    _const_spec((1, LANES))],
        out_specs=pl.BlockSpec((nb, w), lambda i: (i, 0)),
        out_shape=jax.ShapeDtypeStruct((nblk, w), F32),
        compiler_params=_cparams("arbitrary"),
        name="compress",
    )(rows, rows, rows, rows, pos_rows, w1p, w2p, kgain_row)


def _bias_from_dist(dist, tab_ref, h):
    out = jnp.full(dist.shape, tab_ref[0, h], F32)
    for kk in range(1, N_BUCKETS):
        out = jnp.where(dist >= BUCKET_THR[kk - 1], tab_ref[kk, h], out)
    return out


def _bias_tiles_kernel(tab_ref, kq_ref, kqd_ref, qk_ref):
    a = pl.program_id(0)
    row = lax.broadcasted_iota(I32, (LANES, TQ), 0)
    col = lax.broadcasted_iota(I32, (LANES, TQ), 1)

    def head(h, _):
        kq = _bias_from_dist(a * LANES + col - row, tab_ref, h)
        kq_ref[0, h] = kq * LOG2E
        kqd_ref[0, h] = (kq - tab_ref[N_BUCKETS - 1, h]) * LOG2E
        qk_ref[0, h] = _bias_from_dist(a * LANES + row - col, tab_ref, h) * LOG2E
        return 0

    lax.fori_loop(0, kq_ref.shape[1], head, 0)


def _bias_tiles(rel_bias):
    nh = rel_bias.shape[1]
    spec = pl.BlockSpec((1, nh, LANES, TQ), lambda a: (a, 0, 0, 0))
    shape = jax.ShapeDtypeStruct((N_BIAS_TILES, nh, LANES, TQ), F32)
    return pl.pallas_call(
        _bias_tiles_kernel,
        grid=(N_BIAS_TILES,),
        in_specs=[pl.BlockSpec(memory_space=pltpu.SMEM)],
        out_specs=[spec, spec, spec],
        out_shape=[shape, shape, shape],
        compiler_params=_cparams("arbitrary"),
        name="bias_tiles",
    )(rel_bias)


def _stack_heads(q):
    z = jnp.zeros((q.shape[0], HD), q.dtype)
    return jnp.concatenate([jnp.concatenate([q[:, r * HD:(r + 1) * HD], z], axis=1) for r in range(REP)],
                           axis=0)


def _tile_heads(x):
    return jnp.concatenate([x] * REP, axis=1)


def _pad_rows(x, n):
    if x.shape[0] == n:
        return x
    return jnp.concatenate([x, jnp.zeros((n - x.shape[0],) + x.shape[1:], x.dtype)], axis=0)


def _bias4(dist, tab_ref, g):
    biases = [jnp.full(dist.shape, tab_ref[0, g * REP + r], F32) for r in range(REP)]
    for kk in range(1, N_BUCKETS):
        reached = dist >= BUCKET_THR[kk - 1]
        biases = [jnp.where(reached, tab_ref[kk, g * REP + r], biases[r]) for r in range(REP)]
    return biases


def _cmp_branch(qc, kcvc_ref, tab_ref, g, pos, pos0, x_scr, nb):
    tq = qc.shape[0]
    kcvc = kcvc_ref[0:nb, :]
    q4 = _stack_heads(qc)
    lt = _dot_nt(kcvc, q4)
    n_io = lax.broadcasted_iota(I32, (nb, tq), 0)
    dist = pos - (n_io * BLOCK + (BLOCK - 1))
    vis = dist >= 0
    if nb <= CMP_BIAS_ROWS:
        biased = [lt[:, r * tq:(r + 1) * tq] + b for r, b in enumerate(_bias4(dist, tab_ref, g))]
    else:
        far = [tab_ref[N_BUCKETS - 1, g * REP + r] for r in range(REP)]
        x_scr[0:nb, :] = jnp.concatenate([lt[:, r * tq:(r + 1) * tq] + far[r] for r in range(REP)], axis=1)
        first = (pos0 - (BUCKET_THR[-1] + BLOCK - 1)) // BLOCK + 1
        w0 = pl.multiple_of(jnp.clip(first // 8 * 8, 0, nb - CMP_BIAS_ROWS), 8)
        w_io = lax.broadcasted_iota(I32, (CMP_BIAS_ROWS, tq), 0) + w0
        near = _bias4(pos - (w_io * BLOCK + (BLOCK - 1)), tab_ref, g)
        lt_w = _dot_nt(kcvc_ref[pl.ds(w0, CMP_BIAS_ROWS), :], q4)
        x_scr[pl.ds(w0, CMP_BIAS_ROWS), :] = jnp.concatenate(
            [lt_w[:, r * tq:(r + 1) * tq] + near[r] for r in range(REP)], axis=1)
        biased = [x_scr[0:nb, r * tq:(r + 1) * tq] for r in range(REP)]
    probs = []
    for r in range(REP):
        x = jnp.where(vis, biased[r], NEG)
        e = jnp.exp(x - jnp.max(x, axis=0, keepdims=True))
        p = e * (1.0 / jnp.sum(e, axis=0, keepdims=True))
        probs.append(jnp.where(vis, p, 0.0))
    score = probs[0] + probs[1] + probs[2] + probs[3]
    oc_t = _dot(kcvc.T[HD:, :], jnp.concatenate(probs, axis=1))
    cand = n_io < pos // BLOCK
    return oc_t, jnp.where(cand, score, -jnp.inf)


def _topk_rows(score, n_top, pick_fn=None):
    nb, tq = score.shape
    n_f = lax.broadcasted_iota(I32, (nb, tq), 0).astype(F32)
    s = score
    for it in range(n_top):
        mx = jnp.max(s, axis=0, keepdims=True)
        idx = jnp.min(jnp.where(s == mx, n_f, float(nb)), axis=0, keepdims=True)
        if pick_fn is not None:
            pick_fn(it, idx, mx > -jnp.inf)
        s = jnp.where(n_f == idx, -jnp.inf, s)
    return (s == -jnp.inf) & (score > -jnp.inf)


def _flash_init(nl):
    return jnp.full((1, nl), NEG, F32), jnp.zeros((VT_ROWS, nl), F32)


def _online_step(carry, tiles):
    m_c, acc_c = carry
    m_new = m_c
    for x2, _ in tiles:
        m_new = jnp.maximum(m_new, jnp.max(x2, axis=0, keepdims=True))
    acc = jnp.exp2(m_c - m_new) * acc_c
    for x2, vt in tiles:
        acc = acc + jnp.dot(vt, jnp.exp2(x2 - m_new).astype(BF16), preferred_element_type=F32)
    return m_new, acc


def _merge_parts(a, b):
    m = jnp.maximum(a[0], b[0])
    return m, jnp.exp2(a[0] - m) * a[1] + jnp.exp2(b[0] - m) * b[1]


def _normalise(acc):
    return acc[:HD] / acc[HD:HD + 1]


def _untranspose_heads(o_t, tq):
    halves = []
    for p in range(REP // 2):
        pair = jnp.concatenate([o_t[:, (2 * p) * tq:(2 * p + 1) * tq],
                                o_t[:, (2 * p + 1) * tq:(2 * p + 2) * tq]], axis=0)
        halves.append(pair.T)
    return jnp.concatenate(halves, axis=1)


def _gate_row(gt_ref, br, g):
    return jnp.concatenate([gt_ref[pl.ds(br * NKV * REP + g * REP + r, 1), :] for r in range(REP)], axis=1)


def _attn_prompt_kernel(tab_ref, qc_ref, qs_ref, qw_ref, gt_ref, kcvc_ref, ksel_ref, vselt_ref,
                        kw0, kw1, kw2, kw3, kw4, vw0, vw1, vw2, vw3, vw4, bt_ref, btd_ref, o_ref,
                        sel_scr, xc_scr, oct_scr):
    g = pl.program_id(0)
    i = pl.program_id(1)
    tq = TQ
    nl = REP * tq
    t0 = i * tq
    pos = t0 + lax.broadcasted_iota(I32, (1, tq), 1)

    nb = kcvc_ref.shape[0]
    sel_scr[...] = jnp.full((nb, tq), NEG, F32)
    classes = [rows for rows in CMP_ROW_CLASSES if rows < nb] + [nb]
    for k, rows in enumerate(classes):
        lo = classes[k - 1] if k else 0

        @pl.when((2 * i + 2 > lo) & ((2 * i + 2 <= rows) | (rows == nb)))
        def _(rows=rows):
            oc, score = _cmp_branch(qc_ref[...], kcvc_ref, tab_ref, g, pos, t0, xc_scr, rows)
            oct_scr[...] = oc
            sel_scr[0:rows, :] = jnp.where(_topk_rows(score, TOPK - 1), 0.0, NEG)

    oc_t = oct_scr[...]

    def bias_tile(a):
        return jnp.concatenate([bt_ref[a, r] for r in range(REP)], axis=1)

    qs_t = (_stack_heads(qs_ref[...]) * LOG2E).T[:HD].astype(BF16)
    far_bias = jnp.concatenate([jnp.full((1, tq), tab_ref[N_BUCKETS - 1, g * REP + r] * LOG2E, F32)
                                for r in range(REP)], axis=1)
    far_hi = far_bias.astype(BF16).astype(F32)
    row16 = lax.broadcasted_iota(I32, (SUPER_BLOCKS, nl), 0)
    const_rows = jnp.where(row16 == 0, far_hi, jnp.where(row16 == 1, far_bias - far_hi, 0.0)).astype(BF16)
    pad_rows = jnp.zeros((LANES - CONST_LANE - SUPER_BLOCKS, nl), BF16)

    def supers_step(supers, near, carry):
        tiles = []
        for jj in supers:
            mrows = _tile_heads(sel_scr[pl.ds(pl.multiple_of(jj * SUPER_BLOCKS, SUPER_BLOCKS), SUPER_BLOCKS), :])
            q_aug = jnp.concatenate([qs_t, mrows.astype(BF16), const_rows, pad_rows], axis=0)
            for c in range(SUPER // KT):
                s0 = pl.multiple_of(jj * SUPER + c * KT, KT)
                x = jnp.dot(ksel_ref[pl.ds(s0, KT), :], q_aug, preferred_element_type=F32)
                if near:
                    tile0 = jj * (SUPER // LANES) + c * (KT // LANES)
                    x = x + jnp.concatenate([bias_delta(jnp.clip(i - tile0 - a, 0, N_BIAS_TILES - 1))
                                             for a in range(KT // LANES)], axis=0)
                tiles.append((x, jnp.max(x, axis=0, keepdims=True), s0))
        parts = [(m, jnp.dot(vselt_ref[0:VT_ROWS, pl.ds(s0, KT)], jnp.exp2(x - m).astype(BF16),
                             preferred_element_type=F32)) for x, m, s0 in tiles]
        for part in parts:
            carry = _merge_parts(carry, part)
        return carry

    def make_body(near):
        return lambda step, carry: supers_step(
            [step * SUPERS_PER_STEP + u for u in range(SUPERS_PER_STEP)], near, carry)

    def bias_delta(a):
        return jnp.concatenate([btd_ref[a, r] for r in range(REP)], axis=1)

    keys_per_step = SUPER * SUPERS_PER_STEP
    n_super = (2 * i) // SUPER_BLOCKS + 1
    n_steps = n_super // SUPERS_PER_STEP
    n_far = jnp.minimum(jnp.maximum(t0 - (N_BIAS_TILES - 1) * LANES + LANES, 0) // keys_per_step, n_steps)
    carry = lax.fori_loop(0, n_far, make_body(False), _flash_init(nl))
    carry = lax.fori_loop(n_far, n_steps, make_body(True), carry)
    carry = lax.cond(n_super % SUPERS_PER_STEP == 1,
                     lambda c: supers_step([n_super - 1], True, c), lambda c: c, carry)

    sj = lax.broadcasted_iota(I32, (LANES, nl), 0)
    ti4 = _tile_heads(lax.broadcasted_iota(I32, (1, tq), 1))
    t0a = pl.multiple_of(t0, LANES)
    q_diag = jnp.concatenate([qs_t, jnp.zeros((SUPER_BLOCKS, nl), BF16), const_rows, pad_rows], axis=0)
    qw_bf = (_stack_heads(qw_ref[...]) * LOG2E).astype(BF16)
    kw_refs = (kw0, kw1, kw2, kw3, kw4)
    vw_refs = (vw0, vw1, vw2, vw3, vw4)
    x_diag = jnp.dot(ksel_ref[pl.ds(t0a, LANES), :], q_diag, preferred_element_type=F32)
    x_win = [_dot_nt(kw_refs[j][...], qw_bf) for j in range(5)]
    x = jnp.where((sj // BLOCK == ti4 // BLOCK) & (sj <= ti4), x_diag + bias_delta(0), NEG)
    _, accs = _online_step(carry, [(x, vselt_ref[0:VT_ROWS, pl.ds(t0a, LANES)])])

    tiles = []
    for j in range(5):
        dist = LANES * (4 - j) + ti4 - sj
        ok = jnp.where((dist >= 0) & (dist <= WINDOW), 1.0, 0.0) * jnp.where(i - 4 + j >= 0, 1.0, 0.0)
        tiles.append((jnp.where(ok > 0.5, x_win[j] + bias_tile(4 - j), NEG), vw_refs[j][0:VT_ROWS, :]))
    _, accw = _online_step(_flash_init(nl), tiles)

    out_t = (_gate_row(gt_ref, 0, g) * oc_t + _gate_row(gt_ref, 1, g) * _normalise(accs)
             + _gate_row(gt_ref, 2, g) * _normalise(accw))
    o_ref[...] = _untranspose_heads(out_t, tq)


def _attn_prompt(rel_bias, qs, gates_t, kcvc, sel_bf, selt_bf, win_bf, wint_bf, bias_tiles, bias_delta_tiles):
    t = qs.shape[0]
    nb = kcvc.shape[0]
    width = NKV * REP * HD

    def qspec(br):
        return pl.BlockSpec((TQ, REP * HD), lambda g, i, br=br: (i, br * NKV + g))

    def kw_spec(j):
        return pl.BlockSpec((LANES, LANES), lambda g, i, j=j: (jnp.maximum(i - 4 + j, 0), g))

    def vw_spec(j):
        return pl.BlockSpec((LANES, LANES), lambda g, i, j=j: (g, jnp.maximum(i - 4 + j, 0)))

    return pl.pallas_call(
        _attn_prompt_kernel,
        grid=(NKV, t // TQ),
        in_specs=[pl.BlockSpec(memory_space=pltpu.SMEM),
                  qspec(0), qspec(1), qspec(2),
                  pl.BlockSpec((LANES, TQ), lambda g, i: (0, i)),
                  pl.BlockSpec((nb, LANES), lambda g, i: (0, g)),
                  pl.BlockSpec((t, LANES), lambda g, i: (0, g)),
                  pl.BlockSpec((LANES, t), lambda g, i: (g, 0))]
                 + [kw_spec(j) for j in range(5)] + [vw_spec(j) for j in range(5)]
                 + [pl.BlockSpec((N_BIAS_TILES, REP, LANES, TQ), lambda g, i: (0, g, 0, 0))] * 2,
        out_specs=pl.BlockSpec((TQ, REP * HD), lambda g, i: (i, g)),
        out_shape=jax.ShapeDtypeStruct((t, width), F32),
        scratch_shapes=[pltpu.VMEM((nb, TQ), F32), pltpu.VMEM((nb, REP * TQ), F32),
                        pltpu.VMEM((HD, REP * TQ), F32)],
        compiler_params=_cparams("arbitrary", "arbitrary"),
        name="attn_prompt",
    )(rel_bias, qs, qs, qs, gates_t, kcvc, sel_bf, selt_bf,
      win_bf, win_bf, win_bf, win_bf, win_bf, wint_bf, wint_bf, wint_bf, wint_bf, wint_bf, bias_tiles,
      bias_delta_tiles)


def _attn_s1_kernel(tab_ref, qc_ref, kcvc_ref, oct_ref, idx_ref, xc_scr, *, q0):
    g = pl.program_id(1)
    tq = TQ
    pos = q0 + lax.broadcasted_iota(I32, (1, tq), 1)
    oc_t, score = _cmp_branch(_pad_rows(qc_ref[0], tq), kcvc_ref, tab_ref, g, pos, q0, xc_scr,
                              kcvc_ref.shape[0])
    oct_ref[0] = oc_t

    def pick(it, idx, ok):
        idx_ref[0, it:it + 1, :] = jnp.where(ok, idx, -1.0).astype(I32)

    _topk_rows(score, TOPK - 1, pick)
    idx_ref[0, TOPK - 1:TOPK, :] = jnp.full((1, tq), -1, I32)


def _attn_s1(rel_bias, qs8, kcvc, q0):
    batch = qs8.shape[0]
    nb = kcvc.shape[0] // batch
    return pl.pallas_call(
        functools.partial(_attn_s1_kernel, q0=q0),
        grid=(batch, NKV),
        in_specs=[pl.BlockSpec(memory_space=pltpu.SMEM),
                  pl.BlockSpec((1, 8, REP * HD), lambda b, g: (b, 0, g)),
                  pl.BlockSpec((nb, LANES), lambda b, g: (b, g))],
        out_specs=[pl.BlockSpec((1, HD, REP * TQ), lambda b, g: (b * NKV + g, 0, 0)),
                   pl.BlockSpec((1, TOPK, TQ), lambda b, g: (b * NKV + g, 0, 0))],
        out_shape=[jax.ShapeDtypeStruct((batch * NKV, HD, REP * TQ), F32),
                   jax.ShapeDtypeStruct((batch * NKV, TOPK, TQ), I32)],
        scratch_shapes=[pltpu.VMEM((nb, REP * TQ), F32)],
        compiler_params=_cparams("arbitrary", "arbitrary"),
        name="attn_sample_select",
    )(rel_bias, qs8, kcvc)


def _attn_s2_kernel(ids_ref, phys_ref, qs_ref, qw_ref, gt_ref, oct_ref, *refs, q0, nvalid):
    nsel = nvalid * (TOPK - 1)
    kb = refs[:nsel]
    snew_ref, cwin_ref, wnew_ref, bt_ref, o_ref = refs[nsel:]
    b = pl.program_id(0)
    g = pl.program_id(1)
    tq = TQ_SAMPLE
    nr = REP * tq
    tr = lax.broadcasted_iota(I32, (nr, LANES), 0) % tq
    sj = lax.broadcasted_iota(I32, (nr, LANES), 1)
    base = (b * NKV + g) * nvalid * TOPK

    def compact(x):
        return jnp.concatenate([x[:, r * TQ:r * TQ + tq] for r in range(REP)], axis=1)

    def bias_rows(a):
        return jnp.concatenate([bt_ref[a, r][:tq, :] for r in range(REP)], axis=0)

    def attend(q_bf, cached, new_ref):
        xs, vts = [], []
        for tile_t, a, ok in cached:
            xs.append(jnp.where(ok, _dot(q_bf, tile_t) + bias_rows(a), NEG))
            vts.append(tile_t.astype(BF16))
        blk = _pad_rows(new_ref[0], LANES).astype(BF16)
        xs.append(jnp.where((sj <= tr) & (sj < nvalid), _dot_nt(q_bf, blk) + bias_rows(0), NEG))
        x = jnp.concatenate(xs, axis=1)
        p = jnp.exp2(x - jnp.max(x, axis=1, keepdims=True))
        l = jnp.sum(p, axis=1, keepdims=True)
        p = p.astype(BF16)
        nk = x.shape[1] - LANES
        acc = _dot_nt(p[:, :nk], jnp.concatenate(vts, axis=1)) + _dot(p[:, nk:], blk)
        return acc[:, HD:] / l

    cached = []
    for t in range(nvalid):
        for k in range(TOPK - 1):
            n = ids_ref[base + t * TOPK + k]
            nc = jnp.maximum(n, 0)
            a = jnp.clip(q0 // LANES - nc // 2, 0, N_BIAS_TILES - 1)
            lo = jnp.where(n >= 0, (nc % 2) * BLOCK, LANES)
            cached.append((kb[t * (TOPK - 1) + k][0], a, (sj >= lo) & (sj < lo + BLOCK) & (tr == t)))
    o_sel = attend((_stack_heads(_pad_rows(qs_ref[0], tq)) * LOG2E).astype(BF16), cached, snew_ref)

    cached = []
    for j in range(WINDOW // LANES):
        dist = (WINDOW - LANES * j) + tr - sj
        cached.append((cwin_ref[0, :, j * LANES:(j + 1) * LANES], WINDOW // LANES - j,
                       (dist >= 0) & (dist <= WINDOW)))
    o_win = attend((_stack_heads(_pad_rows(qw_ref[0], tq)) * LOG2E).astype(BF16), cached, wnew_ref)

    gt2 = gt_ref.at[0]
    o_cmp = _pad_rows(compact(oct_ref[0]), LANES).T[:, :HD]
    row8 = lax.broadcasted_iota(I32, (8, nr), 0)
    g8 = jnp.zeros((8, nr), F32)
    for br in range(3):
        g8 = jnp.where(row8 == br, compact(_gate_row(gt2, br, g)), g8)
    gates = _pad_rows(g8, LANES).T
    out = gates[:, 0:1] * o_cmp + gates[:, 1:2] * o_sel + gates[:, 2:3] * o_win
    o_ref[0] = jnp.concatenate([out[r * tq:r * tq + 8] for r in range(REP)], axis=1)


def _attn_s2(ids, phys, qs8, gates_t, oct, cache_sel_t, sel_new8, cache_win_t, win_new8, bias_tiles, q0, nvalid):
    batch = qs8.shape[0]
    nsel = nvalid * (TOPK - 1)

    def kb_spec(t, k):
        return pl.BlockSpec((1, LANES, PAGE),
                            lambda b, g, ids_r, phys_r, t=t, k=k:
                            (phys_r[((b * NKV + g) * nvalid + t) * TOPK + k], g, 0))

    def bg(shape, col):
        return pl.BlockSpec(shape, lambda b, g, ids_r, phys_r, col=col: (b, 0, col * NKV + g))

    return pl.pallas_call(
        functools.partial(_attn_s2_kernel, q0=q0, nvalid=nvalid),
        grid_spec=pltpu.PrefetchScalarGridSpec(
            num_scalar_prefetch=2,
            grid=(batch, NKV),
            in_specs=[bg((1, 8, REP * HD), 1), bg((1, 8, REP * HD), 2),
                      pl.BlockSpec((1, LANES, TQ), lambda b, g, ids_r, phys_r: (b, 0, 0)),
                      pl.BlockSpec((1, HD, REP * TQ), lambda b, g, ids_r, phys_r: (b * NKV + g, 0, 0))]
                     + [kb_spec(t, k) for t in range(nvalid) for k in range(TOPK - 1)]
                     + [bg((1, 8, LANES), 0),
                        pl.BlockSpec((1, LANES, WINDOW), lambda b, g, ids_r, phys_r: (b, g, 0)),
                        bg((1, 8, LANES), 0),
                        pl.BlockSpec((N_BIAS_TILES, REP, LANES, TQ), lambda b, g, ids_r, phys_r: (0, g, 0, 0))],
            out_specs=bg((1, 8, REP * HD), 0)),
        out_shape=jax.ShapeDtypeStruct((batch, 8, NKV * REP * HD), F32),
        compiler_params=_cparams("arbitrary", "arbitrary"),
        name="attn_sample",
    )(ids, phys, qs8, qs8, gates_t, oct, *([cache_sel_t] * nsel), sel_new8, cache_win_t, win_new8, bias_tiles)


def _pad_cols(w, n):
    return jnp.pad(w, ((0, 0), (0, n - w.shape[1])))


def _prep(p):
    inner = NH_A * DK_A
    width = NKV * REP * HD
    q = {}
    q['w_in'] = _pad_cols(p['w_in_a'][0], 4 * inner + LANES).astype(BF16)
    q['b_if'] = _pad_cols(p['b_if_a'][0][None, :], LANES)
    q['w_out'] = p['w_out_a'][0].astype(BF16)
    q['w_ff1'] = [p['w_ff1'][l].astype(BF16) for l in range(2)]
    q['w_ff2'] = [p['w_ff2'][l].astype(BF16) for l in range(2)]
    q['w_kv'] = p['w_kv'].astype(BF16)
    q['w_q'] = _pad_cols(p['w_q_b'][0], 3 * width + LANES).astype(BF16)
    q['b_gate'] = _pad_cols(p['b_gate_b'][0][None, :], LANES)
    q['w_o'] = p['w_o_b'][0].astype(BF16)
    q['q_gain'] = (jnp.tile(p['q_norm_b'][0][:, None, :], (1, NKV * REP, 1)) * (HD ** -0.5)).reshape(1, 3 * width)
    seg = np.arange(256) // HD
    q['bd'] = jnp.asarray((seg[:, None] == seg[None, :]).astype(np.float32) / HD, dtype=BF16)
    ones = jnp.ones((HD,), F32)
    q['k_gain'] = jnp.concatenate([jnp.tile(jnp.concatenate([p['k_norm'][br], ones]), NKV) for br in (1, 2)])[None, :]
    q['k_mask'] = jnp.tile(jnp.concatenate([ones, 0.0 * ones]), 2 * NKV)[None, :]
    q['k_gain0'] = jnp.concatenate([p['k_norm'][0], ones])[None, :]
    w1 = p['w_cmp1']
    z = jnp.zeros_like(w1[0])
    per_l = jnp.concatenate([jnp.concatenate([w1[0], z], axis=-1), jnp.concatenate([z, w1[1]], axis=-1)], axis=1)
    q['w_cmp1'] = per_l.reshape(BLOCK // 2, 2 * 2 * HD, 2 * w1.shape[-1]).astype(BF16)
    w2 = p['w_cmp2']
    z2 = jnp.zeros_like(w2[0])
    q['w_cmp2'] = jnp.concatenate([jnp.concatenate([w2[0], z2], axis=1), jnp.concatenate([z2, w2[1]], axis=1)],
                                  axis=0).astype(BF16)
    q['pos_rows'] = jnp.tile(p['cmp_pos'].reshape(BLOCK, 2 * HD), (1, NKV))
    return q


def _feature_major(cache):
    n, t = cache.shape[:2]
    return jnp.transpose(cache, (0, 2, 3, 4, 1)).reshape(n, -1, t)


def _mods(ada_rows, rep):
    k = ada_rows.shape[1] // D_MODEL
    out = []
    for j in range(k):
        a = ada_rows[:, j * D_MODEL:(j + 1) * D_MODEL]
        out.append(jnp.repeat(a, rep, axis=0) if rep > 1 else a)
    return out


def _layer0(x, mods, p, q, mlstm_fn, tm, tmf):
    sh1, sc1, g1, sh2, sc2, g2 = mods
    proj = _proj_plain(x, p['norm_mix'][0][None, :], sh1, sc1, q['w_in'], tm)
    y, states = mlstm_fn(proj)
    x = _out_proj(x, g1, y, q['w_out'], tm)
    x = _ffn(x, p['norm_ffn'][0][None, :], sh2, sc2, g2, q['w_ff1'][0], q['w_ff2'][0], tmf)
    return x, states


def _layer1_tail(x, attn, mods, p, q, tm, tmf):
    _, _, g1, sh2, sc2, g2 = mods
    x = _out_proj(x, g1, attn, q['w_o'], tm)
    return _ffn(x, p['norm_ffn'][1][None, :], sh2, sc2, g2, q['w_ff1'][1], q['w_ff2'][1], tmf)


def kernel(x_prompt, x_sample, cache_cmp, cache_sel, cache_win, state_C, state_n, state_m, page_table,
           c_prompt, c_sample, w_ada, b_ada, norm_mix, norm_ffn, w_ff1, w_ff2, w_in_a, b_if_a,
           head_norm_a, w_out_a, w_ada_kv, b_ada_kv, norm_kv, w_kv, k_norm, cmp_pos, w_cmp1, w_cmp2,
           w_q_b, b_gate_b, q_norm_b, w_o_b, rel_bias):
    p = dict(norm_mix=norm_mix, norm_ffn=norm_ffn, w_ff1=w_ff1, w_ff2=w_ff2, w_in_a=w_in_a, b_if_a=b_if_a,
             w_out_a=w_out_a, w_kv=w_kv, k_norm=k_norm, cmp_pos=cmp_pos, w_cmp1=w_cmp1, w_cmp2=w_cmp2,
             w_q_b=w_q_b, b_gate_b=b_gate_b, q_norm_b=q_norm_b, w_o_b=w_o_b)
    q = _prep(p)
    bp, tp, d = x_prompt.shape
    bs, ts, _ = x_sample.shape
    past = page_table.shape[1] * PAGE
    wbuf = cache_win.shape[1]
    assert bp == 1 and ts < BLOCK and ts <= 8 and wbuf == WINDOW and past % LANES == 0
    assert tp % (SUPER * SUPERS_PER_STEP) == 0
    width = NKV * REP * HD
    kvw = NKV * 2 * HD

    nc = bp + bs
    c_all = jnp.pad(jnp.concatenate([c_prompt, c_sample], axis=0), ((0, -nc % 8), (0, 0)))
    ada = [_ada(c_all, w_ada[l], b_ada[l]) for l in range(2)]
    ada_kv = _ada(c_all, w_ada_kv, b_ada_kv)
    bias_kq, bias_kq_delta, bias_qk = _bias_tiles(rel_bias)
    head_gain = head_norm_a[0]

    xp = x_prompt.reshape(tp, d)
    zc = jnp.zeros((bp, NH_A, DK_A, DK_A), F32)
    zn = jnp.zeros((bp, NH_A, 1, DK_A), F32)
    zm = jnp.zeros((bp, 8, LANES), F32)

    def mlstm_prompt(proj):
        y, c, n, m = _mlstm(proj.reshape(bp, tp, -1), q['b_if'], head_gain, zc, zn, zm,
                            rows=256, valid=256, chunk=256, zero_init=True)
        return y.reshape(tp, -1), (c, n, m)

    xp, (pc, pn, pm) = _layer0(xp, _mods(ada[0][:bp], 1), p, q, mlstm_prompt, 256, 512)
    sh, sc = _mods(ada_kv[:bp], 1)
    p_cmp, _, _, p_cmp_t, p_sel_t, p_win_t, sel_bf, selt_bf, win_bf, wint_bf = _proj_kv(
        xp, norm_kv[None, :], sh, sc, q['w_kv'], q['bd'], q['k_gain'], q['k_mask'], 256)
    kcvc_p = _compress(p_cmp, q['pos_rows'], q['w_cmp1'], q['w_cmp2'], q['k_gain0'],
                       min(CMP_BLOCKS_PER_STEP, tp // BLOCK))
    mods1 = _mods(ada[1][:bp], 1)
    qs_p, gt_p = _proj_q(xp, norm_mix[1][None, :], mods1[0], mods1[1], q['w_q'], q['bd'], q['q_gain'],
                         q['b_gate'], 256)
    attn_p = _attn_prompt(rel_bias, qs_p, gt_p, kcvc_p, sel_bf, selt_bf, win_bf, wint_bf, bias_kq, bias_kq_delta)
    y_prompt = _layer1_tail(xp, attn_p, mods1, p, q, 256, 512).reshape(bp, tp, d)

    ms = bs * ts
    xs = x_sample.reshape(ms, d)
    m0 = jnp.broadcast_to(jnp.pad(state_m[0], ((0, 0), (0, 8 - NH_A)))[:, :, None], (bs, 8, LANES))

    def mlstm_sample(proj):
        proj8 = jnp.pad(proj.reshape(bs, ts, -1), ((0, 0), (0, 8 - ts), (0, 0)))
        y, c, n, m = _mlstm(proj8, q['b_if'], head_gain, state_C[0], state_n[0][:, :, None, :], m0,
                            rows=8, valid=ts, chunk=LANES, zero_init=False)
        return y[:, :ts].reshape(ms, -1), (c, n, m)

    xs, (sc_, sn_, sm_) = _layer0(xs, _mods(ada[0][bp:nc], ts), p, q, mlstm_sample, ms, ms)
    sh, sc = _mods(ada_kv[bp:nc], ts)
    s_cmp, s_sel, s_win = _proj_kv(xs, norm_kv[None, :], sh, sc, q['w_kv'], q['bd'], q['k_gain'],
                                   q['k_mask'], ms)[:3]
    kcvc_s = _compress_paged(_feature_major(cache_cmp), page_table, q['pos_rows'],
                             q['w_cmp1'], q['w_cmp2'], q['k_gain0'], min(CMP_BLOCKS_PER_STEP, past // BLOCK))
    mods1 = _mods(ada[1][bp:nc], ts)
    qs_s, gt_s = _proj_q(xs, norm_mix[1][None, :], mods1[0], mods1[1], q['w_q'], q['bd'], q['q_gain'],
                         q['b_gate'], ms)

    def pad8(a):
        return jnp.pad(a.reshape(bs, ts, -1), ((0, 0), (0, 8 - ts), (0, 0)))

    qs8 = pad8(qs_s)
    oct_s, idx_s = _attn_s1(rel_bias, qs8, kcvc_s, past)
    ids = jnp.transpose(idx_s.reshape(bs, NKV, TOPK, TQ)[:, :, :, :ts], (0, 1, 3, 2))
    idc = jnp.maximum(ids, 0)
    pages = jnp.take_along_axis(page_table, (idc // (PAGE // BLOCK)).reshape(bs, -1), axis=1).reshape(ids.shape)
    gt8 = jnp.pad(jnp.transpose(gt_s.reshape(LANES, bs, ts), (1, 0, 2)), ((0, 0), (0, 0), (0, TQ - ts)))
    attn_s = _attn_s2(ids.reshape(-1), pages.reshape(-1), qs8, gt8, oct_s,
                      _feature_major(cache_sel), pad8(s_sel), _feature_major(cache_win), pad8(s_win),
                      bias_qk, past, ts)
    y_sample = _layer1_tail(xs, attn_s[:, :ts].reshape(ms, width), mods1, p, q, ms, ms).reshape(bs, ts, d)

    rows5 = (NKV, 2, HD)

    def token_major(a_t):
        return jnp.transpose(a_t.reshape(*rows5, a_t.shape[1]), (3, 0, 1, 2))[None]

    s_win_all = jnp.concatenate([cache_win, s_win.reshape(bs, ts, *rows5)], axis=1)
    return (y_prompt, y_sample,
            pc[None], pn.reshape(1, bp, NH_A, DK_A), pm[None, :, :NH_A, 0],
            token_major(p_cmp_t), token_major(p_sel_t), token_major(p_win_t[:, tp - min(WINDOW, tp):]),
            sc_[None], sn_.reshape(1, bs, NH_A, DK_A), sm_[None, :, :NH_A, 0],
            s_cmp.reshape(bs, ts, *rows5), s_sel.reshape(bs, ts, *rows5),
            s_win_all[:, -min(WINDOW, wbuf + ts):])
```

```python
import functools
import math

import numpy as np
import jax
import jax.numpy as jnp
from jax import lax
from jax.experimental import pallas as pl
from jax.experimental.pallas import tpu as pltpu

F32 = jnp.float32
BF16 = jnp.bfloat16
I32 = jnp.int32

D_MODEL = 1024
NH_A = 4
DK_A = 256
NKV = 4
REP = 4
HD = 64
BLOCK = 64
TOPK = 16
WINDOW = 512
N_BUCKETS = 32
REL_MAX_DIST = 2048
PAGE = 128
EPS = 1e-6
NEG = -1e30
LOG2E = math.log2(math.e)

LANES = 128
VMEM_LIMIT = 56 * 1024 * 1024
TQ = 128
TQ_SAMPLE = 32
KT = 256
SUPER_BLOCKS = 16
SUPER = SUPER_BLOCKS * BLOCK
SUPERS_PER_STEP = 4
VT_ROWS = HD + 16
CONST_LANE = HD + SUPER_BLOCKS
N_BIAS_TILES = 14
CMP_BLOCKS_PER_STEP = 64
CMP_BIAS_ROWS = 40
CMP_ROW_CLASSES = (64, 128, 192)


def _bucket_thresholds():
    exact = N_BUCKETS // 2
    d = np.arange(1, 4 * REL_MAX_DIST, dtype=np.float64)
    big = exact + np.floor(np.log(d / exact) / math.log(REL_MAX_DIST / exact) * (N_BUCKETS - exact)).astype(np.int64)
    b = np.where(d < exact, d.astype(np.int64), np.minimum(big, N_BUCKETS - 1))
    return [int(d[np.argmax(b >= k)]) for k in range(1, N_BUCKETS)]


BUCKET_THR = _bucket_thresholds()
assert 128 * (N_BIAS_TILES - 1) - (TQ - 1) >= BUCKET_THR[-1]


def _cparams(*sem):
    return pltpu.CompilerParams(dimension_semantics=sem, vmem_limit_bytes=VMEM_LIMIT)


def _dot(a, b):
    return jnp.dot(a.astype(BF16), b.astype(BF16), preferred_element_type=F32)


def _dot_nt(a, b):
    return lax.dot_general(a.astype(BF16), b.astype(BF16), (((1,), (1,)), ((), ())),
                           preferred_element_type=F32)


def _split3(x):
    hi = x.astype(BF16)
    r1 = x - hi.astype(F32)
    mid = r1.astype(BF16)
    lo = (r1 - mid.astype(F32)).astype(BF16)
    return hi, mid, lo


def _sigmoid(x):
    return 1.0 / (1.0 + jnp.exp(-x))


def _norm_mod(x, gain, shift, scale):
    ms = jnp.mean(x * x, axis=-1, keepdims=True)
    y = x * lax.rsqrt(ms + EPS) * gain
    return y * (1.0 + scale) + shift


def _seg_mean_sq(y, bd):
    parts = []
    for j in range(y.shape[1] // 256):
        sq = y[:, j * 256:(j + 1) * 256]
        sq = sq * sq
        hi = sq.astype(BF16)
        lo = (sq - hi.astype(F32)).astype(BF16)
        parts.append(jnp.dot(hi, bd, preferred_element_type=F32) + jnp.dot(lo, bd, preferred_element_type=F32))
    return parts[0] if len(parts) == 1 else jnp.concatenate(parts, axis=1)


def _row_spec(tm, n, per_row):
    if per_row:
        return pl.BlockSpec((tm, n), lambda i: (i, 0))
    return pl.BlockSpec((1, n), lambda i: (0, 0))


def _const_spec(shape):
    return pl.BlockSpec(shape, lambda i: tuple(0 for _ in shape))


def _ada_kernel(c_ref, w_ref, b_ref, o_ref):
    c = c_ref[...]
    o_ref[...] = _dot(c * _sigmoid(c), w_ref[...]) + b_ref[...]


def _ada(c, w, b):
    m, k = c.shape
    n = w.shape[1]
    tn = 1024
    return pl.pallas_call(
        _ada_kernel,
        grid=(n // tn,),
        in_specs=[pl.BlockSpec((m, k), lambda j: (0, 0)),
                  pl.BlockSpec((k, tn), lambda j: (0, j)),
                  pl.BlockSpec((1, tn), lambda j: (0, j))],
        out_specs=pl.BlockSpec((m, tn), lambda j: (0, j)),
        out_shape=jax.ShapeDtypeStruct((m, n), F32),
        compiler_params=_cparams("arbitrary"),
        name="ada",
    )(c, w, b.reshape(1, n))


def _proj_plain_kernel(x_ref, gain_ref, sh_ref, sc_ref, w_ref, o_ref):
    xn = _norm_mod(x_ref[...], gain_ref[...], sh_ref[...], sc_ref[...])
    o_ref[...] = jnp.dot(xn.astype(BF16), w_ref[...], preferred_element_type=F32)


def _proj_plain(x, gain, shift, scale, w_bf, tm):
    m, d = x.shape
    n = w_bf.shape[1]
    per_row = shift.shape[0] != 1
    return pl.pallas_call(
        _proj_plain_kernel,
        grid=(m // tm,),
        in_specs=[pl.BlockSpec((tm, d), lambda i: (i, 0)),
                  _const_spec((1, d)),
                  _row_spec(tm, d, per_row), _row_spec(tm, d, per_row),
                  _const_spec((d, n))],
        out_specs=pl.BlockSpec((tm, n), lambda i: (i, 0)),
        out_shape=jax.ShapeDtypeStruct((m, n), F32),
        compiler_params=_cparams("arbitrary"),
        name="proj_mlstm",
    )(x, gain, shift, scale, w_bf)


def _proj_q_kernel(x_ref, gain_ref, sh_ref, sc_ref, w_ref, bd_ref, qg_ref, bg_ref, q_ref, gt_ref):
    xn = _norm_mod(x_ref[...], gain_ref[...], sh_ref[...], sc_ref[...])
    y = jnp.dot(xn.astype(BF16), w_ref[...], preferred_element_type=F32)
    nq = q_ref.shape[1]
    yq = y[:, :nq]
    ms = _seg_mean_sq(yq, bd_ref[...])
    q_ref[...] = yq * lax.rsqrt(ms + EPS) * qg_ref[...]
    gates = _sigmoid(y[:, nq:] + bg_ref[...])
    gt_ref[...] = gates.T


def _proj_q(x, gain, shift, scale, w_bf, bd, qgain_row, bgate_row, tm):
    m, d = x.shape
    n = w_bf.shape[1]
    nq = n - LANES
    per_row = shift.shape[0] != 1
    return pl.pallas_call(
        _proj_q_kernel,
        grid=(m // tm,),
        in_specs=[pl.BlockSpec((tm, d), lambda i: (i, 0)),
                  _const_spec((1, d)),
                  _row_spec(tm, d, per_row), _row_spec(tm, d, per_row),
                  _const_spec((d, n)), _const_spec((256, 256)),
                  _const_spec((1, nq)), _const_spec((1, LANES))],
        out_specs=[pl.BlockSpec((tm, nq), lambda i: (i, 0)),
                   pl.BlockSpec((LANES, tm), lambda i: (0, i))],
        out_shape=[jax.ShapeDtypeStruct((m, nq), F32),
                   jax.ShapeDtypeStruct((LANES, m), F32)],
        compiler_params=_cparams("arbitrary"),
        name="proj_q",
    )(x, gain, shift, scale, w_bf, bd, qgain_row, bgate_row)


def _proj_kv_kernel(x_ref, gain_ref, sh_ref, sc_ref, w_ref, bd_ref, kg_ref, km_ref,
                    cmp_ref, sel_ref, win_ref, cmpt_ref, selt_ref, wint_ref,
                    selb_ref, seltb_ref, winb_ref, wintb_ref):
    xn = _norm_mod(x_ref[...], gain_ref[...], sh_ref[...], sc_ref[...])
    y = jnp.dot(xn.astype(BF16), w_ref[...], preferred_element_type=F32)
    w = cmp_ref.shape[1]
    cmp = y[:, :w]
    ykn = y[:, w:]
    ms = _seg_mean_sq(ykn, bd_ref[...])
    ykn = jnp.where(km_ref[...] > 0.5, ykn * lax.rsqrt(ms + EPS) * kg_ref[...], ykn)
    sel = ykn[:, :w]
    win = ykn[:, w:]
    sel_t = sel.T
    win_t = win.T
    cmp_ref[...] = cmp
    sel_ref[...] = sel
    win_ref[...] = win
    cmpt_ref[...] = cmp.T
    selt_ref[...] = sel_t
    wint_ref[...] = win_t
    tm = sel.shape[0]
    tok = pl.program_id(0) * tm + lax.broadcasted_iota(I32, sel.shape, 0)
    lane = lax.broadcasted_iota(I32, sel.shape, 1) % LANES
    onehot = jnp.where((lane - HD == (tok // BLOCK) % SUPER_BLOCKS) | (lane == CONST_LANE)
                       | (lane == CONST_LANE + 1), 1.0, 0.0)
    selb_ref[...] = jnp.where(lane < HD, sel, onehot).astype(BF16)
    winb_ref[...] = win.astype(BF16)

    def value_rows(x_t):
        ones_blk = jnp.where(lax.broadcasted_iota(I32, (HD, tm), 0) == 0, 1.0, 0.0)
        return jnp.concatenate(
            [piece for gg in range(NKV) for piece in (x_t[gg * LANES + HD:(gg + 1) * LANES], ones_blk)],
            axis=0).astype(BF16)

    seltb_ref[...] = value_rows(sel_t)
    wintb_ref[...] = value_rows(win_t)


def _proj_kv(x, gain, shift, scale, w_bf, bd, kgain_row, kmask_row, tm):
    m, d = x.shape
    n = w_bf.shape[1]
    w = n // 3
    per_row = shift.shape[0] != 1
    row = pl.BlockSpec((tm, w), lambda i: (i, 0))
    col = pl.BlockSpec((w, tm), lambda i: (0, i))
    return pl.pallas_call(
        _proj_kv_kernel,
        grid=(m // tm,),
        in_specs=[pl.BlockSpec((tm, d), lambda i: (i, 0)),
                  _const_spec((1, d)),
                  _row_spec(tm, d, per_row), _row_spec(tm, d, per_row),
                  _const_spec((d, n)), _const_spec((256, 256)),
                  _const_spec((1, 2 * w)), _const_spec((1, 2 * w))],
        out_specs=[row, row, row, col, col, col, row, col, row, col],
        out_shape=[jax.ShapeDtypeStruct((m, w), F32)] * 3 + [jax.ShapeDtypeStruct((w, m), F32)] * 3
                  + [jax.ShapeDtypeStruct((m, w), BF16), jax.ShapeDtypeStruct((w, m), BF16),
                     jax.ShapeDtypeStruct((m, w), BF16), jax.ShapeDtypeStruct((w, m), BF16)],
        compiler_params=_cparams("arbitrary"),
        name="proj_kv",
    )(x, gain, shift, scale, w_bf, bd, kgain_row, kmask_row)


def _out_proj_kernel(x_ref, g_ref, y_ref, w_ref, o_ref):
    o_ref[...] = x_ref[...] + g_ref[...] * jnp.dot(y_ref[...].astype(BF16), w_ref[...],
                                                   preferred_element_type=F32)


def _out_proj(x, g, y, w_bf, tm):
    m, d = x.shape
    k = y.shape[1]
    per_row = g.shape[0] != 1
    return pl.pallas_call(
        _out_proj_kernel,
        grid=(m // tm,),
        in_specs=[pl.BlockSpec((tm, d), lambda i: (i, 0)),
                  _row_spec(tm, d, per_row),
                  pl.BlockSpec((tm, k), lambda i: (i, 0)),
                  _const_spec((k, d))],
        out_specs=pl.BlockSpec((tm, d), lambda i: (i, 0)),
        out_shape=jax.ShapeDtypeStruct((m, d), F32),
        compiler_params=_cparams("arbitrary"),
        name="out_proj",
    )(x, g, y, w_bf)


def _ffn_kernel(x_ref, gain_ref, sh_ref, sc_ref, g_ref, w1_ref, w2_ref, o_ref, *, fc):
    x = x_ref[...]
    xn = _norm_mod(x, gain_ref[...], sh_ref[...], sc_ref[...]).astype(BF16)
    acc = jnp.zeros(x.shape, F32)
    for c in range(w1_ref.shape[1] // fc):
        h = jnp.dot(xn, w1_ref[:, c * fc:(c + 1) * fc], preferred_element_type=F32)
        h = jnp.maximum(h, 0.0)
        acc = acc + jnp.dot((h * h).astype(BF16), w2_ref[c * fc:(c + 1) * fc, :],
                            preferred_element_type=F32)
    o_ref[...] = x + g_ref[...] * acc


def _ffn(x, gain, shift, scale, g, w1_bf, w2_bf, tm):
    m, d = x.shape
    f = w1_bf.shape[1]
    per_row = shift.shape[0] != 1
    return pl.pallas_call(
        functools.partial(_ffn_kernel, fc=1024),
        grid=(m // tm,),
        in_specs=[pl.BlockSpec((tm, d), lambda i: (i, 0)),
                  _const_spec((1, d)),
                  _row_spec(tm, d, per_row), _row_spec(tm, d, per_row), _row_spec(tm, d, per_row),
                  _const_spec((d, f)), _const_spec((f, d))],
        out_specs=pl.BlockSpec((tm, d), lambda i: (i, 0)),
        out_shape=jax.ShapeDtypeStruct((m, d), F32),
        compiler_params=_cparams("arbitrary"),
        name="ffn",
    )(x, gain, shift, scale, g, w1_bf, w2_bf)


def _mlstm_kernel(q_ref, k_ref, v_ref, o_ref, gt_ref, bif_ref, hg_ref, c0_ref, n0_ref, m0_ref,
                  y_ref, cout_ref, nout_ref, mout_ref, c_scr, n_scr, m_scr, *, rows, valid, chunk, zero_init):
    ci = pl.program_id(1)
    nci = pl.num_programs(1)

    @pl.when(ci == 0)
    def _():
        if zero_init:
            c_scr[...] = jnp.zeros(c_scr.shape, F32)
            n_scr[...] = jnp.zeros(n_scr.shape, F32)
            m_scr[...] = jnp.zeros(m_scr.shape, F32)
        else:
            c_scr[...] = c0_ref[0]
            n_scr[...] = n0_ref[0]
            m_scr[...] = m0_ref[0]

    L = chunk

    def padded(ref_val, fill):
        if rows == L:
            return ref_val
        pad = jnp.full((L - rows, ref_val.shape[1]), fill, ref_val.dtype)
        return jnp.concatenate([ref_val, pad], axis=0)

    q = padded(q_ref[0], 0.0)
    k = padded(k_ref[0], 0.0)
    v = padded(v_ref[0], 0.0)
    og = padded(o_ref[0], 0.0)
    g = padded(gt_ref[0], 0.0) + bif_ref[...]
    lf = -(jnp.maximum(-g, 0.0) + jnp.log1p(jnp.exp(-jnp.abs(g))))
    li = g
    if valid != L:
        is_real = lax.broadcasted_iota(I32, (L, LANES), 0) < valid
        lf = jnp.where(is_real, lf, 0.0)
        li = jnp.where(is_real, li, NEG)
    r_io = lax.broadcasted_iota(I32, (L, L), 0)
    c_io = lax.broadcasted_iota(I32, (L, L), 1)
    causal = c_io <= r_io
    tril = jnp.where(causal, 1.0, 0.0).astype(BF16)
    hi, mid, lo = _split3(lf)
    b = (jnp.dot(tril, hi, preferred_element_type=F32) + jnp.dot(tril, mid, preferred_element_type=F32)
         + jnp.dot(tril, lo, preferred_element_type=F32))
    lane = lax.broadcasted_iota(I32, (L, LANES), 1)
    mixed_t = jnp.where(lane < NH_A, li, b).T
    for h in range(NH_A):
        sl = slice(h * DK_A, (h + 1) * DK_A)
        qh = q[:, sl] * (DK_A ** -0.5)
        kh = k[:, sl]
        vh = v[:, sl]
        b_col = b[:, NH_A + h:NH_A + h + 1]
        li_col = li[:, h:h + 1]
        b_row = mixed_t[NH_A + h:NH_A + h + 1, :]
        li_row = mixed_t[h:h + 1, :]
        m_prev = m_scr[h:h + 1, 0:1]
        c_prev = c_scr[h]
        n_prev = n_scr[h]
        dmat = jnp.where(causal, b_col - b_row + li_row, NEG)
        m_inter = b_col + m_prev
        m_t = jnp.maximum(m_inter, jnp.max(dmat, axis=-1, keepdims=True))
        s = _dot_nt(qh, kh) * jnp.exp(dmat - m_t)
        a_inter = jnp.exp(m_inter - m_t)
        num = _dot(s, vh) + a_inter * _dot(qh, c_prev)
        den = jnp.sum(s, axis=-1, keepdims=True) + a_inter * jnp.sum(qh * n_prev, axis=-1, keepdims=True)
        hh = num / jnp.maximum(jnp.abs(den), jnp.exp(-m_t))
        b_last = b_col[L - 1:L, :]
        g_col = b_last - b_col + li_col
        m_new = jnp.maximum(b_last + m_prev, jnp.max(g_col, axis=0, keepdims=True))
        w_col = jnp.exp(g_col - m_new)
        decay = jnp.exp(b_last + m_prev - m_new)
        kw = kh * w_col
        c_scr[h] = decay * c_prev + _dot(kw.T, vh)
        n_scr[h] = decay * n_prev + jnp.sum(kw, axis=0, keepdims=True)
        m_scr[h:h + 1, :] = jnp.broadcast_to(m_new, (1, LANES))
        hn = hh * lax.rsqrt(jnp.mean(hh * hh, axis=-1, keepdims=True) + EPS) * hg_ref[h:h + 1, :]
        yh = hn * _sigmoid(og[:, sl])
        y_ref[0, :, sl] = yh[:rows]

    @pl.when(ci == nci - 1)
    def _():
        cout_ref[0] = c_scr[...]
        nout_ref[0] = n_scr[...]
        mout_ref[0] = m_scr[...]


def _mlstm(proj, b_if_row, head_gain, c0, n0, m0, *, rows, valid, chunk, zero_init):
    batch, t, _ = proj.shape
    nch = t // rows
    inner = NH_A * DK_A

    def colblk(j, width):
        return pl.BlockSpec((1, rows, width), lambda b, c, j=j: (b, c, j))

    state_c = pl.BlockSpec((1, NH_A, DK_A, DK_A), lambda b, c: (b, 0, 0, 0))
    state_n = pl.BlockSpec((1, NH_A, 1, DK_A), lambda b, c: (b, 0, 0, 0))
    state_m = pl.BlockSpec((1, 8, LANES), lambda b, c: (b, 0, 0))
    return pl.pallas_call(
        functools.partial(_mlstm_kernel, rows=rows, valid=valid, chunk=chunk, zero_init=zero_init),
        grid=(batch, nch),
        in_specs=[colblk(0, inner), colblk(1, inner), colblk(2, inner), colblk(3, inner),
                  colblk(4 * inner // LANES, LANES),
                  pl.BlockSpec((1, LANES), lambda b, c: (0, 0)),
                  pl.BlockSpec((NH_A, DK_A), lambda b, c: (0, 0)),
                  state_c, state_n, state_m],
        out_specs=[pl.BlockSpec((1, rows, inner), lambda b, c: (b, c, 0)), state_c, state_n, state_m],
        out_shape=[jax.ShapeDtypeStruct((batch, t, inner), F32),
                   jax.ShapeDtypeStruct((batch, NH_A, DK_A, DK_A), F32),
                   jax.ShapeDtypeStruct((batch, NH_A, 1, DK_A), F32),
                   jax.ShapeDtypeStruct((batch, 8, LANES), F32)],
        scratch_shapes=[pltpu.VMEM((NH_A, DK_A, DK_A), F32), pltpu.VMEM((NH_A, 1, DK_A), F32),
                        pltpu.VMEM((8, LANES), F32)],
        compiler_params=_cparams("arbitrary", "arbitrary"),
        name="mlstm",
    )(proj, proj, proj, proj, proj, b_if_row, head_gain, c0, n0, m0)


def _compress_body(r_refs, pos_ref, w1_ref, w2_ref, kg_ref, o_ref, nb):
    acc = None
    for lp in range(BLOCK // 2):
        pieces = []
        for g in range(NKV):
            xa = r_refs[g][pl.ds(2 * lp, nb, stride=BLOCK), :]
            xb = r_refs[g][pl.ds(2 * lp + 1, nb, stride=BLOCK), :]
            pieces.append(jnp.concatenate([xa, xb], axis=1))
        pa = pos_ref[2 * lp:2 * lp + 1, 0:LANES]
        pb = pos_ref[2 * lp + 1:2 * lp + 2, 0:LANES]
        pieces.append(jnp.broadcast_to(jnp.concatenate([pa, pb], axis=1), (8, 2 * LANES)))
        x = jnp.concatenate(pieces, axis=0).astype(BF16)
        d = jnp.dot(x, w1_ref[lp], preferred_element_type=F32)
        acc = d if acc is None else acc + d
    pre = acc[:NKV * nb] + acc[NKV * nb:NKV * nb + 1]
    hid = pre * _sigmoid(pre)
    out = jnp.dot(hid.astype(BF16), w2_ref[...], preferred_element_type=F32)
    is_k = lax.broadcasted_iota(I32, out.shape, 1) < HD
    ms = jnp.sum(jnp.where(is_k, out * out, 0.0), axis=-1, keepdims=True) * (1.0 / HD)
    out = jnp.where(is_k, out * lax.rsqrt(ms + EPS) * kg_ref[...], out)
    for g in range(NKV):
        o_ref[:, g * LANES:(g + 1) * LANES] = out[g * nb:(g + 1) * nb]


def _compress_kernel(r0, r1, r2, r3, pos_ref, w1_ref, w2_ref, kg_ref, o_ref, *, nb):
    _compress_body((r0, r1, r2, r3), pos_ref, w1_ref, w2_ref, kg_ref, o_ref, nb)


def _compress_paged_kernel(pt_ref, *refs, nb):
    npages = nb * BLOCK // PAGE
    pages = refs[:npages]
    pos_ref, w1_ref, w2_ref, kg_ref, o_ref = refs[npages:npages + 5]
    r_scrs = refs[npages + 5:]
    for p in range(npages):
        for g in range(NKV):
            r_scrs[g][p * PAGE:(p + 1) * PAGE, :] = pages[p][0, g * LANES:(g + 1) * LANES, :].T
    _compress_body(r_scrs, pos_ref, w1_ref, w2_ref, kg_ref, o_ref, nb)


def _compress_paged(cache_t, page_table, pos_rows, w1p, w2p, kgain_row, nb):
    batch, npg = page_table.shape
    w = cache_t.shape[1]
    npages = nb * BLOCK // PAGE
    steps = npg // npages

    def page_spec(p):
        return pl.BlockSpec((1, w, PAGE), lambda b, j, pt, p=p: (pt[b, j * npages + p], 0, 0))

    def const(shape):
        return pl.BlockSpec(shape, lambda b, j, pt: tuple(0 for _ in shape))

    return pl.pallas_call(
        functools.partial(_compress_paged_kernel, nb=nb),
        grid_spec=pltpu.PrefetchScalarGridSpec(
            num_scalar_prefetch=1,
            grid=(batch, steps),
            in_specs=[page_spec(p) for p in range(npages)]
                     + [const((BLOCK, w)), const(w1p.shape), const(w2p.shape), const((1, LANES))],
            out_specs=pl.BlockSpec((nb, w), lambda b, j, pt: (b * steps + j, 0)),
            scratch_shapes=[pltpu.VMEM((nb * BLOCK, LANES), F32) for _ in range(NKV)]),
        out_shape=jax.ShapeDtypeStruct((batch * npg * PAGE // BLOCK, w), F32),
        compiler_params=_cparams("arbitrary", "arbitrary"),
        name="compress_paged",
    )(page_table, *([cache_t] * npages), pos_rows, w1p, w2p, kgain_row)


def _compress(rows, pos_rows, w1p, w2p, kgain_row, nb):
    t, w = rows.shape
    nblk = t // BLOCK
    return pl.pallas_call(
        functools.partial(_compress_kernel, nb=nb),
        grid=(nblk // nb,),
        in_specs=[pl.BlockSpec((nb * BLOCK, LANES), lambda i, g=g: (i, g)) for g in range(NKV)]
                 + [_const_spec((BLOCK, w)), _const_spec(w1p.shape), _const_spec(w2p.shape),
                    _const_spec((1, LANES))],
        out_specs=pl.BlockSpec((nb, w), lambda i: (i, 0)),
        out_shape=jax.ShapeDtypeStruct((nblk, w), F32),
        compiler_params=_cparams("arbitrary"),
        name="compress",
    )(rows, rows, rows, rows, pos_rows, w1p, w2p, kgain_row)


def _bias_from_dist(dist, tab_ref, h):
    out = jnp.full(dist.shape, tab_ref[0, h], F32)
    for kk in range(1, N_BUCKETS):
        out = jnp.where(dist >= BUCKET_THR[kk - 1], tab_ref[kk, h], out)
    return out


def _bias_tiles_kernel(tab_ref, kq_ref, kqd_ref, qk_ref):
    a = pl.program_id(0)
    row = lax.broadcasted_iota(I32, (LANES, TQ), 0)
    col = lax.broadcasted_iota(I32, (LANES, TQ), 1)

    def head(h, _):
        kq = _bias_from_dist(a * LANES + col - row, tab_ref, h)
        kq_ref[0, h] = kq * LOG2E
        kqd_ref[0, h] = (kq - tab_ref[N_BUCKETS - 1, h]) * LOG2E
        qk_ref[0, h] = _bias_from_dist(a * LANES + row - col, tab_ref, h) * LOG2E
        return 0

    lax.fori_loop(0, kq_ref.shape[1], head, 0)


def _bias_tiles(rel_bias):
    nh = rel_bias.shape[1]
    spec = pl.BlockSpec((1, nh, LANES, TQ), lambda a: (a, 0, 0, 0))
    shape = jax.ShapeDtypeStruct((N_BIAS_TILES, nh, LANES, TQ), F32)
    return pl.pallas_call(
        _bias_tiles_kernel,
        grid=(N_BIAS_TILES,),
        in_specs=[pl.BlockSpec(memory_space=pltpu.SMEM)],
        out_specs=[spec, spec, spec],
        out_shape=[shape, shape, shape],
        compiler_params=_cparams("arbitrary"),
        name="bias_tiles",
    )(rel_bias)


def _stack_heads(q):
    z = jnp.zeros((q.shape[0], HD), q.dtype)
    return jnp.concatenate([jnp.concatenate([q[:, r * HD:(r + 1) * HD], z], axis=1) for r in range(REP)],
                           axis=0)


def _tile_heads(x):
    return jnp.concatenate([x] * REP, axis=1)


def _pad_rows(x, n):
    if x.shape[0] == n:
        return x
    return jnp.concatenate([x, jnp.zeros((n - x.shape[0],) + x.shape[1:], x.dtype)], axis=0)


def _bias4(dist, tab_ref, g):
    biases = [jnp.full(dist.shape, tab_ref[0, g * REP + r], F32) for r in range(REP)]
    for kk in range(1, N_BUCKETS):
        reached = dist >= BUCKET_THR[kk - 1]
        biases = [jnp.where(reached, tab_ref[kk, g * REP + r], biases[r]) for r in range(REP)]
    return biases


def _cmp_branch(qc, kcvc_ref, tab_ref, g, pos, pos0, x_scr, nb):
    tq = qc.shape[0]
    kcvc = kcvc_ref[0:nb, :]
    q4 = _stack_heads(qc)
    lt = _dot_nt(kcvc, q4)
    n_io = lax.broadcasted_iota(I32, (nb, tq), 0)
    dist = pos - (n_io * BLOCK + (BLOCK - 1))
    vis = dist >= 0
    if nb <= CMP_BIAS_ROWS:
        biased = [lt[:, r * tq:(r + 1) * tq] + b for r, b in enumerate(_bias4(dist, tab_ref, g))]
    else:
        far = [tab_ref[N_BUCKETS - 1, g * REP + r] for r in range(REP)]
        x_scr[0:nb, :] = jnp.concatenate([lt[:, r * tq:(r + 1) * tq] + far[r] for r in range(REP)], axis=1)
        first = (pos0 - (BUCKET_THR[-1] + BLOCK - 1)) // BLOCK + 1
        w0 = pl.multiple_of(jnp.clip(first // 8 * 8, 0, nb - CMP_BIAS_ROWS), 8)
        w_io = lax.broadcasted_iota(I32, (CMP_BIAS_ROWS, tq), 0) + w0
        near = _bias4(pos - (w_io * BLOCK + (BLOCK - 1)), tab_ref, g)
        lt_w = _dot_nt(kcvc_ref[pl.ds(w0, CMP_BIAS_ROWS), :], q4)
        x_scr[pl.ds(w0, CMP_BIAS_ROWS), :] = jnp.concatenate(
            [lt_w[:, r * tq:(r + 1) * tq] + near[r] for r in range(REP)], axis=1)
        biased = [x_scr[0:nb, r * tq:(r + 1) * tq] for r in range(REP)]
    probs = []
    for r in range(REP):
        x = jnp.where(vis, biased[r], NEG)
        e = jnp.exp(x - jnp.max(x, axis=0, keepdims=True))
        p = e * (1.0 / jnp.sum(e, axis=0, keepdims=True))
        probs.append(jnp.where(vis, p, 0.0))
    score = probs[0] + probs[1] + probs[2] + probs[3]
    oc_t = _dot(kcvc.T[HD:, :], jnp.concatenate(probs, axis=1))
    cand = n_io < pos // BLOCK
    return oc_t, jnp.where(cand, score, -jnp.inf)


def _topk_rows(score, n_top, pick_fn=None):
    nb, tq = score.shape
    n_f = lax.broadcasted_iota(I32, (nb, tq), 0).astype(F32)
    s = score
    for it in range(n_top):
        mx = jnp.max(s, axis=0, keepdims=True)
        idx = jnp.min(jnp.where(s == mx, n_f, float(nb)), axis=0, keepdims=True)
        if pick_fn is not None:
            pick_fn(it, idx, mx > -jnp.inf)
        s = jnp.where(n_f == idx, -jnp.inf, s)
    return (s == -jnp.inf) & (score > -jnp.inf)


def _flash_init(nl):
    return jnp.full((1, nl), NEG, F32), jnp.zeros((VT_ROWS, nl), F32)


def _online_step(carry, tiles):
    m_c, acc_c = carry
    m_new = m_c
    for x2, _ in tiles:
        m_new = jnp.maximum(m_new, jnp.max(x2, axis=0, keepdims=True))
    acc = jnp.exp2(m_c - m_new) * acc_c
    for x2, vt in tiles:
        acc = acc + jnp.dot(vt, jnp.exp2(x2 - m_new).astype(BF16), preferred_element_type=F32)
    return m_new, acc


def _merge_parts(a, b):
    m = jnp.maximum(a[0], b[0])
    return m, jnp.exp2(a[0] - m) * a[1] + jnp.exp2(b[0] - m) * b[1]


def _normalise(acc):
    return acc[:HD] / acc[HD:HD + 1]


def _untranspose_heads(o_t, tq):
    halves = []
    for p in range(REP // 2):
        pair = jnp.concatenate([o_t[:, (2 * p) * tq:(2 * p + 1) * tq],
                                o_t[:, (2 * p + 1) * tq:(2 * p + 2) * tq]], axis=0)
        halves.append(pair.T)
    return jnp.concatenate(halves, axis=1)


def _gate_row(gt_ref, br, g):
    return jnp.concatenate([gt_ref[pl.ds(br * NKV * REP + g * REP + r, 1), :] for r in range(REP)], axis=1)


def _attn_prompt_kernel(tab_ref, qc_ref, qs_ref, qw_ref, gt_ref, kcvc_ref, ksel_ref, vselt_ref,
                        kw0, kw1, kw2, kw3, kw4, vw0, vw1, vw2, vw3, vw4, bt_ref, btd_ref, o_ref,
                        sel_scr, xc_scr, oct_scr):
    g = pl.program_id(0)
    i = pl.program_id(1)
    tq = TQ
    nl = REP * tq
    t0 = i * tq
    pos = t0 + lax.broadcasted_iota(I32, (1, tq), 1)

    nb = kcvc_ref.shape[0]
    sel_scr[...] = jnp.full((nb, tq), NEG, F32)
    classes = [rows for rows in CMP_ROW_CLASSES if rows < nb] + [nb]
    for k, rows in enumerate(classes):
        lo = classes[k - 1] if k else 0

        @pl.when((2 * i + 2 > lo) & ((2 * i + 2 <= rows) | (rows == nb)))
        def _(rows=rows):
            oc, score = _cmp_branch(qc_ref[...], kcvc_ref, tab_ref, g, pos, t0, xc_scr, rows)
            oct_scr[...] = oc
            sel_scr[0:rows, :] = jnp.where(_topk_rows(score, TOPK - 1), 0.0, NEG)

    oc_t = oct_scr[...]

    def bias_tile(a):
        return jnp.concatenate([bt_ref[a, r] for r in range(REP)], axis=1)

    qs_t = (_stack_heads(qs_ref[...]) * LOG2E).T[:HD].astype(BF16)
    far_bias = jnp.concatenate([jnp.full((1, tq), tab_ref[N_BUCKETS - 1, g * REP + r] * LOG2E, F32)
                                for r in range(REP)], axis=1)
    far_hi = far_bias.astype(BF16).astype(F32)
    row16 = lax.broadcasted_iota(I32, (SUPER_BLOCKS, nl), 0)
    const_rows = jnp.where(row16 == 0, far_hi, jnp.where(row16 == 1, far_bias - far_hi, 0.0)).astype(BF16)
    pad_rows = jnp.zeros((LANES - CONST_LANE - SUPER_BLOCKS, nl), BF16)

    def supers_step(supers, near, carry):
        tiles = []
        for jj in supers:
            mrows = _tile_heads(sel_scr[pl.ds(pl.multiple_of(jj * SUPER_BLOCKS, SUPER_BLOCKS), SUPER_BLOCKS), :])
            q_aug = jnp.concatenate([qs_t, mrows.astype(BF16), const_rows, pad_rows], axis=0)
            for c in range(SUPER // KT):
                s0 = pl.multiple_of(jj * SUPER + c * KT, KT)
                x = jnp.dot(ksel_ref[pl.ds(s0, KT), :], q_aug, preferred_element_type=F32)
                if near:
                    tile0 = jj * (SUPER // LANES) + c * (KT // LANES)
                    x = x + jnp.concatenate([bias_delta(jnp.clip(i - tile0 - a, 0, N_BIAS_TILES - 1))
                                             for a in range(KT // LANES)], axis=0)
                tiles.append((x, jnp.max(x, axis=0, keepdims=True), s0))
        parts = [(m, jnp.dot(vselt_ref[0:VT_ROWS, pl.ds(s0, KT)], jnp.exp2(x - m).astype(BF16),
                             preferred_element_type=F32)) for x, m, s0 in tiles]
        for part in parts:
            carry = _merge_parts(carry, part)
        return carry

    def make_body(near):
        return lambda step, carry: supers_step(
            [step * SUPERS_PER_STEP + u for u in range(SUPERS_PER_STEP)], near, carry)

    def bias_delta(a):
        return jnp.concatenate([btd_ref[a, r] for r in range(REP)], axis=1)

    keys_per_step = SUPER * SUPERS_PER_STEP
    n_super = (2 * i) // SUPER_BLOCKS + 1
    n_steps = n_super // SUPERS_PER_STEP
    n_far = jnp.minimum(jnp.maximum(t0 - (N_BIAS_TILES - 1) * LANES + LANES, 0) // keys_per_step, n_steps)
    carry = lax.fori_loop(0, n_far, make_body(False), _flash_init(nl))
    carry = lax.fori_loop(n_far, n_steps, make_body(True), carry)
    rem = n_super % SUPERS_PER_STEP
    base = n_steps * SUPERS_PER_STEP
    carry = lax.cond((rem & 2) != 0, lambda c: supers_step([base, base + 1], True, c), lambda c: c, carry)
    carry = lax.cond((rem & 1) != 0, lambda c: supers_step([n_super - 1], True, c), lambda c: c, carry)

    sj = lax.broadcasted_iota(I32, (LANES, nl), 0)
    ti4 = _tile_heads(lax.broadcasted_iota(I32, (1, tq), 1))
    t0a = pl.multiple_of(t0, LANES)
    q_diag = jnp.concatenate([qs_t, jnp.zeros((SUPER_BLOCKS, nl), BF16), const_rows, pad_rows], axis=0)
    qw_bf = (_stack_heads(qw_ref[...]) * LOG2E).astype(BF16)
    kw_refs = (kw0, kw1, kw2, kw3, kw4)
    vw_refs = (vw0, vw1, vw2, vw3, vw4)
    x_diag = jnp.dot(ksel_ref[pl.ds(t0a, LANES), :], q_diag, preferred_element_type=F32)
    x_win = [_dot_nt(kw_refs[j][...], qw_bf) for j in range(5)]
    x = jnp.where((sj // BLOCK == ti4 // BLOCK) & (sj <= ti4), x_diag + bias_delta(0), NEG)
    _, accs = _online_step(carry, [(x, vselt_ref[0:VT_ROWS, pl.ds(t0a, LANES)])])

    tiles = []
    for j in range(5):
        dist = LANES * (4 - j) + ti4 - sj
        ok = jnp.where((dist >= 0) & (dist <= WINDOW), 1.0, 0.0) * jnp.where(i - 4 + j >= 0, 1.0, 0.0)
        tiles.append((jnp.where(ok > 0.5, x_win[j] + bias_tile(4 - j), NEG), vw_refs[j][0:VT_ROWS, :]))
    _, accw = _online_step(_flash_init(nl), tiles)

    out_t = (_gate_row(gt_ref, 0, g) * oc_t + _gate_row(gt_ref, 1, g) * _normalise(accs)
             + _gate_row(gt_ref, 2, g) * _normalise(accw))
    o_ref[...] = _untranspose_heads(out_t, tq)


def _attn_prompt(rel_bias, qs, gates_t, kcvc, sel_bf, selt_bf, win_bf, wint_bf, bias_tiles, bias_delta_tiles):
    t = qs.shape[0]
    nb = kcvc.shape[0]
    width = NKV * REP * HD

    def qspec(br):
        return pl.BlockSpec((TQ, REP * HD), lambda g, i, br=br: (i, br * NKV + g))

    def kw_spec(j):
        return pl.BlockSpec((LANES, LANES), lambda g, i, j=j: (jnp.maximum(i - 4 + j, 0), g))

    def vw_spec(j):
        return pl.BlockSpec((LANES, LANES), lambda g, i, j=j: (g, jnp.maximum(i - 4 + j, 0)))

    return pl.pallas_call(
        _attn_prompt_kernel,
        grid=(NKV, t // TQ),
        in_specs=[pl.BlockSpec(memory_space=pltpu.SMEM),
                  qspec(0), qspec(1), qspec(2),
                  pl.BlockSpec((LANES, TQ), lambda g, i: (0, i)),
                  pl.BlockSpec((nb, LANES), lambda g, i: (0, g)),
                  pl.BlockSpec((t, LANES), lambda g, i: (0, g)),
                  pl.BlockSpec((LANES, t), lambda g, i: (g, 0))]
                 + [kw_spec(j) for j in range(5)] + [vw_spec(j) for j in range(5)]
                 + [pl.BlockSpec((N_BIAS_TILES, REP, LANES, TQ), lambda g, i: (0, g, 0, 0))] * 2,
        out_specs=pl.BlockSpec((TQ, REP * HD), lambda g, i: (i, g)),
        out_shape=jax.ShapeDtypeStruct((t, width), F32),
        scratch_shapes=[pltpu.VMEM((nb, TQ), F32), pltpu.VMEM((nb, REP * TQ), F32),
                        pltpu.VMEM((HD, REP * TQ), F32)],
        compiler_params=_cparams("arbitrary", "arbitrary"),
        name="attn_prompt",
    )(rel_bias, qs, qs, qs, gates_t, kcvc, sel_bf, selt_bf,
      win_bf, win_bf, win_bf, win_bf, win_bf, wint_bf, wint_bf, wint_bf, wint_bf, wint_bf, bias_tiles,
      bias_delta_tiles)


def _attn_s1_kernel(tab_ref, qc_ref, kcvc_ref, oct_ref, idx_ref, xc_scr, *, q0):
    g = pl.program_id(1)
    tq = TQ
    pos = q0 + lax.broadcasted_iota(I32, (1, tq), 1)
    oc_t, score = _cmp_branch(_pad_rows(qc_ref[0], tq), kcvc_ref, tab_ref, g, pos, q0, xc_scr,
                              kcvc_ref.shape[0])
    oct_ref[0] = oc_t

    def pick(it, idx, ok):
        idx_ref[0, it:it + 1, :] = jnp.where(ok, idx, -1.0).astype(I32)

    _topk_rows(score, TOPK - 1, pick)
    idx_ref[0, TOPK - 1:TOPK, :] = jnp.full((1, tq), -1, I32)


def _attn_s1(rel_bias, qs8, kcvc, q0):
    batch = qs8.shape[0]
    nb = kcvc.shape[0] // batch
    return pl.pallas_call(
        functools.partial(_attn_s1_kernel, q0=q0),
        grid=(batch, NKV),
        in_specs=[pl.BlockSpec(memory_space=pltpu.SMEM),
                  pl.BlockSpec((1, 8, REP * HD), lambda b, g: (b, 0, g)),
                  pl.BlockSpec((nb, LANES), lambda b, g: (b, g))],
        out_specs=[pl.BlockSpec((1, HD, REP * TQ), lambda b, g: (b * NKV + g, 0, 0)),
                   pl.BlockSpec((1, TOPK, TQ), lambda b, g: (b * NKV + g, 0, 0))],
        out_shape=[jax.ShapeDtypeStruct((batch * NKV, HD, REP * TQ), F32),
                   jax.ShapeDtypeStruct((batch * NKV, TOPK, TQ), I32)],
        scratch_shapes=[pltpu.VMEM((nb, REP * TQ), F32)],
        compiler_params=_cparams("arbitrary", "arbitrary"),
        name="attn_sample_select",
    )(rel_bias, qs8, kcvc)


def _attn_s2_kernel(ids_ref, phys_ref, qs_ref, qw_ref, gt_ref, oct_ref, *refs, q0, nvalid):
    nsel = nvalid * (TOPK - 1)
    kb = refs[:nsel]
    snew_ref, cwin_ref, wnew_ref, bt_ref, o_ref = refs[nsel:]
    b = pl.program_id(0)
    g = pl.program_id(1)
    tq = TQ_SAMPLE
    nr = REP * tq
    tr = lax.broadcasted_iota(I32, (nr, LANES), 0) % tq
    sj = lax.broadcasted_iota(I32, (nr, LANES), 1)
    base = (b * NKV + g) * nvalid * TOPK

    def compact(x):
        return jnp.concatenate([x[:, r * TQ:r * TQ + tq] for r in range(REP)], axis=1)

    def bias_rows(a):
        return jnp.concatenate([bt_ref[a, r][:tq, :] for r in range(REP)], axis=0)

    def attend(q_bf, cached, new_ref):
        xs, vts = [], []
        for tile_t, a, ok in cached:
            xs.append(jnp.where(ok, _dot(q_bf, tile_t) + bias_rows(a), NEG))
            vts.append(tile_t.astype(BF16))
        blk = _pad_rows(new_ref[0], LANES).astype(BF16)
        xs.append(jnp.where((sj <= tr) & (sj < nvalid), _dot_nt(q_bf, blk) + bias_rows(0), NEG))
        x = jnp.concatenate(xs, axis=1)
        p = jnp.exp2(x - jnp.max(x, axis=1, keepdims=True))
        l = jnp.sum(p, axis=1, keepdims=True)
        p = p.astype(BF16)
        nk = x.shape[1] - LANES
        acc = _dot_nt(p[:, :nk], jnp.concatenate(vts, axis=1)) + _dot(p[:, nk:], blk)
        return acc[:, HD:] / l

    cached = []
    for t in range(nvalid):
        for k in range(TOPK - 1):
            n = ids_ref[base + t * TOPK + k]
            nc = jnp.maximum(n, 0)
            a = jnp.clip(q0 // LANES - nc // 2, 0, N_BIAS_TILES - 1)
            lo = jnp.where(n >= 0, (nc % 2) * BLOCK, LANES)
            cached.append((kb[t * (TOPK - 1) + k][0], a, (sj >= lo) & (sj < lo + BLOCK) & (tr == t)))
    o_sel = attend((_stack_heads(_pad_rows(qs_ref[0], tq)) * LOG2E).astype(BF16), cached, snew_ref)

    cached = []
    for j in range(WINDOW // LANES):
        dist = (WINDOW - LANES * j) + tr - sj
        cached.append((cwin_ref[0, :, j * LANES:(j + 1) * LANES], WINDOW // LANES - j,
                       (dist >= 0) & (dist <= WINDOW)))
    o_win = attend((_stack_heads(_pad_rows(qw_ref[0], tq)) * LOG2E).astype(BF16), cached, wnew_ref)

    gt2 = gt_ref.at[0]
    o_cmp = _pad_rows(compact(oct_ref[0]), LANES).T[:, :HD]
    row8 = lax.broadcasted_iota(I32, (8, nr), 0)
    g8 = jnp.zeros((8, nr), F32)
    for br in range(3):
        g8 = jnp.where(row8 == br, compact(_gate_row(gt2, br, g)), g8)
    gates = _pad_rows(g8, LANES).T
    out = gates[:, 0:1] * o_cmp + gates[:, 1:2] * o_sel + gates[:, 2:3] * o_win
    o_ref[0] = jnp.concatenate([out[r * tq:r * tq + 8] for r in range(REP)], axis=1)


def _attn_s2(ids, phys, qs8, gates_t, oct, cache_sel_t, sel_new8, cache_win_t, win_new8, bias_tiles, q0, nvalid):
    batch = qs8.shape[0]
    nsel = nvalid * (TOPK - 1)

    def kb_spec(t, k):
        return pl.BlockSpec((1, LANES, PAGE),
                            lambda b, g, ids_r, phys_r, t=t, k=k:
                            (phys_r[((b * NKV + g) * nvalid + t) * TOPK + k], g, 0))

    def bg(shape, col):
        return pl.BlockSpec(shape, lambda b, g, ids_r, phys_r, col=col: (b, 0, col * NKV + g))

    return pl.pallas_call(
        functools.partial(_attn_s2_kernel, q0=q0, nvalid=nvalid),
        grid_spec=pltpu.PrefetchScalarGridSpec(
            num_scalar_prefetch=2,
            grid=(batch, NKV),
            in_specs=[bg((1, 8, REP * HD), 1), bg((1, 8, REP * HD), 2),
                      pl.BlockSpec((1, LANES, TQ), lambda b, g, ids_r, phys_r: (b, 0, 0)),
                      pl.BlockSpec((1, HD, REP * TQ), lambda b, g, ids_r, phys_r: (b * NKV + g, 0, 0))]
                     + [kb_spec(t, k) for t in range(nvalid) for k in range(TOPK - 1)]
                     + [bg((1, 8, LANES), 0),
                        pl.BlockSpec((1, LANES, WINDOW), lambda b, g, ids_r, phys_r: (b, g, 0)),
                        bg((1, 8, LANES), 0),
                        pl.BlockSpec((N_BIAS_TILES, REP, LANES, TQ), lambda b, g, ids_r, phys_r: (0, g, 0, 0))],
            out_specs=bg((1, 8, REP * HD), 0)),
        out_shape=jax.ShapeDtypeStruct((batch, 8, NKV * REP * HD), F32),
        compiler_params=_cparams("arbitrary", "arbitrary"),
        name="attn_sample",
    )(ids, phys, qs8, qs8, gates_t, oct, *([cache_sel_t] * nsel), sel_new8, cache_win_t, win_new8, bias_tiles)


def _pad_cols(w, n):
    return jnp.pad(w, ((0, 0), (0, n - w.shape[1])))


def _prep(p):
    inner = NH_A * DK_A
    width = NKV * REP * HD
    q = {}
    q['w_in'] = _pad_cols(p['w_in_a'][0], 4 * inner + LANES).astype(BF16)
    q['b_if'] = _pad_cols(p['b_if_a'][0][None, :], LANES)
    q['w_out'] = p['w_out_a'][0].astype(BF16)
    q['w_ff1'] = [p['w_ff1'][l].astype(BF16) for l in range(2)]
    q['w_ff2'] = [p['w_ff2'][l].astype(BF16) for l in range(2)]
    q['w_kv'] = p['w_kv'].astype(BF16)
    q['w_q'] = _pad_cols(p['w_q_b'][0], 3 * width + LANES).astype(BF16)
    q['b_gate'] = _pad_cols(p['b_gate_b'][0][None, :], LANES)
    q['w_o'] = p['w_o_b'][0].astype(BF16)
    q['q_gain'] = (jnp.tile(p['q_norm_b'][0][:, None, :], (1, NKV * REP, 1)) * (HD ** -0.5)).reshape(1, 3 * width)
    seg = np.arange(256) // HD
    q['bd'] = jnp.asarray((seg[:, None] == seg[None, :]).astype(np.float32) / HD, dtype=BF16)
    ones = jnp.ones((HD,), F32)
    q['k_gain'] = jnp.concatenate([jnp.tile(jnp.concatenate([p['k_norm'][br], ones]), NKV) for br in (1, 2)])[None, :]
    q['k_mask'] = jnp.tile(jnp.concatenate([ones, 0.0 * ones]), 2 * NKV)[None, :]
    q['k_gain0'] = jnp.concatenate([p['k_norm'][0], ones])[None, :]
    w1 = p['w_cmp1']
    z = jnp.zeros_like(w1[0])
    per_l = jnp.concatenate([jnp.concatenate([w1[0], z], axis=-1), jnp.concatenate([z, w1[1]], axis=-1)], axis=1)
    q['w_cmp1'] = per_l.reshape(BLOCK // 2, 2 * 2 * HD, 2 * w1.shape[-1]).astype(BF16)
    w2 = p['w_cmp2']
    z2 = jnp.zeros_like(w2[0])
    q['w_cmp2'] = jnp.concatenate([jnp.concatenate([w2[0], z2], axis=1), jnp.concatenate([z2, w2[1]], axis=1)],
                                  axis=0).astype(BF16)
    q['pos_rows'] = jnp.tile(p['cmp_pos'].reshape(BLOCK, 2 * HD), (1, NKV))
    return q


def _feature_major(cache):
    n, t = cache.shape[:2]
    return jnp.transpose(cache, (0, 2, 3, 4, 1)).reshape(n, -1, t)


def _mods(ada_rows, rep):
    k = ada_rows.shape[1] // D_MODEL
    out = []
    for j in range(k):
        a = ada_rows[:, j * D_MODEL:(j + 1) * D_MODEL]
        out.append(jnp.repeat(a, rep, axis=0) if rep > 1 else a)
    return out


def _layer0(x, mods, p, q, mlstm_fn, tm, tmf):
    sh1, sc1, g1, sh2, sc2, g2 = mods
    proj = _proj_plain(x, p['norm_mix'][0][None, :], sh1, sc1, q['w_in'], tm)
    y, states = mlstm_fn(proj)
    x = _out_proj(x, g1, y, q['w_out'], tm)
    x = _ffn(x, p['norm_ffn'][0][None, :], sh2, sc2, g2, q['w_ff1'][0], q['w_ff2'][0], tmf)
    return x, states


def _layer1_tail(x, attn, mods, p, q, tm, tmf):
    _, _, g1, sh2, sc2, g2 = mods
    x = _out_proj(x, g1, attn, q['w_o'], tm)
    return _ffn(x, p['norm_ffn'][1][None, :], sh2, sc2, g2, q['w_ff1'][1], q['w_ff2'][1], tmf)


def kernel(x_prompt, x_sample, cache_cmp, cache_sel, cache_win, state_C, state_n, state_m, page_table,
           c_prompt, c_sample, w_ada, b_ada, norm_mix, norm_ffn, w_ff1, w_ff2, w_in_a, b_if_a,
           head_norm_a, w_out_a, w_ada_kv, b_ada_kv, norm_kv, w_kv, k_norm, cmp_pos, w_cmp1, w_cmp2,
           w_q_b, b_gate_b, q_norm_b, w_o_b, rel_bias):
    p = dict(norm_mix=norm_mix, norm_ffn=norm_ffn, w_ff1=w_ff1, w_ff2=w_ff2, w_in_a=w_in_a, b_if_a=b_if_a,
             w_out_a=w_out_a, w_kv=w_kv, k_norm=k_norm, cmp_pos=cmp_pos, w_cmp1=w_cmp1, w_cmp2=w_cmp2,
             w_q_b=w_q_b, b_gate_b=b_gate_b, q_norm_b=q_norm_b, w_o_b=w_o_b)
    q = _prep(p)
    bp, tp, d = x_prompt.shape
    bs, ts, _ = x_sample.shape
    past = page_table.shape[1] * PAGE
    wbuf = cache_win.shape[1]
    assert bp == 1 and ts < BLOCK and ts <= 8 and wbuf == WINDOW and past % LANES == 0
    assert tp % (SUPER * SUPERS_PER_STEP) == 0
    width = NKV * REP * HD
    kvw = NKV * 2 * HD

    nc = bp + bs
    c_all = jnp.pad(jnp.concatenate([c_prompt, c_sample], axis=0), ((0, -nc % 8), (0, 0)))
    ada = [_ada(c_all, w_ada[l], b_ada[l]) for l in range(2)]
    ada_kv = _ada(c_all, w_ada_kv, b_ada_kv)
    bias_kq, bias_kq_delta, bias_qk = _bias_tiles(rel_bias)
    head_gain = head_norm_a[0]

    xp = x_prompt.reshape(tp, d)
    zc = jnp.zeros((bp, NH_A, DK_A, DK_A), F32)
    zn = jnp.zeros((bp, NH_A, 1, DK_A), F32)
    zm = jnp.zeros((bp, 8, LANES), F32)

    def mlstm_prompt(proj):
        y, c, n, m = _mlstm(proj.reshape(bp, tp, -1), q['b_if'], head_gain, zc, zn, zm,
                            rows=256, valid=256, chunk=256, zero_init=True)
        return y.reshape(tp, -1), (c, n, m)

    xp, (pc, pn, pm) = _layer0(xp, _mods(ada[0][:bp], 1), p, q, mlstm_prompt, 256, 512)
    sh, sc = _mods(ada_kv[:bp], 1)
    p_cmp, _, _, p_cmp_t, p_sel_t, p_win_t, sel_bf, selt_bf, win_bf, wint_bf = _proj_kv(
        xp, norm_kv[None, :], sh, sc, q['w_kv'], q['bd'], q['k_gain'], q['k_mask'], 256)
    kcvc_p = _compress(p_cmp, q['pos_rows'], q['w_cmp1'], q['w_cmp2'], q['k_gain0'],
                       min(CMP_BLOCKS_PER_STEP, tp // BLOCK))
    mods1 = _mods(ada[1][:bp], 1)
    qs_p, gt_p = _proj_q(xp, norm_mix[1][None, :], mods1[0], mods1[1], q['w_q'], q['bd'], q['q_gain'],
                         q['b_gate'], 256)
    attn_p = _attn_prompt(rel_bias, qs_p, gt_p, kcvc_p, sel_bf, selt_bf, win_bf, wint_bf, bias_kq, bias_kq_delta)
    y_prompt = _layer1_tail(xp, attn_p, mods1, p, q, 256, 512).reshape(bp, tp, d)

    ms = bs * ts
    xs = x_sample.reshape(ms, d)
    m0 = jnp.broadcast_to(jnp.pad(state_m[0], ((0, 0), (0, 8 - NH_A)))[:, :, None], (bs, 8, LANES))

    def mlstm_sample(proj):
        proj8 = jnp.pad(proj.reshape(bs, ts, -1), ((0, 0), (0, 8 - ts), (0, 0)))
        y, c, n, m = _mlstm(proj8, q['b_if'], head_gain, state_C[0], state_n[0][:, :, None, :], m0,
                            rows=8, valid=ts, chunk=LANES, zero_init=False)
        return y[:, :ts].reshape(ms, -1), (c, n, m)

    xs, (sc_, sn_, sm_) = _layer0(xs, _mods(ada[0][bp:nc], ts), p, q, mlstm_sample, ms, ms)
    sh, sc = _mods(ada_kv[bp:nc], ts)
    s_cmp, s_sel, s_win = _proj_kv(xs, norm_kv[None, :], sh, sc, q['w_kv'], q['bd'], q['k_gain'],
                                   q['k_mask'], ms)[:3]
    kcvc_s = _compress_paged(_feature_major(cache_cmp), page_table, q['pos_rows'],
                             q['w_cmp1'], q['w_cmp2'], q['k_gain0'], min(CMP_BLOCKS_PER_STEP, past // BLOCK))
    mods1 = _mods(ada[1][bp:nc], ts)
    qs_s, gt_s = _proj_q(xs, norm_mix[1][None, :], mods1[0], mods1[1], q['w_q'], q['bd'], q['q_gain'],
                         q['b_gate'], ms)

    def pad8(a):
        return jnp.pad(a.reshape(bs, ts, -1), ((0, 0), (0, 8 - ts), (0, 0)))

    qs8 = pad8(qs_s)
    oct_s, idx_s = _attn_s1(rel_bias, qs8, kcvc_s, past)
    ids = jnp.transpose(idx_s.reshape(bs, NKV, TOPK, TQ)[:, :, :, :ts], (0, 1, 3, 2))
    idc = jnp.maximum(ids, 0)
    pages = jnp.take_along_axis(page_table, (idc // (PAGE // BLOCK)).reshape(bs, -1), axis=1).reshape(ids.shape)
    gt8 = jnp.pad(jnp.transpose(gt_s.reshape(LANES, bs, ts), (1, 0, 2)), ((0, 0), (0, 0), (0, TQ - ts)))
    attn_s = _attn_s2(ids.reshape(-1), pages.reshape(-1), qs8, gt8, oct_s,
                      _feature_major(cache_sel), pad8(s_sel), _feature_major(cache_win), pad8(s_win),
                      bias_qk, past, ts)
    y_sample = _layer1_tail(xs, attn_s[:, :ts].reshape(ms, width), mods1, p, q, ms, ms).reshape(bs, ts, d)

    rows5 = (NKV, 2, HD)

    def token_major(a_t):
        return jnp.transpose(a_t.reshape(*rows5, a_t.shape[1]), (3, 0, 1, 2))[None]

    s_win_all = jnp.concatenate([cache_win, s_win.reshape(bs, ts, *rows5)], axis=1)
    return (y_prompt, y_sample,
            pc[None], pn.reshape(1, bp, NH_A, DK_A), pm[None, :, :NH_A, 0],
            token_major(p_cmp_t), token_major(p_sel_t), token_major(p_win_t[:, tp - min(WINDOW, tp):]),
            sc_[None], sn_.reshape(1, bs, NH_A, DK_A), sm_[None, :, :NH_A, 0],
            s_cmp.reshape(bs, ts, *rows5), s_sel.reshape(bs, ts, *rows5),
            s_win_all[:, -min(WINDOW, wbuf + ts):])
```
